```python
import jax
import jax.numpy as jnp
from jax import lax
import numpy as np

D_MODEL = 1024
BATCH = 16
SEQ = 256
DEPTH = 4
DEC_BATCH = 4
DEC_SEQ = 1024
PAST_LEN = 256

GRID_W = 64
N_MIXERS = 3
N_MLA_LAYERS = (DEPTH + 2) // 3
N_GMLP_LAYERS = (DEPTH + 1) // 3
N_SWA_LAYERS = DEPTH // 3
ALPHA = (2.0 * DEPTH) ** 0.25
BETA = (8.0 * DEPTH) ** -0.25
MLA_HEADS = 8
MLA_Q_LORA = 384
MLA_KV_LORA = 256
MLA_NOPE = 128
MLA_ROPE = 64
MLA_V = 128
GM_FF = 6 * D_MODEL
GM_HALF = GM_FF // 2
GM_GROUPS = 8
CHUNK = 128
SWA_HEADS = 16
SWA_KV_HEADS = 4
SWA_HEAD_DIM = 64
WINDOW = 128
BAND = 128
N_EXPERTS = 16
EXPERT_FF = 2 * D_MODEL
EC_CAPACITY = 2
Q_BLOCK = 128
ROPE_BASE = 10000.0
NEG_INF = -1e30
EPS = 1e-6
F32 = jnp.float32

kernel_name = 'hybrid_mla_gmlp_swa_ecmoe_diffusion_step'


def rms_norm(x, gain):
    x32 = x.astype(F32)
    y = x32 * lax.rsqrt(jnp.mean(x32 * x32, axis=-1, keepdims=True) + EPS)
    return y.astype(x.dtype) * gain


def layer_norm(x, gain, bias=None):
    x32 = x.astype(F32)
    mu = jnp.mean(x32, axis=-1, keepdims=True)
    var = jnp.mean(jnp.square(x32 - mu), axis=-1, keepdims=True)
    y = ((x32 - mu) * lax.rsqrt(var + EPS)).astype(x.dtype) * gain
    return y if bias is None else y + bias


def modulation(cond, w, b):
    m = jax.nn.silu(cond) @ w + b
    return jnp.split(m[:, None, :], 6, axis=-1)


def axial_rope(x):
    L, d = x.shape[1], x.shape[-1]
    rows = L // GRID_W
    t_row = jnp.repeat(jnp.arange(rows), GRID_W).astype(F32)
    t_col = jnp.tile(jnp.arange(GRID_W), rows).astype(F32)
    n_freq = d // 4
    inv_freq = ROPE_BASE ** (-jnp.arange(n_freq, dtype=F32) / n_freq)

    def rot(xa, t):
        ang = t[:, None] * inv_freq[None, :]
        cos = jnp.cos(ang)[None, :, None, :]
        sin = jnp.sin(ang)[None, :, None, :]
        x1, x2 = jnp.split(xa, 2, axis=-1)
        return jnp.concatenate([x1 * cos - x2 * sin, x2 * cos + x1 * sin], axis=-1)

    xr, xc = jnp.split(x.astype(F32), 2, axis=-1)
    return jnp.concatenate([rot(xr, t_row), rot(xc, t_col)], axis=-1).astype(x.dtype)


def dense_attn(q, k, v, scale, sink=None):
    B, Lq, Hk, G, dk = q.shape
    nq = Lq // Q_BLOCK
    qb = jnp.moveaxis(q.reshape(B, nq, Q_BLOCK, Hk, G, dk), 1, 0)

    def one_block(qi):
        s = jnp.einsum('bqhgd,bkhd->bhgqk', qi, k).astype(F32) * scale
        if sink is not None:
            sk = jnp.broadcast_to(sink.astype(F32)[None, :, :, None, None], s.shape[:-1] + (1,))
            p = jax.nn.softmax(jnp.concatenate([s, sk], axis=-1), axis=-1)[..., :-1]
        else:
            p = jax.nn.softmax(s, axis=-1)
        return jnp.einsum('bhgqk,bkhd->bqhgd', p.astype(v.dtype), v)

    o = lax.map(one_block, qb)
    return jnp.moveaxis(o, 0, 1).reshape(B, Lq, Hk * G * v.shape[-1])


def window_attn(q, k, v, k_ctx, v_ctx, sink, scale):
    B, L, Hk, G, d = q.shape
    Lc = k_ctx.shape[1]
    nb = L // BAND
    qb = q.reshape(B, nb, BAND, Hk, G, d)

    def band(t):
        tp = jnp.pad(t, ((0, 0), (BAND, BAND), (0, 0), (0, 0))).reshape(B, nb + 2, BAND, Hk, t.shape[-1])
        return jnp.concatenate([tp[:, :-2], tp[:, 1:-1], tp[:, 2:]], axis=2)

    kb, vb = band(k), band(v)
    qi = jnp.arange(BAND)[:, None]
    kj = jnp.arange(3 * BAND)[None, :]
    rel = kj - BAND - qi
    kpos = jnp.arange(nb)[:, None, None] * BAND + kj[None] - BAND
    valid = (jnp.abs(rel) <= WINDOW)[None] & (kpos >= 0) & (kpos < L)

    s_loc = jnp.einsum('bnqhgd,bnkhd->bnhgqk', qb, kb).astype(F32) * scale
    s_loc = jnp.where(valid[None, :, None, None], s_loc, NEG_INF)
    s_ctx = jnp.einsum('bnqhgd,bchd->bnhgqc', qb, k_ctx).astype(F32) * scale
    s_sink = jnp.broadcast_to(sink.astype(F32)[None, None, :, :, None, None], s_loc.shape[:-1] + (1,))
    p = jax.nn.softmax(jnp.concatenate([s_loc, s_ctx, s_sink], axis=-1), axis=-1)
    p_loc = p[..., :3 * BAND].astype(v.dtype)
    p_ctx = p[..., 3 * BAND:3 * BAND + Lc].astype(v.dtype)
    o = (jnp.einsum('bnhgqk,bnkhd->bnqhgd', p_loc, vb)
         + jnp.einsum('bnhgqc,bchd->bnqhgd', p_ctx, v_ctx))
    return o.reshape(B, L, Hk * G * d)


def mla_project(h, w_in, q_gain, kv_gain, w_q_up):
    z = h @ w_in
    c_q, c_kv, k_pe = jnp.split(z, [MLA_Q_LORA, MLA_Q_LORA + MLA_KV_LORA], axis=-1)
    q = jnp.einsum('blr,rhd->blhd', rms_norm(c_q, q_gain), w_q_up)
    return q, rms_norm(c_kv, kv_gain), k_pe


def mla_keys_values(c_kv, k_pe, w_kv_up):
    kv = jnp.einsum('blr,rhd->blhd', c_kv, w_kv_up)
    k_nope, v = jnp.split(kv, [MLA_NOPE], axis=-1)
    k_rope = jnp.broadcast_to(k_pe[:, :, None, :], k_nope.shape[:3] + (MLA_ROPE,))
    return jnp.concatenate([k_nope, k_rope], axis=-1), v


def mla_context(h, w_in, q_gain, kv_gain, w_q_up, w_kv_up, w_out):
    q, c_kv, k_pe = mla_project(h, w_in, q_gain, kv_gain, w_q_up)
    k, v = mla_keys_values(c_kv, k_pe, w_kv_up)
    o = dense_attn(q[:, :, :, None, :], k, v, (MLA_NOPE + MLA_ROPE) ** -0.5)
    return o @ w_out, c_kv, k_pe


def mla_latent(h, ckv_ctx, kpe_ctx, w_in, q_gain, kv_gain, w_q_up, w_kv_up, w_out):
    q, c_kv, k_pe = mla_project(h, w_in, q_gain, kv_gain, w_q_up)
    q = jnp.concatenate([q[..., :MLA_NOPE], axial_rope(q[..., MLA_NOPE:])], axis=-1)
    k_pe = axial_rope(k_pe[:, :, None, :])[:, :, 0, :]
    k_lat, v_lat = mla_keys_values(c_kv, k_pe, w_kv_up)
    k_ctx, v_ctx = mla_keys_values(ckv_ctx, kpe_ctx, w_kv_up)
    k = jnp.concatenate([k_ctx, k_lat], axis=1)
    v = jnp.concatenate([v_ctx, v_lat], axis=1)
    o = dense_attn(q[:, :, :, None, :], k, v, (MLA_NOPE + MLA_ROPE) ** -0.5)
    return o @ w_out


def gmlp(h, w_in, v_gain, w_s, b_s, w_out):
    B, L, _ = h.shape
    z = jax.nn.gelu(h @ w_in)
    u, v = jnp.split(z, 2, axis=-1)
    v = layer_norm(v, v_gain)
    nc = L // CHUNK
    v = v.reshape(B, nc, CHUNK, GM_GROUPS, GM_HALF // GM_GROUPS)
    mixed = jnp.einsum('gnm,bkmgc->bkngc', w_s, v) + b_s.T[None, None, :, :, None]
    return (u * mixed.reshape(B, L, GM_HALF)) @ w_out


def swa_project(h, w_qkv):
    B, L, _ = h.shape
    qkv = h @ w_qkv
    q, k, v = jnp.split(qkv, [SWA_HEADS * SWA_HEAD_DIM, (SWA_HEADS + SWA_KV_HEADS) * SWA_HEAD_DIM], axis=-1)
    return (q.reshape(B, L, SWA_HEADS, SWA_HEAD_DIM),
            k.reshape(B, L, SWA_KV_HEADS, SWA_HEAD_DIM),
            v.reshape(B, L, SWA_KV_HEADS, SWA_HEAD_DIM))


def swa_context(h, w_qkv, sink, w_out):
    B, L, _ = h.shape
    q, k, v = swa_project(h, w_qkv)
    G = SWA_HEADS // SWA_KV_HEADS
    o = dense_attn(q.reshape(B, L, SWA_KV_HEADS, G, SWA_HEAD_DIM), k, v,
                   SWA_HEAD_DIM ** -0.5, sink.reshape(SWA_KV_HEADS, G))
    return o @ w_out, k, v


def swa_latent(h, k_ctx, v_ctx, w_qkv, sink, w_out):
    B, L, _ = h.shape
    q, k, v = swa_project(h, w_qkv)
    q, k = axial_rope(q), axial_rope(k)
    G = SWA_HEADS // SWA_KV_HEADS
    o = window_attn(q.reshape(B, L, SWA_KV_HEADS, G, SWA_HEAD_DIM), k, v, k_ctx, v_ctx,
                    sink.reshape(SWA_KV_HEADS, G), SWA_HEAD_DIM ** -0.5)
    return o @ w_out


def ec_moe(h, router, w_gate, w_up, w_down):
    x = h.reshape(-1, h.shape[-1])
    n_tok = x.shape[0]
    cap = EC_CAPACITY * n_tok // N_EXPERTS
    aff = jax.nn.softmax((x @ router).astype(F32), axis=-1)
    gate, idx = lax.top_k(aff.T, cap)
    xe = x[idx]
    hid = jax.nn.silu(jnp.einsum('ecd,edf->ecf', xe, w_gate)) * jnp.einsum('ecd,edf->ecf', xe, w_up)
    ye = jnp.einsum('ecf,efd->ecd', hid, w_down) * gate[..., None].astype(x.dtype)
    out = jnp.zeros_like(x).at[idx.reshape(-1)].add(ye.reshape(-1, x.shape[-1]))
    return out.reshape(h.shape)


def setup_inputs(seed: int = 0) -> dict:
    key = jax.random.key(seed)
    keys = jax.random.split(key, 40)
    counter = iter(range(40))

    def nrm(shape, scale=1.0):
        return jax.random.normal(keys[next(counter)], shape, F32) * scale

    d = D_MODEL
    return {
        'x_prompt': nrm((BATCH, SEQ, d)),
        'x_sample': nrm((DEC_BATCH, DEC_SEQ, d)),
        'cache_mla_ckv': nrm((DEC_BATCH, N_MLA_LAYERS, PAST_LEN, MLA_KV_LORA)),
        'cache_mla_kpe': nrm((DEC_BATCH, N_MLA_LAYERS, PAST_LEN, MLA_ROPE)),
        'cache_swa_k': nrm((DEC_BATCH, N_SWA_LAYERS, PAST_LEN, SWA_KV_HEADS, SWA_HEAD_DIM)),
        'cache_swa_v': nrm((DEC_BATCH, N_SWA_LAYERS, PAST_LEN, SWA_KV_HEADS, SWA_HEAD_DIM)),
        'c': nrm((DEC_BATCH, d)),
        'c_ctx': nrm((d,)),
        'mod_w': nrm((DEPTH, d, 6 * d), 0.5 * d ** -0.5),
        'mod_b': nrm((DEPTH, 6 * d), 0.02),
        'ln_gain': 1.0 + nrm((DEPTH, 2, d), 0.02),
        'ln_bias': nrm((DEPTH, 2, d), 0.02),
        'mla_w_in': nrm((N_MLA_LAYERS, d, MLA_Q_LORA + MLA_KV_LORA + MLA_ROPE), d ** -0.5),
        'mla_q_gain': 1.0 + nrm((N_MLA_LAYERS, MLA_Q_LORA), 0.02),
        'mla_kv_gain': 1.0 + nrm((N_MLA_LAYERS, MLA_KV_LORA), 0.02),
        'mla_w_q_up': nrm((N_MLA_LAYERS, MLA_Q_LORA, MLA_HEADS, MLA_NOPE + MLA_ROPE), MLA_Q_LORA ** -0.5),
        'mla_w_kv_up': nrm((N_MLA_LAYERS, MLA_KV_LORA, MLA_HEADS, MLA_NOPE + MLA_V), MLA_KV_LORA ** -0.5),
        'mla_w_out': nrm((N_MLA_LAYERS, MLA_HEADS * MLA_V, d), BETA * (MLA_HEADS * MLA_V) ** -0.5),
        'gm_w_in': nrm((N_GMLP_LAYERS, d, GM_FF), d ** -0.5),
        'gm_v_gain': 1.0 + nrm((N_GMLP_LAYERS, GM_HALF), 0.02),
        'gm_w_s': nrm((N_GMLP_LAYERS, GM_GROUPS, CHUNK, CHUNK), 0.5 * CHUNK ** -0.5),
        'gm_b_s': 1.0 + nrm((N_GMLP_LAYERS, GM_GROUPS, CHUNK), 0.1),
        'gm_w_out': nrm((N_GMLP_LAYERS, GM_HALF, d), BETA * GM_HALF ** -0.5),
        'swa_w_qkv': nrm((N_SWA_LAYERS, d, (SWA_HEADS + 2 * SWA_KV_HEADS) * SWA_HEAD_DIM), d ** -0.5),
        'swa_sink': nrm((N_SWA_LAYERS, SWA_HEADS), 0.5),
        'swa_w_out': nrm((N_SWA_LAYERS, SWA_HEADS * SWA_HEAD_DIM, d), BETA * (SWA_HEADS * SWA_HEAD_DIM) ** -0.5),
        'moe_router': nrm((DEPTH, d, N_EXPERTS), d ** -0.5),
        'moe_w_gate': nrm((DEPTH, N_EXPERTS, d, EXPERT_FF), d ** -0.5),
        'moe_w_up': nrm((DEPTH, N_EXPERTS, d, EXPERT_FF), d ** -0.5),
        'moe_w_down': nrm((DEPTH, N_EXPERTS, EXPERT_FF, d), BETA * EXPERT_FF ** -0.5),
    }


def reference(x_prompt, x_sample, cache_mla_ckv, cache_mla_kpe, cache_swa_k, cache_swa_v, c, c_ctx,
              mod_w, mod_b, ln_gain, ln_bias,
              mla_w_in, mla_q_gain, mla_kv_gain, mla_w_q_up, mla_w_kv_up, mla_w_out,
              gm_w_in, gm_v_gain, gm_w_s, gm_b_s, gm_w_out,
              swa_w_qkv, swa_sink, swa_w_out,
              moe_router, moe_w_gate, moe_w_up, moe_w_down):
    xp, xs = x_prompt, x_sample
    cond_ctx = c_ctx[None, :]
    ckv_list, kpe_list, k_list, v_list = [], [], [], []
    for i in range(DEPTH):
        kind, j = i % N_MIXERS, i // N_MIXERS
        mp = modulation(cond_ctx, mod_w[i], mod_b[i])
        ms = modulation(c, mod_w[i], mod_b[i])
        hp = xp * (1.0 + mp[1]) + mp[0]
        hs = xs * (1.0 + ms[1]) + ms[0]
        if kind == 0:
            yp, ckv, kpe = mla_context(hp, mla_w_in[j], mla_q_gain[j], mla_kv_gain[j],
                                       mla_w_q_up[j], mla_w_kv_up[j], mla_w_out[j])
            ys = mla_latent(hs, cache_mla_ckv[:, j], cache_mla_kpe[:, j], mla_w_in[j], mla_q_gain[j],
                            mla_kv_gain[j], mla_w_q_up[j], mla_w_kv_up[j], mla_w_out[j])
            ckv_list.append(ckv)
            kpe_list.append(kpe)
        elif kind == 1:
            yp = gmlp(hp, gm_w_in[j], gm_v_gain[j], gm_w_s[j], gm_b_s[j], gm_w_out[j])
            ys = gmlp(hs, gm_w_in[j], gm_v_gain[j], gm_w_s[j], gm_b_s[j], gm_w_out[j])
        else:
            yp, k_c, v_c = swa_context(hp, swa_w_qkv[j], swa_sink[j], swa_w_out[j])
            ys = swa_latent(hs, cache_swa_k[:, j], cache_swa_v[:, j], swa_w_qkv[j], swa_sink[j], swa_w_out[j])
            k_list.append(k_c)
            v_list.append(v_c)
        xp = layer_norm(ALPHA * xp + mp[2] * yp, ln_gain[i, 0], ln_bias[i, 0])
        xs = layer_norm(ALPHA * xs + ms[2] * ys, ln_gain[i, 0], ln_bias[i, 0])
        hp = xp * (1.0 + mp[4]) + mp[3]
        hs = xs * (1.0 + ms[4]) + ms[3]
        yp = ec_moe(hp, moe_router[i], moe_w_gate[i], moe_w_up[i], moe_w_down[i])
        ys = ec_moe(hs, moe_router[i], moe_w_gate[i], moe_w_up[i], moe_w_down[i])
        xp = layer_norm(ALPHA * xp + mp[5] * yp, ln_gain[i, 1], ln_bias[i, 1])
        xs = layer_norm(ALPHA * xs + ms[5] * ys, ln_gain[i, 1], ln_bias[i, 1])
    new_mla_ckv = jnp.stack(ckv_list, axis=1)
    new_mla_kpe = jnp.stack(kpe_list, axis=1)
    new_swa_k = jnp.stack(k_list, axis=1)
    new_swa_v = jnp.stack(v_list, axis=1)
    return (xp, xs, new_mla_ckv, new_mla_kpe, new_swa_k, new_swa_v)
```

```python
import functools

import jax
import jax.numpy as jnp
from jax import lax
from jax.experimental import pallas as pl
from jax.experimental.pallas import tpu as pltpu

D_MODEL = 1024
N_TOK = 4096
DEPTH = 4
DEC_BATCH = 4
DEC_SEQ = 1024
PAST_LEN = 256
CTX_BATCH = 16
CTX_SEQ = 256
GRID_W = 64
ALPHA = (2.0 * DEPTH) ** 0.25
MLA_HEADS = 8
MLA_Q_LORA = 384
MLA_KV_LORA = 256
MLA_NOPE = 128
MLA_ROPE = 64
MLA_V = 128
MLA_QK_PAD = 256
GM_FF = 6 * D_MODEL
GM_HALF = GM_FF // 2
GM_GROUPS = 8
GM_GW = GM_HALF // GM_GROUPS
CHUNK = 128
SWA_HEADS = 16
SWA_KV_HEADS = 4
SWA_HEAD_DIM = 64
WINDOW = 128
BAND = 128
N_EXPERTS = 16
EXPERT_FF = 2 * D_MODEL
CAP = 2 * N_TOK // N_EXPERTS
ROPE_BASE = 10000.0
NEG_INF = -1e30
EPS = 1e-6
F32 = jnp.float32
BF16 = jnp.bfloat16
I32 = jnp.int32
U32 = jnp.uint32
HIGHEST = lax.Precision.HIGHEST

LANES = 128
VMEM_LIMIT = 56 * 1024 * 1024
TM = 256
FF_TILE = 512
TOK_BLK = 128
MOD_TN = 1536

F_POS_HI, F_POS_LO, F_G0, F_G1, F_G2 = 0, 16, 32, 48, 64
F_T_HI, F_T_LO, F_OFF_HI, F_OFF_LO, F_CNT = 80, 81, 82, 83, 84


def _params(sem, vmem=VMEM_LIMIT):
    return pltpu.CompilerParams(dimension_semantics=sem, vmem_limit_bytes=vmem)


def _dot(a, b):
    return jnp.dot(a, b, preferred_element_type=F32)


def _dot_nt(a, b):
    return lax.dot_general(a, b, (((1,), (1,)), ((), ())), preferred_element_type=F32)


def _sigmoid(x):
    return 1.0 / (1.0 + jnp.exp(-x))


def _gelu_tanh(x):
    return 0.5 * x * (1.0 + jnp.tanh(0.7978845608028654 * (x + 0.044715 * (x * x * x))))


def _layer_norm(t, gain, bias):
    mu = jnp.mean(t, axis=-1, keepdims=True)
    d = t - mu
    var = jnp.mean(d * d, axis=-1, keepdims=True)
    return d * lax.rsqrt(var + EPS) * gain + bias


def _rms_norm(t, gain):
    return t * lax.rsqrt(jnp.mean(t * t, axis=-1, keepdims=True) + EPS) * gain


def _rope128(x, cos, sin):
    lane = lax.broadcasted_iota(I32, x.shape, 1)
    first = (lane % 32) < 16
    swapped = jnp.where(first, -pltpu.roll(x, 112, 1), pltpu.roll(x, 16, 1))
    return x * cos + swapped * sin


def _pack2(lo, hi):
    def rnd(v):
        b = lax.bitcast_convert_type(v, U32)
        return b + jnp.uint32(0x7FFF) + ((b >> 16) & jnp.uint32(1))
    return (rnd(lo) >> 16) | (rnd(hi) & jnp.uint32(0xFFFF0000))


def _unpack2(w):
    lo = lax.bitcast_convert_type(w << 16, F32)
    hi = lax.bitcast_convert_type(w & jnp.uint32(0xFFFF0000), F32)
    return lo, hi


def _mod_kernel(cond_ref, w_ref, b_ref, o_ref):
    c = cond_ref[...]
    s = c * _sigmoid(c)
    o_ref[0] = jnp.dot(s, w_ref[0], precision=HIGHEST, preferred_element_type=F32) + b_ref[0]


def _modulation_all(cond8, mod_w, mod_b):
    return pl.pallas_call(
        _mod_kernel,
        grid=(DEPTH, 6 * D_MODEL // MOD_TN),
        in_specs=[pl.BlockSpec((8, D_MODEL), lambda i, n: (0, 0)),
                  pl.BlockSpec((1, D_MODEL, MOD_TN), lambda i, n: (i, 0, n)),
                  pl.BlockSpec((1, 1, MOD_TN), lambda i, n: (i, 0, n))],
        out_specs=pl.BlockSpec((1, 8, MOD_TN), lambda i, n: (i, 0, n)),
        out_shape=jax.ShapeDtypeStruct((DEPTH, 8, 6 * D_MODEL), F32),
        compiler_params=_params(("arbitrary", "arbitrary")),
        name="modulation",
    )(cond8, mod_w, mod_b.reshape(DEPTH, 1, 6 * D_MODEL))


def _seg_spec(seg_len):
    return pl.BlockSpec((1, 1, D_MODEL), lambda r: (r * TM // seg_len, 0, 0))


def _row_spec(width):
    return pl.BlockSpec((TM, width), lambda r: (r, 0))


def _full_spec(shape):
    nd = len(shape)
    return pl.BlockSpec(shape, lambda r: (0,) * nd)


def _mla_proj_kernel(rope, x_ref, sc_ref, sh_ref, win_ref, qg_ref, kvg_ref, wq_ref, wkv_ref, *rest):
    if rope:
        cos_ref, sin_ref, q_ref, kv_ref, kpe_ref, ckv_ref, kraw_ref = rest
    else:
        q_ref, kv_ref, kpe_ref, ckv_ref, kraw_ref = rest
    h = (x_ref[...] * (1.0 + sc_ref[0]) + sh_ref[0]).astype(BF16)
    z = _dot(h, win_ref[...])
    cq = _rms_norm(z[:, :MLA_Q_LORA], qg_ref[...])
    ckv = _rms_norm(z[:, MLA_Q_LORA:MLA_Q_LORA + MLA_KV_LORA], kvg_ref[...])
    kp = z[:, MLA_Q_LORA + MLA_KV_LORA:]
    ckv_ref[...] = ckv
    kraw_ref[...] = kp[:, :MLA_ROPE]
    q = _dot(cq.astype(BF16), wq_ref[...])
    kv_ref[...] = _dot(ckv.astype(BF16), wkv_ref[...]).astype(BF16)
    if rope:
        cos, sin = cos_ref[...], sin_ref[...]
        for hd in range(MLA_HEADS):
            a = hd * MLA_QK_PAD
            q_ref[:, a:a + LANES] = q[:, a:a + LANES].astype(BF16)
            q_ref[:, a + LANES:a + 2 * LANES] = _rope128(q[:, a + LANES:a + 2 * LANES], cos, sin).astype(BF16)
        kp = _rope128(kp, cos, sin)
    else:
        q_ref[...] = q.astype(BF16)
    kpe_ref[...] = kp.astype(BF16)


def _mla_proj(x, sc, sh, w, seg_len, rope_tabs):
    rope = rope_tabs is not None
    in_specs = [_row_spec(D_MODEL), _seg_spec(seg_len), _seg_spec(seg_len),
                _full_spec(w["w_in"].shape), _full_spec((1, MLA_Q_LORA)), _full_spec((1, MLA_KV_LORA)),
                _full_spec(w["w_q"].shape), _full_spec(w["w_kv"].shape)]
    args = [x, sc, sh, w["w_in"], w["q_gain"], w["kv_gain"], w["w_q"], w["w_kv"]]
    if rope:
        nblk = DEC_SEQ // TM
        tab_spec = pl.BlockSpec((TM, LANES), lambda r: (r % nblk, 0))
        in_specs += [tab_spec, tab_spec]
        args += list(rope_tabs)
    wq = MLA_HEADS * MLA_QK_PAD
    return pl.pallas_call(
        functools.partial(_mla_proj_kernel, rope),
        grid=(N_TOK // TM,),
        in_specs=in_specs,
        out_specs=[_row_spec(wq), _row_spec(wq), _row_spec(LANES), _row_spec(MLA_KV_LORA), _row_spec(MLA_ROPE)],
        out_shape=[jax.ShapeDtypeStruct((N_TOK, wq), BF16), jax.ShapeDtypeStruct((N_TOK, wq), BF16),
                   jax.ShapeDtypeStruct((N_TOK, LANES), BF16), jax.ShapeDtypeStruct((N_TOK, MLA_KV_LORA), F32),
                   jax.ShapeDtypeStruct((N_TOK, MLA_ROPE), F32)],
        compiler_params=_params(("arbitrary",)),
        name="mla_proj",
    )(*args)


def _mm_kernel(a_ref, w_ref, o_ref):
    o_ref[...] = _dot(a_ref[...], w_ref[...]).astype(o_ref.dtype)


def _matmul_bf16(a, w):
    m, k = a.shape
    n = w.shape[1]
    return pl.pallas_call(
        _mm_kernel,
        grid=(m // TM,),
        in_specs=[pl.BlockSpec((TM, k), lambda r: (r, 0)), _full_spec(w.shape)],
        out_specs=pl.BlockSpec((TM, n), lambda r: (r, 0)),
        out_shape=jax.ShapeDtypeStruct((m, n), BF16),
        compiler_params=_params(("arbitrary",)),
        name="matmul",
    )(a, w)


MLA_SCALE = (MLA_NOPE + MLA_ROPE) ** -0.5


def _mla_attn_ctx_kernel(q_ref, kv_ref, kpe_ref, o_ref):
    kpe = kpe_ref[...]
    for hd in range(MLA_HEADS):
        a = hd * MLA_QK_PAD
        qh = q_ref[:, a:a + MLA_QK_PAD]
        kh = jnp.concatenate([kv_ref[:, a:a + LANES], kpe], axis=1)
        vh = kv_ref[:, a + LANES:a + 2 * LANES]
        s = _dot_nt(qh, kh) * MLA_SCALE
        p = jnp.exp(s - jnp.max(s, axis=-1, keepdims=True))
        l = jnp.sum(p, axis=-1, keepdims=True)
        o_ref[:, hd * MLA_V:(hd + 1) * MLA_V] = (_dot(p.astype(BF16), vh) / l).astype(BF16)


def _mla_attn_ctx(q, kv, kpe):
    wq = MLA_HEADS * MLA_QK_PAD
    blk = lambda w: pl.BlockSpec((CTX_SEQ, w), lambda b: (b, 0))
    return pl.pallas_call(
        _mla_attn_ctx_kernel,
        grid=(CTX_BATCH,),
        in_specs=[blk(wq), blk(wq), blk(LANES)],
        out_specs=blk(MLA_HEADS * MLA_V),
        out_shape=jax.ShapeDtypeStruct((N_TOK, MLA_HEADS * MLA_V), BF16),
        compiler_params=_params(("arbitrary",)),
        name="mla_attn_ctx",
    )(q, kv, kpe)


def _mla_attn_lat_kernel(q_ref, kvl_ref, kpel_ref, kvc_ref, kpec_ref, o_ref):
    kpel = kpel_ref[...]
    kpec = kpec_ref[...]
    for hd in range(MLA_HEADS):
        a = hd * MLA_QK_PAD
        qh = q_ref[:, a:a + MLA_QK_PAD]
        kc = jnp.concatenate([kvc_ref[:, a:a + LANES], kpec], axis=1)
        kl = jnp.concatenate([kvl_ref[:, a:a + LANES], kpel], axis=1)
        sc = _dot_nt(qh, kc) * MLA_SCALE
        sl = _dot_nt(qh, kl) * MLA_SCALE
        m = jnp.maximum(jnp.max(sc, axis=-1, keepdims=True), jnp.max(sl, axis=-1, keepdims=True))
        pc = jnp.exp(sc - m)
        plat = jnp.exp(sl - m)
        l = jnp.sum(pc, axis=-1, keepdims=True) + jnp.sum(plat, axis=-1, keepdims=True)
        o = _dot(pc.astype(BF16), kvc_ref[:, a + LANES:a + 2 * LANES]) \
            + _dot(plat.astype(BF16), kvl_ref[:, a + LANES:a + 2 * LANES])
        o_ref[:, hd * MLA_V:(hd + 1) * MLA_V] = (o / l).astype(BF16)


def _mla_attn_lat(q, kv, kpe, kv_ctx, kpe_ctx):
    wq = MLA_HEADS * MLA_QK_PAD
    nq = DEC_SEQ // TM
    return pl.pallas_call(
        _mla_attn_lat_kernel,
        grid=(DEC_BATCH, nq),
        in_specs=[pl.BlockSpec((TM, wq), lambda b, i: (b * nq + i, 0)),
                  pl.BlockSpec((DEC_SEQ, wq), lambda b, i: (b, 0)),
                  pl.BlockSpec((DEC_SEQ, LANES), lambda b, i: (b, 0)),
                  pl.BlockSpec((PAST_LEN, wq), lambda b, i: (b, 0)),
                  pl.BlockSpec((PAST_LEN, LANES), lambda b, i: (b, 0))],
        out_specs=pl.BlockSpec((TM, MLA_HEADS * MLA_V), lambda b, i: (b * nq + i, 0)),
        out_shape=jax.ShapeDtypeStruct((N_TOK, MLA_HEADS * MLA_V), BF16),
        compiler_params=_params(("arbitrary", "arbitrary")),
        name="mla_attn_lat",
    )(q, kv, kpe, kv_ctx, kpe_ctx)


def _gmlp_kernel(x_ref, sc_ref, sh_ref, win_ref, vg_ref, ws_ref, bs_ref, p_ref):
    h = (x_ref[...] * (1.0 + sc_ref[0]) + sh_ref[0]).astype(BF16)
    zv = _gelu_tanh(_dot(h, win_ref[:, GM_HALF:]))
    mu = jnp.mean(zv, axis=-1, keepdims=True)
    d = zv - mu
    var = jnp.mean(d * d, axis=-1, keepdims=True)
    vn = (d * lax.rsqrt(var + EPS) * vg_ref[...]).astype(BF16)
    for g in range(GM_GROUPS):
        c0 = g * GM_GW
        u = _gelu_tanh(_dot(h, win_ref[:, c0:c0 + GM_GW]))
        mixed = jnp.concatenate(
            [_dot(ws_ref[g], vn[k * CHUNK:(k + 1) * CHUNK, c0:c0 + GM_GW]) for k in range(TM // CHUNK)], axis=0)
        bias = jnp.concatenate([bs_ref[g]] * (TM // CHUNK), axis=0)
        p_ref[:, c0:c0 + GM_GW] = (u * (mixed + bias)).astype(BF16)


def _gmlp_in(x, sc, sh, w, seg_len):
    return pl.pallas_call(
        _gmlp_kernel,
        grid=(N_TOK // TM,),
        in_specs=[_row_spec(D_MODEL), _seg_spec(seg_len), _seg_spec(seg_len),
                  _full_spec((D_MODEL, GM_FF)), _full_spec((1, GM_HALF)),
                  _full_spec((GM_GROUPS, CHUNK, CHUNK)), _full_spec((GM_GROUPS, CHUNK, 1))],
        out_specs=_row_spec(GM_HALF),
        out_shape=jax.ShapeDtypeStruct((N_TOK, GM_HALF), BF16),
        compiler_params=_params(("arbitrary",)),
        name="gmlp_in",
    )(x, sc, sh, w["w_in"], w["v_gain"], w["w_s"], w["b_s"])


SWA_SCALE = SWA_HEAD_DIM ** -0.5
SWA_QW = SWA_HEADS * SWA_HEAD_DIM
SWA_KW = SWA_KV_HEADS * 2 * LANES
SWA_VW = SWA_KV_HEADS * LANES


def _swa_proj_kernel(rope, x_ref, sc_ref, sh_ref, w_ref, *rest):
    if rope:
        cos_ref, sin_ref, q_ref, kd_ref, vd_ref, kraw_ref, vraw_ref = rest
    else:
        q_ref, kd_ref, vd_ref, kraw_ref, vraw_ref = rest
    h = (x_ref[...] * (1.0 + sc_ref[0]) + sh_ref[0]).astype(BF16)
    z = _dot(h, w_ref[...])
    lane = lax.broadcasted_iota(I32, (TM, LANES), 1)
    low = lane < SWA_HEAD_DIM
    for m in range(SWA_KV_HEADS // 2):
        k0 = z[:, SWA_QW + (2 * m) * 2 * LANES:SWA_QW + (2 * m) * 2 * LANES + LANES]
        k1 = z[:, SWA_QW + (2 * m + 1) * 2 * LANES + LANES:SWA_QW + (2 * m + 2) * 2 * LANES]
        kraw_ref[:, m * LANES:(m + 1) * LANES] = jnp.where(low, k0, k1)
        v0 = z[:, SWA_QW + SWA_KW + (2 * m) * LANES:SWA_QW + SWA_KW + (2 * m + 1) * LANES]
        v1 = z[:, SWA_QW + SWA_KW + (2 * m + 1) * LANES:SWA_QW + SWA_KW + (2 * m + 2) * LANES]
        vraw_ref[:, m * LANES:(m + 1) * LANES] = jnp.where(low, v0, v1)
    vd_ref[...] = z[:, SWA_QW + SWA_KW:].astype(BF16)
    if rope:
        cos, sin = cos_ref[...], sin_ref[...]
        for j in range(SWA_QW // LANES):
            q_ref[:, j * LANES:(j + 1) * LANES] = _rope128(z[:, j * LANES:(j + 1) * LANES], cos, sin).astype(BF16)
        for j in range(SWA_KW // LANES):
            a = SWA_QW + j * LANES
            kd_ref[:, j * LANES:(j + 1) * LANES] = _rope128(z[:, a:a + LANES], cos, sin).astype(BF16)
    else:
        q_ref[...] = z[:, :SWA_QW].astype(BF16)
        kd_ref[...] = z[:, SWA_QW:SWA_QW + SWA_KW].astype(BF16)


def _swa_proj(x, sc, sh, w_ext, seg_len, rope_tabs):
    rope = rope_tabs is not None
    in_specs = [_row_spec(D_MODEL), _seg_spec(seg_len), _seg_spec(seg_len), _full_spec(w_ext.shape)]
    args = [x, sc, sh, w_ext]
    if rope:
        nblk = DEC_SEQ // TM
        tab_spec = pl.BlockSpec((TM, LANES), lambda r: (r % nblk, 0))
        in_specs += [tab_spec, tab_spec]
        args += list(rope_tabs)
    kvw = SWA_KV_HEADS * SWA_HEAD_DIM
    return pl.pallas_call(
        functools.partial(_swa_proj_kernel, rope),
        grid=(N_TOK // TM,),
        in_specs=in_specs,
        out_specs=[_row_spec(SWA_QW), _row_spec(SWA_KW), _row_spec(SWA_VW), _row_spec(kvw), _row_spec(kvw)],
        out_shape=[jax.ShapeDtypeStruct((N_TOK, SWA_QW), BF16), jax.ShapeDtypeStruct((N_TOK, SWA_KW), BF16),
                   jax.ShapeDtypeStruct((N_TOK, SWA_VW), BF16), jax.ShapeDtypeStruct((N_TOK, kvw), F32),
                   jax.ShapeDtypeStruct((N_TOK, kvw), F32)],
        compiler_params=_params(("arbitrary",)),
        name="swa_proj",
    )(*args)


def _swa_heads(q_ref, sink_ref, o_ref, score_fn, value_fn):
    rows = q_ref.shape[0]
    lane = lax.broadcasted_iota(I32, (rows, LANES), 1)
    low = lane < SWA_HEAD_DIM
    for hk in range(SWA_KV_HEADS):
        for pr in range(2):
            c0 = hk * 2 * LANES + pr * LANES
            qp = q_ref[:, c0:c0 + LANES]
            res = []
            for par in range(2):
                hd = hk * 4 + pr * 2 + par
                sk = sink_ref[:, hd:hd + 1]
                blocks = score_fn(hk, par, qp)
                m = sk
                for s in blocks:
                    m = jnp.maximum(m, jnp.max(s, axis=-1, keepdims=True))
                probs = [jnp.exp(s - m) for s in blocks]
                l = jnp.exp(sk - m)
                for p in probs:
                    l = l + jnp.sum(p, axis=-1, keepdims=True)
                res.append(value_fn(hk, probs) / l)
            o_ref[:, c0:c0 + LANES] = jnp.where(low, res[0], res[1]).astype(BF16)


def _swa_attn_ctx_kernel(q_ref, kd_ref, vd_ref, sink_ref, o_ref):
    def score_fn(hk, par, qp):
        a = hk * 2 * LANES + par * LANES
        return [_dot_nt(qp, kd_ref[:, a:a + LANES]) * SWA_SCALE]

    def value_fn(hk, probs):
        return _dot(probs[0].astype(BF16), vd_ref[:, hk * LANES:(hk + 1) * LANES])

    _swa_heads(q_ref, sink_ref, o_ref, score_fn, value_fn)


def _swa_attn_ctx(q, kd, vd, sink):
    blk = lambda w: pl.BlockSpec((CTX_SEQ, w), lambda b: (b, 0))
    return pl.pallas_call(
        _swa_attn_ctx_kernel,
        grid=(CTX_BATCH,),
        in_specs=[blk(SWA_QW), blk(SWA_KW), blk(SWA_VW), pl.BlockSpec((1, SWA_HEADS), lambda b: (0, 0))],
        out_specs=blk(SWA_QW),
        out_shape=jax.ShapeDtypeStruct((N_TOK, SWA_QW), BF16),
        compiler_params=_params(("arbitrary",)),
        name="swa_attn_ctx",
    )(q, kd, vd, sink)


SWA_WIN = 3 * BAND


def _swa_attn_lat_kernel(q_ref, kdl_ref, vdl_ref, kdc_ref, vdc_ref, sink_ref, o_ref):
    nb = pl.program_id(1)
    start = pl.multiple_of(jnp.clip((nb - 1) * BAND, 0, DEC_SEQ - SWA_WIN), BAND)
    qpos = nb * BAND + lax.broadcasted_iota(I32, (BAND, SWA_WIN), 0)
    kpos = start + lax.broadcasted_iota(I32, (BAND, SWA_WIN), 1)
    valid = jnp.abs(qpos - kpos) <= WINDOW

    def score_fn(hk, par, qp):
        a = hk * 2 * LANES + par * LANES
        kl = kdl_ref[pl.ds(start, SWA_WIN), a:a + LANES]
        sl = jnp.where(valid, _dot_nt(qp, kl) * SWA_SCALE, NEG_INF)
        sc = _dot_nt(qp, kdc_ref[:, a:a + LANES]) * SWA_SCALE
        return [sl, sc]

    def value_fn(hk, probs):
        vl = vdl_ref[pl.ds(start, SWA_WIN), hk * LANES:(hk + 1) * LANES]
        return _dot(probs[0].astype(BF16), vl) + _dot(probs[1].astype(BF16), vdc_ref[:, hk * LANES:(hk + 1) * LANES])

    _swa_heads(q_ref, sink_ref, o_ref, score_fn, value_fn)


def _swa_attn_lat(q, kd, vd, kd_ctx, vd_ctx, sink):
    nbands = DEC_SEQ // BAND
    return pl.pallas_call(
        _swa_attn_lat_kernel,
        grid=(DEC_BATCH, nbands),
        in_specs=[pl.BlockSpec((BAND, SWA_QW), lambda b, i: (b * nbands + i, 0)),
                  pl.BlockSpec((DEC_SEQ, SWA_KW), lambda b, i: (b, 0)),
                  pl.BlockSpec((DEC_SEQ, SWA_VW), lambda b, i: (b, 0)),
                  pl.BlockSpec((PAST_LEN, SWA_KW), lambda b, i: (b, 0)),
                  pl.BlockSpec((PAST_LEN, SWA_VW), lambda b, i: (b, 0)),
                  pl.BlockSpec((1, SWA_HEADS), lambda b, i: (0, 0))],
        out_specs=pl.BlockSpec((BAND, SWA_QW), lambda b, i: (b * nbands + i, 0)),
        out_shape=jax.ShapeDtypeStruct((N_TOK, SWA_QW), BF16),
        compiler_params=_params(("arbitrary", "arbitrary")),
        name="swa_attn_lat",
    )(q, kd, vd, kd_ctx, vd_ctx, sink)


def _post_kernel(a_ref, w_ref, x_ref, g1_ref, sc2_ref, sh2_ref, lng_ref, lnb_ref, rt_ref, x1_ref, h2_ref, lg_ref):
    y = _dot(a_ref[...], w_ref[...])
    x1 = _layer_norm(ALPHA * x_ref[...] + g1_ref[0] * y, lng_ref[...], lnb_ref[...])
    x1_ref[...] = x1
    h2 = x1 * (1.0 + sc2_ref[0]) + sh2_ref[0]
    half = D_MODEL // 2
    h2_ref[...] = _pack2(h2[:, :half], h2[:, half:])
    lg_ref[...] = jnp.dot(h2, rt_ref[...], precision=HIGHEST, preferred_element_type=F32)


def _post(a, w_out, x, g1, sc2, sh2, lng, lnb, router_pad, seg_len):
    k = a.shape[1]
    return pl.pallas_call(
        _post_kernel,
        grid=(N_TOK // TM,),
        in_specs=[_row_spec(k), _full_spec((k, D_MODEL)), _row_spec(D_MODEL),
                  _seg_spec(seg_len), _seg_spec(seg_len), _seg_spec(seg_len),
                  _full_spec((1, D_MODEL)), _full_spec((1, D_MODEL)), _full_spec((D_MODEL, LANES))],
        out_specs=[_row_spec(D_MODEL), _row_spec(D_MODEL // 2), _row_spec(LANES)],
        out_shape=[jax.ShapeDtypeStruct((N_TOK, D_MODEL), F32), jax.ShapeDtypeStruct((N_TOK, D_MODEL // 2), U32),
                   jax.ShapeDtypeStruct((N_TOK, LANES), F32)],
        compiler_params=_params(("arbitrary",)),
        name="mixer_out",
    )(a, w_out, x, g1, sc2, sh2, lng, lnb, router_pad)


CUM_BLK = 256


def _excl_cumsum_lanes(a, upper):
    outs = []
    carry = jnp.zeros((a.shape[0], 1), F32)
    for b in range(N_TOK // CUM_BLK):
        blk = a[:, b * CUM_BLK:(b + 1) * CUM_BLK]
        outs.append(_dot(blk.astype(BF16), upper) + carry)
        carry = carry + jnp.sum(blk, axis=1, keepdims=True)
    return jnp.concatenate(outs, axis=1)


def _route_kernel(lg_ref, r_ref, tok_ref, boff_ref, idx_ref, pos_ref, rank_scr, w_scr):
    e_rows = N_EXPERTS
    lt = lg_ref[...].T[:e_rows]
    ex = jnp.exp(lt - jnp.max(lt, axis=0, keepdims=True))
    aff = ex / jnp.sum(ex, axis=0, keepdims=True)
    keys = lax.bitcast_convert_type(aff, I32)

    cur = jnp.zeros((e_rows, 1), I32)
    for b in range(30, -1, -1):
        cand = cur | jnp.int32(1 << b)
        cnt = jnp.sum(jnp.where(keys >= cand, 1.0, 0.0), axis=1, keepdims=True)
        cur = jnp.where(cnt >= float(CAP), cand, cur)

    ri = lax.broadcasted_iota(I32, (CUM_BLK, CUM_BLK), 0)
    ci = lax.broadcasted_iota(I32, (CUM_BLK, CUM_BLK), 1)
    upper = jnp.where(ri < ci, 1.0, 0.0).astype(BF16)

    gt = keys > cur
    eq = keys == cur
    need = float(CAP) - jnp.sum(jnp.where(gt, 1.0, 0.0), axis=1, keepdims=True)
    tie_rank = _excl_cumsum_lanes(jnp.where(eq, 1.0, 0.0), upper)
    sel = gt | (eq & (tie_rank < need))
    msk = jnp.where(sel, 1.0, 0.0)

    cnt_t = jnp.sum(msk, axis=0, keepdims=True)
    both = _excl_cumsum_lanes(jnp.concatenate([msk, jnp.broadcast_to(cnt_t, (8, N_TOK))], axis=0), upper)
    rank = both[:e_rows]
    off = both[e_rows:e_rows + 1]
    rank_scr[...] = jnp.where(sel, rank, -1.0)

    li = lax.broadcasted_iota(I32, (e_rows, LANES), 0)
    lk = lax.broadcasted_iota(I32, (e_rows, LANES), 1)
    lower = jnp.where(lk < li, 1.0, 0.0).astype(BF16)
    msk_pad = jnp.concatenate([msk, jnp.zeros((LANES - e_rows, N_TOK), F32)], axis=0).astype(BF16)
    pos = off + _dot(lower, msk_pad)

    inv = 1.0 / LANES
    pos_hi = jnp.floor(pos * inv)
    g0 = aff.astype(BF16).astype(F32)
    r1 = aff - g0
    g1 = r1.astype(BF16).astype(F32)
    g2 = (r1 - g1).astype(BF16).astype(F32)
    tt = lax.broadcasted_iota(I32, (1, N_TOK), 1).astype(F32)
    t_hi = jnp.floor(tt * inv)
    off_hi = jnp.floor(off * inv)
    misc = jnp.concatenate([t_hi, tt - t_hi * LANES, off_hi, off - off_hi * LANES, cnt_t,
                            jnp.zeros((3, N_TOK), F32)], axis=0)
    feat = jnp.concatenate([pos_hi, pos - pos_hi * LANES, g0, g1, g2, misc,
                            jnp.zeros((LANES - 5 * e_rows - 8, N_TOK), F32)], axis=0)
    table = feat.T
    tok_ref[...] = table
    w_scr[...] = table.astype(BF16)

    nblk = N_TOK // TOK_BLK
    bt = lax.broadcasted_iota(I32, (N_TOK, LANES), 0) // TOK_BLK
    bb = lax.broadcasted_iota(I32, (N_TOK, LANES), 1)
    before = jnp.where((bt < bb) & (bb <= nblk), 1.0, 0.0).astype(BF16)
    boff = _dot(jnp.broadcast_to(cnt_t, (8, N_TOK)).astype(BF16), before)
    boff_ref[...] = boff.astype(I32)

    sub = lax.broadcasted_iota(I32, (8, LANES), 0)
    ln8 = lax.broadcasted_iota(I32, (8, LANES), 1)
    rb_rows = 128

    def per_expert(e, carry):
        rk = rank_scr[pl.ds(e, 1), :]
        want = jnp.where(sub < 5, sub * e_rows + e, sub + (F_T_HI - 5))
        pick = jnp.where(ln8 == want, 1.0, 0.0).astype(BF16)
        for rb in range(CAP // rb_rows):
            slot = (rb * rb_rows + lax.broadcasted_iota(I32, (rb_rows, 1), 0)).astype(F32)
            onehot = jnp.where(rk == slot, 1.0, 0.0).astype(BF16)
            rows = _dot(onehot, w_scr[...])
            r_ref[pl.ds(e, 1), rb * rb_rows:(rb + 1) * rb_rows, :] = rows[None]
            got = _dot_nt(pick, rows.astype(BF16))
            idx_ref[pl.ds(e, 1), :, rb * rb_rows:(rb + 1) * rb_rows] = (got[5:6] * LANES + got[6:7]).astype(I32)[None]
            pos_ref[pl.ds(e, 1), :, rb * rb_rows:(rb + 1) * rb_rows] = (got[0:1] * LANES + got[1:2]).astype(I32)[None]
        return carry

    lax.fori_loop(0, e_rows, per_expert, 0)


def _route(logits):
    return pl.pallas_call(
        _route_kernel,
        grid=(1,),
        in_specs=[pl.BlockSpec((N_TOK, LANES), lambda i: (0, 0))],
        out_specs=[pl.BlockSpec((N_EXPERTS, CAP, LANES), lambda i: (0, 0, 0)),
                   pl.BlockSpec((N_TOK, LANES), lambda i: (0, 0)),
                   pl.BlockSpec((8, LANES), lambda i: (0, 0)),
                   pl.BlockSpec((N_EXPERTS, 1, CAP), lambda i: (0, 0, 0)),
                   pl.BlockSpec((N_EXPERTS, 1, CAP), lambda i: (0, 0, 0))],
        out_shape=[jax.ShapeDtypeStruct((N_EXPERTS, CAP, LANES), F32),
                   jax.ShapeDtypeStruct((N_TOK, LANES), F32),
                   jax.ShapeDtypeStruct((8, LANES), I32),
                   jax.ShapeDtypeStruct((N_EXPERTS, 1, CAP), I32),
                   jax.ShapeDtypeStruct((N_EXPERTS, 1, CAP), I32)],
        scratch_shapes=[pltpu.VMEM((N_EXPERTS, N_TOK), F32), pltpu.VMEM((N_TOK, LANES), BF16)],
        compiler_params=_params(("arbitrary",)),
        name="moe_route",
    )(logits)


ROWS_E = 2 * CAP
HALF_D = D_MODEL // 2
N_FT = EXPERT_FF // FF_TILE


def _ffn_kernel(idx_s, pos_s, hc_hbm, hl_hbm, rc_ref, rl_ref, wg_ref, wu_ref, wd_ref, yc_hbm, yl_hbm,
                xe, xb, acc, yb, gsem, ssem):
    e = pl.program_id(0)
    f = pl.program_id(1)

    def gather_rows(ee, slot):
        base = ee * ROWS_E

        def body(r, carry):
            tc = idx_s[base + r]
            pltpu.make_async_copy(hc_hbm.at[pl.ds(tc, 1)], xe.at[slot, pl.ds(r, 1)], gsem.at[slot]).start()
            tl = idx_s[base + CAP + r]
            pltpu.make_async_copy(hl_hbm.at[pl.ds(tl, 1)], xe.at[slot, pl.ds(CAP + r, 1)], gsem.at[slot]).start()
            return carry

        lax.fori_loop(0, CAP, body, 0, unroll=8)

    def wait_scatter():
        pltpu.make_async_copy(yb, yc_hbm.at[pl.ds(0, ROWS_E)], ssem.at[0]).wait()

    @pl.when(f == 0)
    def _():
        @pl.when(e == 0)
        def _():
            gather_rows(0, 0)

        @pl.when(e + 1 < N_EXPERTS)
        def _():
            gather_rows(e + 1, (e + 1) % 2)

        slot = e % 2
        pltpu.make_async_copy(hc_hbm.at[pl.ds(0, ROWS_E)], xe.at[slot], gsem.at[slot]).wait()
        lo, hi = _unpack2(xe[slot])
        xb[:, :HALF_D] = lo.astype(BF16)
        xb[:, HALF_D:] = hi.astype(BF16)

    x = xb[...]
    g = _dot(x, wg_ref[0].astype(BF16))
    u = _dot(x, wu_ref[0].astype(BF16))
    hid = (g * _sigmoid(g) * u).astype(BF16)
    part = _dot(hid, wd_ref[0].astype(BF16))

    @pl.when(f == 0)
    def _():
        acc[...] = part

    @pl.when(f > 0)
    def _():
        acc[...] += part

    @pl.when(f == N_FT - 1)
    def _():
        @pl.when(e > 0)
        def _():
            wait_scatter()

        lane = lax.broadcasted_iota(I32, (CAP, LANES), 1)
        mine = (lane == F_G0 + e) | (lane == F_G1 + e) | (lane == F_G2 + e)
        gate = jnp.concatenate(
            [jnp.sum(jnp.where(mine, rc_ref[0], 0.0), axis=1, keepdims=True),
             jnp.sum(jnp.where(mine, rl_ref[0], 0.0), axis=1, keepdims=True)], axis=0)
        y = acc[...] * gate
        yb[...] = _pack2(y[:, :HALF_D], y[:, HALF_D:])
        base = e * ROWS_E

        def body(r, carry):
            pc = pos_s[base + r]
            pltpu.make_async_copy(yb.at[pl.ds(r, 1)], yc_hbm.at[pl.ds(pc, 1)], ssem.at[0]).start()
            pl_ = pos_s[base + CAP + r]
            pltpu.make_async_copy(yb.at[pl.ds(CAP + r, 1)], yl_hbm.at[pl.ds(pl_, 1)], ssem.at[0]).start()
            return carry

        lax.fori_loop(0, CAP, body, 0, unroll=8)

        @pl.when(e == N_EXPERTS - 1)
        def _():
            wait_scatter()


def _moe_ffn(idx_all, pos_all, h_ctx, h_lat, r_ctx, r_lat, w_gate, w_up, w_down):
    any_spec = pl.BlockSpec(memory_space=pl.ANY)
    grid_spec = pltpu.PrefetchScalarGridSpec(
        num_scalar_prefetch=2,
        grid=(N_EXPERTS, N_FT),
        in_specs=[any_spec, any_spec,
                  pl.BlockSpec((1, CAP, LANES), lambda e, f, *_: (e, 0, 0)),
                  pl.BlockSpec((1, CAP, LANES), lambda e, f, *_: (e, 0, 0)),
                  pl.BlockSpec((1, D_MODEL, FF_TILE), lambda e, f, *_: (e, 0, f)),
                  pl.BlockSpec((1, D_MODEL, FF_TILE), lambda e, f, *_: (e, 0, f)),
                  pl.BlockSpec((1, FF_TILE, D_MODEL), lambda e, f, *_: (e, f, 0))],
        out_specs=[any_spec, any_spec],
        scratch_shapes=[pltpu.VMEM((2, ROWS_E, HALF_D), U32), pltpu.VMEM((ROWS_E, D_MODEL), BF16),
                        pltpu.VMEM((ROWS_E, D_MODEL), F32), pltpu.VMEM((ROWS_E, HALF_D), U32),
                        pltpu.SemaphoreType.DMA((2,)), pltpu.SemaphoreType.DMA((1,))],
    )
    n_rows = N_EXPERTS * CAP
    return pl.pallas_call(
        _ffn_kernel,
        grid_spec=grid_spec,
        out_shape=[jax.ShapeDtypeStruct((n_rows, HALF_D), U32), jax.ShapeDtypeStruct((n_rows, HALF_D), U32)],
        compiler_params=_params(("arbitrary", "arbitrary")),
        name="moe_ffn",
    )(idx_all, pos_all, h_ctx, h_lat, r_ctx, r_lat, w_gate, w_up, w_down)


Y_ROWS = N_EXPERTS * CAP


def _combine_kernel(boff_s, ys_ref, tok_ref, x_ref, g2_ref, lng_ref, lnb_ref, o_ref, acc):
    b = pl.program_id(0)
    lo_row = boff_s[b]
    hi_row = boff_s[b + 1]
    start0 = (lo_row // 8) * 8
    n_chunks = (hi_row - start0 + TOK_BLK - 1) // TOK_BLK
    tok = tok_ref[...]
    off = tok[:, F_OFF_HI:F_OFF_HI + 1] * LANES + tok[:, F_OFF_LO:F_OFF_LO + 1]
    end = off + tok[:, F_CNT:F_CNT + 1]
    acc[...] = jnp.zeros_like(acc)

    def chunk(k, carry):
        nominal = start0 + k * TOK_BLK
        start = pl.multiple_of(jnp.minimum(nominal, Y_ROWS - TOK_BLK), 8)
        rows = start + lax.broadcasted_iota(I32, (TOK_BLK, TOK_BLK), 1)
        rowsf = rows.astype(F32)
        seg = jnp.where((rowsf >= off) & (rowsf < end) & (rows >= nominal), 1.0, 0.0).astype(BF16)
        lo, hi = _unpack2(ys_ref[pl.ds(start, TOK_BLK), :])
        acc[:, :HALF_D] += _dot(seg, lo.astype(BF16))
        acc[:, HALF_D:] += _dot(seg, hi.astype(BF16))
        return carry

    lax.fori_loop(0, n_chunks, chunk, 0)
    o_ref[...] = _layer_norm(ALPHA * x_ref[...] + g2_ref[0] * acc[...], lng_ref[...], lnb_ref[...])


def _combine(boff, ys, tok, x1, g2, lng, lnb, seg_len):
    grid_spec = pltpu.PrefetchScalarGridSpec(
        num_scalar_prefetch=1,
        grid=(N_TOK // TOK_BLK,),
        in_specs=[pl.BlockSpec((Y_ROWS, HALF_D), lambda b, *_: (0, 0)),
                  pl.BlockSpec((TOK_BLK, LANES), lambda b, *_: (b, 0)),
                  pl.BlockSpec((TOK_BLK, D_MODEL), lambda b, *_: (b, 0)),
                  pl.BlockSpec((1, 1, D_MODEL), lambda b, *_: (b * TOK_BLK // seg_len, 0, 0)),
                  pl.BlockSpec((1, D_MODEL), lambda b, *_: (0, 0)),
                  pl.BlockSpec((1, D_MODEL), lambda b, *_: (0, 0))],
        out_specs=pl.BlockSpec((TOK_BLK, D_MODEL), lambda b, *_: (b, 0)),
        scratch_shapes=[pltpu.VMEM((TOK_BLK, D_MODEL), F32)],
    )
    return pl.pallas_call(
        _combine_kernel,
        grid_spec=grid_spec,
        out_shape=jax.ShapeDtypeStruct((N_TOK, D_MODEL), F32),
        compiler_params=_params(("arbitrary",)),
        name="moe_combine",
    )(boff, ys, tok, x1, g2, lng, lnb)


def _rope_tables():
    t = jnp.arange(DEC_SEQ)
    n_freq = MLA_ROPE // 4
    inv_freq = ROPE_BASE ** (-jnp.arange(n_freq, dtype=F32) / n_freq)
    ang_r = (t // GRID_W).astype(F32)[:, None] * inv_freq[None, :]
    ang_c = (t % GRID_W).astype(F32)[:, None] * inv_freq[None, :]
    cos64 = jnp.concatenate([jnp.cos(ang_r)] * 2 + [jnp.cos(ang_c)] * 2, axis=1)
    sin64 = jnp.concatenate([jnp.sin(ang_r)] * 2 + [jnp.sin(ang_c)] * 2, axis=1)
    one, zero = jnp.ones_like(cos64), jnp.zeros_like(sin64)
    mla = (jnp.concatenate([cos64, one], axis=1), jnp.concatenate([sin64, zero], axis=1))
    swa = (jnp.concatenate([cos64, cos64], axis=1), jnp.concatenate([sin64, sin64], axis=1))
    return mla, swa


def _swa_key_layout(k):
    z = jnp.zeros_like(k)
    return jnp.stack([k, z, z, k], axis=2).reshape(k.shape[0], SWA_KW)


def _swa_value_layout(v):
    return jnp.stack([v, v], axis=2).reshape(v.shape[0], SWA_VW)


def kernel(x_prompt, x_sample, cache_mla_ckv, cache_mla_kpe, cache_swa_k, cache_swa_v, c, c_ctx, mod_w, mod_b, ln_gain, ln_bias, mla_w_in, mla_q_gain, mla_kv_gain, mla_w_q_up, mla_w_kv_up, mla_w_out, gm_w_in, gm_v_gain, gm_w_s, gm_b_s, gm_w_out, swa_w_qkv, swa_sink, swa_w_out, moe_router, moe_w_gate, moe_w_up, moe_w_down):
    d = D_MODEL
    xs = [x_prompt.reshape(N_TOK, d), x_sample.reshape(N_TOK, d)]
    seg_lens = [N_TOK, DEC_SEQ]
    cond8 = jnp.concatenate([c_ctx[None, :], c, jnp.zeros((3, d), F32)], axis=0)
    mod = _modulation_all(cond8, mod_w, mod_b)
    rope_mla, rope_swa = _rope_tables()

    def mod_rows(i, q):
        m = mod[i, :, q * d:(q + 1) * d]
        return [m[0:1].reshape(1, 1, d), m[1:1 + DEC_BATCH].reshape(DEC_BATCH, 1, d)]

    ckv_out, kpe_out, k_out, v_out = [], [], [], []
    for i in range(DEPTH):
        kind, j = i % 3, i // 3
        sh1, sc1, g1, sh2, sc2, g2 = (mod_rows(i, q) for q in range(6))
        router_pad = jnp.pad(moe_router[i], ((0, 0), (0, LANES - N_EXPERTS)))
        lng = ln_gain[i].reshape(2, 1, d)
        lnb = ln_bias[i].reshape(2, 1, d)

        mixed = []
        if kind == 0:
            w = {
                "w_in": jnp.pad(mla_w_in[j], ((0, 0), (0, MLA_ROPE))).astype(BF16),
                "q_gain": mla_q_gain[j][None, :], "kv_gain": mla_kv_gain[j][None, :],
                "w_q": jnp.pad(mla_w_q_up[j], ((0, 0), (0, 0), (0, MLA_QK_PAD - MLA_NOPE - MLA_ROPE)))
                .reshape(MLA_Q_LORA, MLA_HEADS * MLA_QK_PAD).astype(BF16),
                "w_kv": mla_w_kv_up[j].reshape(MLA_KV_LORA, MLA_HEADS * (MLA_NOPE + MLA_V)).astype(BF16),
            }
            w_out = mla_w_out[j].astype(BF16)
            q, kv, kpe, ckv, kraw = _mla_proj(xs[0], sc1[0], sh1[0], w, seg_lens[0], None)
            ckv_out.append(ckv.reshape(CTX_BATCH, CTX_SEQ, MLA_KV_LORA))
            kpe_out.append(kraw.reshape(CTX_BATCH, CTX_SEQ, MLA_ROPE))
            mixed.append(_mla_attn_ctx(q, kv, kpe))
            q, kv, kpe, _, _ = _mla_proj(xs[1], sc1[1], sh1[1], w, seg_lens[1], rope_mla)
            kv_ctx = _matmul_bf16(cache_mla_ckv[:, j].reshape(DEC_BATCH * PAST_LEN, MLA_KV_LORA).astype(BF16), w["w_kv"])
            kpe_ctx = jnp.pad(cache_mla_kpe[:, j].reshape(DEC_BATCH * PAST_LEN, MLA_ROPE),
                              ((0, 0), (0, LANES - MLA_ROPE))).astype(BF16)
            mixed.append(_mla_attn_lat(q, kv, kpe, kv_ctx, kpe_ctx))
        elif kind == 1:
            w = {"w_in": gm_w_in[j].astype(BF16), "v_gain": gm_v_gain[j][None, :],
                 "w_s": gm_w_s[j].astype(BF16), "b_s": gm_b_s[j][:, :, None]}
            w_out = gm_w_out[j].astype(BF16)
            for g in range(2):
                mixed.append(_gmlp_in(xs[g], sc1[g], sh1[g], w, seg_lens[g]))
        else:
            wq = swa_w_qkv[j]
            nq, nkv = SWA_HEADS * SWA_HEAD_DIM, SWA_KV_HEADS * SWA_HEAD_DIM
            w_ext = jnp.concatenate(
                [wq[:, :nq],
                 _swa_key_layout(wq[:, nq:nq + nkv].reshape(d, SWA_KV_HEADS, SWA_HEAD_DIM)),
                 _swa_value_layout(wq[:, nq + nkv:].reshape(d, SWA_KV_HEADS, SWA_HEAD_DIM))], axis=1).astype(BF16)
            w_out = swa_w_out[j].astype(BF16)
            sink = swa_sink[j][None, :]
            q, kd, vd, kraw, vraw = _swa_proj(xs[0], sc1[0], sh1[0], w_ext, seg_lens[0], None)
            k_out.append(kraw.reshape(CTX_BATCH, CTX_SEQ, SWA_KV_HEADS, SWA_HEAD_DIM))
            v_out.append(vraw.reshape(CTX_BATCH, CTX_SEQ, SWA_KV_HEADS, SWA_HEAD_DIM))
            mixed.append(_swa_attn_ctx(q, kd, vd, sink))
            q, kd, vd, _, _ = _swa_proj(xs[1], sc1[1], sh1[1], w_ext, seg_lens[1], rope_swa)
            rows = DEC_BATCH * PAST_LEN
            kd_ctx = _swa_key_layout(cache_swa_k[:, j].reshape(rows, SWA_KV_HEADS, SWA_HEAD_DIM)).astype(BF16)
            vd_ctx = _swa_value_layout(cache_swa_v[:, j].reshape(rows, SWA_KV_HEADS, SWA_HEAD_DIM)).astype(BF16)
            mixed.append(_swa_attn_lat(q, kd, vd, kd_ctx, vd_ctx, sink))

        x1, h2, routed = [], [], []
        for g in range(2):
            x1_g, h2_g, lg_g = _post(mixed[g], w_out, xs[g], g1[g], sc2[g], sh2[g], lng[0], lnb[0], router_pad, seg_lens[g])
            x1.append(x1_g)
            h2.append(h2_g)
            routed.append(_route(lg_g))
        idx_all = jnp.concatenate([routed[0][3], routed[1][3]], axis=2).reshape(-1)
        pos_all = jnp.concatenate([routed[0][4], routed[1][4]], axis=2).reshape(-1)
        ys = _moe_ffn(idx_all, pos_all, h2[0], h2[1], routed[0][0], routed[1][0],
                      moe_w_gate[i], moe_w_up[i], moe_w_down[i])
        for g in range(2):
            boff = routed[g][2][0]
            xs[g] = _combine(boff, ys[g], routed[g][1], x1[g], g2[g], lng[1], lnb[1], seg_lens[g])

    return (xs[0].reshape(CTX_BATCH, CTX_SEQ, d), xs[1].reshape(DEC_BATCH, DEC_SEQ, d),
            jnp.stack(ckv_out, axis=1), jnp.stack(kpe_out, axis=1),
            jnp.stack(k_out, axis=1), jnp.stack(v_out, axis=1))
```

```python
import functools

import jax
import jax.numpy as jnp
from jax import lax
from jax.experimental import pallas as pl
from jax.experimental.pallas import tpu as pltpu

D_MODEL = 1024
N_TOK = 4096
DEPTH = 4
DEC_BATCH = 4
DEC_SEQ = 1024
PAST_LEN = 256
CTX_BATCH = 16
CTX_SEQ = 256
GRID_W = 64
ALPHA = (2.0 * DEPTH) ** 0.25
MLA_HEADS = 8
MLA_Q_LORA = 384
MLA_KV_LORA = 256
MLA_NOPE = 128
MLA_ROPE = 64
MLA_V = 128
MLA_QK_PAD = 256
GM_FF = 6 * D_MODEL
GM_HALF = GM_FF // 2
GM_GROUPS = 8
GM_GW = GM_HALF // GM_GROUPS
CHUNK = 128
SWA_HEADS = 16
SWA_KV_HEADS = 4
SWA_HEAD_DIM = 64
WINDOW = 128
BAND = 128
N_EXPERTS = 16
EXPERT_FF = 2 * D_MODEL
CAP = 2 * N_TOK // N_EXPERTS
ROPE_BASE = 10000.0
NEG_INF = -1e30
EPS = 1e-6
F32 = jnp.float32
BF16 = jnp.bfloat16
I32 = jnp.int32
U32 = jnp.uint32
HIGHEST = lax.Precision.HIGHEST

LANES = 128
SUBLANES = 8
ROW_TILES = D_MODEL // LANES
VMEM_LIMIT = 56 * 1024 * 1024
TM = 256
FF_TILE = 1024
TOK_BLK = 256
MOD_TN = 1536

F_POS_HI, F_POS_LO, F_G0, F_G1, F_G2 = 0, 16, 32, 48, 64
F_T_HI, F_T_LO, F_OFF_HI, F_OFF_LO, F_CNT = 80, 81, 82, 83, 84


def _params(sem, vmem=VMEM_LIMIT):
    return pltpu.CompilerParams(dimension_semantics=sem, vmem_limit_bytes=vmem)


def _dot(a, b):
    return jnp.dot(a, b, preferred_element_type=F32)


def _dot_nt(a, b):
    return lax.dot_general(a, b, (((1,), (1,)), ((), ())), preferred_element_type=F32)


def _sigmoid(x):
    return 1.0 / (1.0 + jnp.exp(-x))


def _gelu_tanh(x):
    return 0.5 * x * (1.0 + jnp.tanh(0.7978845608028654 * (x + 0.044715 * (x * x * x))))


def _layer_norm(t, gain, bias):
    mu = jnp.mean(t, axis=-1, keepdims=True)
    d = t - mu
    var = jnp.mean(d * d, axis=-1, keepdims=True)
    return d * lax.rsqrt(var + EPS) * gain + bias


def _rms_norm(t, gain):
    return t * lax.rsqrt(jnp.mean(t * t, axis=-1, keepdims=True) + EPS) * gain


def _rope128(x, cos, sin):
    lane = lax.broadcasted_iota(I32, x.shape, 1)
    first = (lane % 32) < 16
    swapped = jnp.where(first, -pltpu.roll(x, 112, 1), pltpu.roll(x, 16, 1))
    return x * cos + swapped * sin


def _mod_kernel(cond_ref, w_ref, b_ref, o_ref):
    c = cond_ref[...]
    s = c * _sigmoid(c)
    o_ref[0] = jnp.dot(s, w_ref[0], precision=HIGHEST, preferred_element_type=F32) + b_ref[0]


def _modulation_all(cond8, mod_w, mod_b):
    return pl.pallas_call(
        _mod_kernel,
        grid=(DEPTH, 6 * D_MODEL // MOD_TN),
        in_specs=[pl.BlockSpec((8, D_MODEL), lambda i, n: (0, 0)),
                  pl.BlockSpec((1, D_MODEL, MOD_TN), lambda i, n: (i, 0, n)),
                  pl.BlockSpec((1, 1, MOD_TN), lambda i, n: (i, 0, n))],
        out_specs=pl.BlockSpec((1, 8, MOD_TN), lambda i, n: (i, 0, n)),
        out_shape=jax.ShapeDtypeStruct((DEPTH, 8, 6 * D_MODEL), F32),
        compiler_params=_params(("arbitrary", "arbitrary")),
        name="modulation",
    )(cond8, mod_w, mod_b.reshape(DEPTH, 1, 6 * D_MODEL))


def _seg_spec(seg_len):
    return pl.BlockSpec((1, 1, D_MODEL), lambda r: (r * TM // seg_len, 0, 0))


def _row_spec(width):
    return pl.BlockSpec((TM, width), lambda r: (r, 0))


def _full_spec(shape):
    nd = len(shape)
    return pl.BlockSpec(shape, lambda r: (0,) * nd)


def _mla_proj_kernel(rope, x_ref, sc_ref, sh_ref, win_ref, qg_ref, kvg_ref, wq_ref, wkv_ref, *rest):
    if rope:
        cos_ref, sin_ref, q_ref, kv_ref, kpe_ref, ckv_ref, kraw_ref = rest
    else:
        q_ref, kv_ref, kpe_ref, ckv_ref, kraw_ref = rest
    h = (x_ref[...] * (1.0 + sc_ref[0]) + sh_ref[0]).astype(BF16)
    z = _dot(h, win_ref[...])
    cq = _rms_norm(z[:, :MLA_Q_LORA], qg_ref[...])
    ckv = _rms_norm(z[:, MLA_Q_LORA:MLA_Q_LORA + MLA_KV_LORA], kvg_ref[...])
    kp = z[:, MLA_Q_LORA + MLA_KV_LORA:]
    ckv_ref[...] = ckv
    kraw_ref[...] = kp[:, :MLA_ROPE]
    q = _dot(cq.astype(BF16), wq_ref[...])
    kv_ref[...] = _dot(ckv.astype(BF16), wkv_ref[...]).astype(BF16)
    if rope:
        cos, sin = cos_ref[...], sin_ref[...]
        for hd in range(MLA_HEADS):
            a = hd * MLA_QK_PAD
            q_ref[:, a:a + LANES] = q[:, a:a + LANES].astype(BF16)
            q_ref[:, a + LANES:a + 2 * LANES] = _rope128(q[:, a + LANES:a + 2 * LANES], cos, sin).astype(BF16)
        kp = _rope128(kp, cos, sin)
    else:
        q_ref[...] = q.astype(BF16)
    kpe_ref[...] = kp.astype(BF16)


def _mla_proj(x, sc, sh, w, seg_len, rope_tabs):
    rope = rope_tabs is not None
    in_specs = [_row_spec(D_MODEL), _seg_spec(seg_len), _seg_spec(seg_len),
                _full_spec(w["w_in"].shape), _full_spec((1, MLA_Q_LORA)), _full_spec((1, MLA_KV_LORA)),
                _full_spec(w["w_q"].shape), _full_spec(w["w_kv"].shape)]
    args = [x, sc, sh, w["w_in"], w["q_gain"], w["kv_gain"], w["w_q"], w["w_kv"]]
    if rope:
        nblk = DEC_SEQ // TM
        tab_spec = pl.BlockSpec((TM, LANES), lambda r: (r % nblk, 0))
        in_specs += [tab_spec, tab_spec]
        args += list(rope_tabs)
    wq = MLA_HEADS * MLA_QK_PAD
    return pl.pallas_call(
        functools.partial(_mla_proj_kernel, rope),
        grid=(N_TOK // TM,),
        in_specs=in_specs,
        out_specs=[_row_spec(wq), _row_spec(wq), _row_spec(LANES), _row_spec(MLA_KV_LORA), _row_spec(MLA_ROPE)],
        out_shape=[jax.ShapeDtypeStruct((N_TOK, wq), BF16), jax.ShapeDtypeStruct((N_TOK, wq), BF16),
                   jax.ShapeDtypeStruct((N_TOK, LANES), BF16), jax.ShapeDtypeStruct((N_TOK, MLA_KV_LORA), F32),
                   jax.ShapeDtypeStruct((N_TOK, MLA_ROPE), F32)],
        compiler_params=_params(("arbitrary",)),
        name="mla_proj",
    )(*args)


def _mm_kernel(a_ref, w_ref, o_ref):
    o_ref[...] = _dot(a_ref[...], w_ref[...]).astype(o_ref.dtype)


def _matmul_bf16(a, w):
    m, k = a.shape
    n = w.shape[1]
    return pl.pallas_call(
        _mm_kernel,
        grid=(m // TM,),
        in_specs=[pl.BlockSpec((TM, k), lambda r: (r, 0)), _full_spec(w.shape)],
        out_specs=pl.BlockSpec((TM, n), lambda r: (r, 0)),
        out_shape=jax.ShapeDtypeStruct((m, n), BF16),
        compiler_params=_params(("arbitrary",)),
        name="matmul",
    )(a, w)


MLA_SCALE = (MLA_NOPE + MLA_ROPE) ** -0.5


def _mla_attn_ctx_kernel(q_ref, kv_ref, kpe_ref, o_ref):
    kpe = kpe_ref[...]
    for hd in range(MLA_HEADS):
        a = hd * MLA_QK_PAD
        qh = q_ref[:, a:a + MLA_QK_PAD]
        kh = jnp.concatenate([kv_ref[:, a:a + LANES], kpe], axis=1)
        vh = kv_ref[:, a + LANES:a + 2 * LANES]
        s = _dot_nt(qh, kh) * MLA_SCALE
        p = jnp.exp(s - jnp.max(s, axis=-1, keepdims=True))
        l = jnp.sum(p, axis=-1, keepdims=True)
        o_ref[:, hd * MLA_V:(hd + 1) * MLA_V] = (_dot(p.astype(BF16), vh) / l).astype(BF16)


def _mla_attn_ctx(q, kv, kpe):
    wq = MLA_HEADS * MLA_QK_PAD
    blk = lambda w: pl.BlockSpec((CTX_SEQ, w), lambda b: (b, 0))
    return pl.pallas_call(
        _mla_attn_ctx_kernel,
        grid=(CTX_BATCH,),
        in_specs=[blk(wq), blk(wq), blk(LANES)],
        out_specs=blk(MLA_HEADS * MLA_V),
        out_shape=jax.ShapeDtypeStruct((N_TOK, MLA_HEADS * MLA_V), BF16),
        compiler_params=_params(("arbitrary",)),
        name="mla_attn_ctx",
    )(q, kv, kpe)


def _mla_attn_lat_kernel(q_ref, kvl_ref, kpel_ref, kvc_ref, kpec_ref, o_ref):
    kpel = kpel_ref[...]
    kpec = kpec_ref[...]
    for hd in range(MLA_HEADS):
        a = hd * MLA_QK_PAD
        qh = q_ref[:, a:a + MLA_QK_PAD]
        kc = jnp.concatenate([kvc_ref[:, a:a + LANES], kpec], axis=1)
        kl = jnp.concatenate([kvl_ref[:, a:a + LANES], kpel], axis=1)
        sc = _dot_nt(qh, kc) * MLA_SCALE
        sl = _dot_nt(qh, kl) * MLA_SCALE
        m = jnp.maximum(jnp.max(sc, axis=-1, keepdims=True), jnp.max(sl, axis=-1, keepdims=True))
        pc = jnp.exp(sc - m)
        plat = jnp.exp(sl - m)
        l = jnp.sum(pc, axis=-1, keepdims=True) + jnp.sum(plat, axis=-1, keepdims=True)
        o = _dot(pc.astype(BF16), kvc_ref[:, a + LANES:a + 2 * LANES]) \
            + _dot(plat.astype(BF16), kvl_ref[:, a + LANES:a + 2 * LANES])
        o_ref[:, hd * MLA_V:(hd + 1) * MLA_V] = (o / l).astype(BF16)


def _mla_attn_lat(q, kv, kpe, kv_ctx, kpe_ctx):
    wq = MLA_HEADS * MLA_QK_PAD
    nq = DEC_SEQ // TM
    return pl.pallas_call(
        _mla_attn_lat_kernel,
        grid=(DEC_BATCH, nq),
        in_specs=[pl.BlockSpec((TM, wq), lambda b, i: (b * nq + i, 0)),
                  pl.BlockSpec((DEC_SEQ, wq), lambda b, i: (b, 0)),
                  pl.BlockSpec((DEC_SEQ, LANES), lambda b, i: (b, 0)),
                  pl.BlockSpec((PAST_LEN, wq), lambda b, i: (b, 0)),
                  pl.BlockSpec((PAST_LEN, LANES), lambda b, i: (b, 0))],
        out_specs=pl.BlockSpec((TM, MLA_HEADS * MLA_V), lambda b, i: (b * nq + i, 0)),
        out_shape=jax.ShapeDtypeStruct((N_TOK, MLA_HEADS * MLA_V), BF16),
        compiler_params=_params(("arbitrary", "arbitrary")),
        name="mla_attn_lat",
    )(q, kv, kpe, kv_ctx, kpe_ctx)


def _gmlp_kernel(x_ref, sc_ref, sh_ref, win_ref, vg_ref, ws_ref, bs_ref, p_ref):
    h = (x_ref[...] * (1.0 + sc_ref[0]) + sh_ref[0]).astype(BF16)
    zv = _gelu_tanh(_dot(h, win_ref[:, GM_HALF:]))
    mu = jnp.mean(zv, axis=-1, keepdims=True)
    d = zv - mu
    var = jnp.mean(d * d, axis=-1, keepdims=True)
    vn = (d * lax.rsqrt(var + EPS) * vg_ref[...]).astype(BF16)
    for g in range(GM_GROUPS):
        c0 = g * GM_GW
        u = _gelu_tanh(_dot(h, win_ref[:, c0:c0 + GM_GW]))
        mixed = jnp.concatenate(
            [_dot(ws_ref[g], vn[k * CHUNK:(k + 1) * CHUNK, c0:c0 + GM_GW]) for k in range(TM // CHUNK)], axis=0)
        bias = jnp.concatenate([bs_ref[g]] * (TM // CHUNK), axis=0)
        p_ref[:, c0:c0 + GM_GW] = (u * (mixed + bias)).astype(BF16)


def _gmlp_in(x, sc, sh, w, seg_len):
    return pl.pallas_call(
        _gmlp_kernel,
        grid=(N_TOK // TM,),
        in_specs=[_row_spec(D_MODEL), _seg_spec(seg_len), _seg_spec(seg_len),
                  _full_spec((D_MODEL, GM_FF)), _full_spec((1, GM_HALF)),
                  _full_spec((GM_GROUPS, CHUNK, CHUNK)), _full_spec((GM_GROUPS, CHUNK, 1))],
        out_specs=_row_spec(GM_HALF),
        out_shape=jax.ShapeDtypeStruct((N_TOK, GM_HALF), BF16),
        compiler_params=_params(("arbitrary",)),
        name="gmlp_in",
    )(x, sc, sh, w["w_in"], w["v_gain"], w["w_s"], w["b_s"])


SWA_SCALE = SWA_HEAD_DIM ** -0.5
SWA_QW = SWA_HEADS * SWA_HEAD_DIM
SWA_KW = SWA_KV_HEADS * 2 * LANES
SWA_VW = SWA_KV_HEADS * LANES


def _swa_proj_kernel(rope, x_ref, sc_ref, sh_ref, w_ref, *rest):
    if rope:
        cos_ref, sin_ref, q_ref, kd_ref, vd_ref, kraw_ref, vraw_ref = rest
    else:
        q_ref, kd_ref, vd_ref, kraw_ref, vraw_ref = rest
    h = (x_ref[...] * (1.0 + sc_ref[0]) + sh_ref[0]).astype(BF16)
    z = _dot(h, w_ref[...])
    lane = lax.broadcasted_iota(I32, (TM, LANES), 1)
    low = lane < SWA_HEAD_DIM
    for m in range(SWA_KV_HEADS // 2):
        k0 = z[:, SWA_QW + (2 * m) * 2 * LANES:SWA_QW + (2 * m) * 2 * LANES + LANES]
        k1 = z[:, SWA_QW + (2 * m + 1) * 2 * LANES + LANES:SWA_QW + (2 * m + 2) * 2 * LANES]
        kraw_ref[:, m * LANES:(m + 1) * LANES] = jnp.where(low, k0, k1)
        v0 = z[:, SWA_QW + SWA_KW + (2 * m) * LANES:SWA_QW + SWA_KW + (2 * m + 1) * LANES]
        v1 = z[:, SWA_QW + SWA_KW + (2 * m + 1) * LANES:SWA_QW + SWA_KW + (2 * m + 2) * LANES]
        vraw_ref[:, m * LANES:(m + 1) * LANES] = jnp.where(low, v0, v1)
    vd_ref[...] = z[:, SWA_QW + SWA_KW:].astype(BF16)
    if rope:
        cos, sin = cos_ref[...], sin_ref[...]
        for j in range(SWA_QW // LANES):
            q_ref[:, j * LANES:(j + 1) * LANES] = _rope128(z[:, j * LANES:(j + 1) * LANES], cos, sin).astype(BF16)
        for j in range(SWA_KW // LANES):
            a = SWA_QW + j * LANES
            kd_ref[:, j * LANES:(j + 1) * LANES] = _rope128(z[:, a:a + LANES], cos, sin).astype(BF16)
    else:
        q_ref[...] = z[:, :SWA_QW].astype(BF16)
        kd_ref[...] = z[:, SWA_QW:SWA_QW + SWA_KW].astype(BF16)


def _swa_proj(x, sc, sh, w_ext, seg_len, rope_tabs):
    rope = rope_tabs is not None
    in_specs = [_row_spec(D_MODEL), _seg_spec(seg_len), _seg_spec(seg_len), _full_spec(w_ext.shape)]
    args = [x, sc, sh, w_ext]
    if rope:
        nblk = DEC_SEQ // TM
        tab_spec = pl.BlockSpec((TM, LANES), lambda r: (r % nblk, 0))
        in_specs += [tab_spec, tab_spec]
        args += list(rope_tabs)
    kvw = SWA_KV_HEADS * SWA_HEAD_DIM
    return pl.pallas_call(
        functools.partial(_swa_proj_kernel, rope),
        grid=(N_TOK // TM,),
        in_specs=in_specs,
        out_specs=[_row_spec(SWA_QW), _row_spec(SWA_KW), _row_spec(SWA_VW), _row_spec(kvw), _row_spec(kvw)],
        out_shape=[jax.ShapeDtypeStruct((N_TOK, SWA_QW), BF16), jax.ShapeDtypeStruct((N_TOK, SWA_KW), BF16),
                   jax.ShapeDtypeStruct((N_TOK, SWA_VW), BF16), jax.ShapeDtypeStruct((N_TOK, kvw), F32),
                   jax.ShapeDtypeStruct((N_TOK, kvw), F32)],
        compiler_params=_params(("arbitrary",)),
        name="swa_proj",
    )(*args)


def _swa_heads(q_ref, sink_ref, o_ref, score_fn, value_fn):
    rows = q_ref.shape[0]
    lane = lax.broadcasted_iota(I32, (rows, LANES), 1)
    low = lane < SWA_HEAD_DIM
    stacked, sinks = None, []
    for hk in range(SWA_KV_HEADS):
        for pr in range(2):
            c0 = hk * 2 * LANES + pr * LANES
            qp = q_ref[:, c0:c0 + LANES]
            for par in range(2):
                hd = hk * 4 + pr * 2 + par
                sinks.append(jnp.broadcast_to(sink_ref[:, hd:hd + 1], (rows, 1)))
                blocks = score_fn(hk, par, qp)
                if stacked is None:
                    stacked = [[] for _ in blocks]
                for kind, s in zip(stacked, blocks):
                    kind.append(s)
    scores = [jnp.concatenate(kind, axis=0) for kind in stacked]
    sk = jnp.concatenate(sinks, axis=0)
    m = sk
    for s in scores:
        m = jnp.maximum(m, jnp.max(s, axis=-1, keepdims=True))
    probs = [jnp.exp(s - m) for s in scores]
    l = jnp.exp(sk - m)
    for p in probs:
        l = l + jnp.sum(p, axis=-1, keepdims=True)
    inv = 1.0 / l
    n = 0
    for hk in range(SWA_KV_HEADS):
        for pr in range(2):
            c0 = hk * 2 * LANES + pr * LANES
            res = []
            for par in range(2):
                blk = slice(n * rows, (n + 1) * rows)
                res.append(value_fn(hk, [p[blk] for p in probs]) * inv[blk])
                n += 1
            o_ref[:, c0:c0 + LANES] = jnp.where(low, res[0], res[1]).astype(BF16)


def _swa_attn_ctx_kernel(q_ref, kd_ref, vd_ref, sink_ref, o_ref):
    def score_fn(hk, par, qp):
        a = hk * 2 * LANES + par * LANES
        return [_dot_nt(qp, kd_ref[:, a:a + LANES]) * SWA_SCALE]

    def value_fn(hk, probs):
        return _dot(probs[0].astype(BF16), vd_ref[:, hk * LANES:(hk + 1) * LANES])

    _swa_heads(q_ref, sink_ref, o_ref, score_fn, value_fn)


def _swa_attn_ctx(q, kd, vd, sink):
    blk = lambda w: pl.BlockSpec((CTX_SEQ, w), lambda b: (b, 0))
    return pl.pallas_call(
        _swa_attn_ctx_kernel,
        grid=(CTX_BATCH,),
        in_specs=[blk(SWA_QW), blk(SWA_KW), blk(SWA_VW), pl.BlockSpec((1, SWA_HEADS), lambda b: (0, 0))],
        out_specs=blk(SWA_QW),
        out_shape=jax.ShapeDtypeStruct((N_TOK, SWA_QW), BF16),
        compiler_params=_params(("arbitrary",)),
        name="swa_attn_ctx",
    )(q, kd, vd, sink)


SWA_WIN = 3 * BAND


def _swa_attn_lat_kernel(q_ref, kdl_ref, vdl_ref, kdc_ref, vdc_ref, sink_ref, o_ref):
    nb = pl.program_id(1)
    start = pl.multiple_of(jnp.clip((nb - 1) * BAND, 0, DEC_SEQ - SWA_WIN), BAND)
    qpos = nb * BAND + lax.broadcasted_iota(I32, (BAND, SWA_WIN), 0)
    kpos = start + lax.broadcasted_iota(I32, (BAND, SWA_WIN), 1)
    valid = jnp.abs(qpos - kpos) <= WINDOW

    def score_fn(hk, par, qp):
        a = hk * 2 * LANES + par * LANES
        kl = kdl_ref[pl.ds(start, SWA_WIN), a:a + LANES]
        sl = jnp.where(valid, _dot_nt(qp, kl) * SWA_SCALE, NEG_INF)
        sc = _dot_nt(qp, kdc_ref[:, a:a + LANES]) * SWA_SCALE
        return [sl, sc]

    def value_fn(hk, probs):
        vl = vdl_ref[pl.ds(start, SWA_WIN), hk * LANES:(hk + 1) * LANES]
        return _dot(probs[0].astype(BF16), vl) + _dot(probs[1].astype(BF16), vdc_ref[:, hk * LANES:(hk + 1) * LANES])

    _swa_heads(q_ref, sink_ref, o_ref, score_fn, value_fn)


def _swa_attn_lat(q, kd, vd, kd_ctx, vd_ctx, sink):
    nbands = DEC_SEQ // BAND
    return pl.pallas_call(
        _swa_attn_lat_kernel,
        grid=(DEC_BATCH, nbands),
        in_specs=[pl.BlockSpec((BAND, SWA_QW), lambda b, i: (b * nbands + i, 0)),
                  pl.BlockSpec((DEC_SEQ, SWA_KW), lambda b, i: (b, 0)),
                  pl.BlockSpec((DEC_SEQ, SWA_VW), lambda b, i: (b, 0)),
                  pl.BlockSpec((PAST_LEN, SWA_KW), lambda b, i: (b, 0)),
                  pl.BlockSpec((PAST_LEN, SWA_VW), lambda b, i: (b, 0)),
                  pl.BlockSpec((1, SWA_HEADS), lambda b, i: (0, 0))],
        out_specs=pl.BlockSpec((BAND, SWA_QW), lambda b, i: (b * nbands + i, 0)),
        out_shape=jax.ShapeDtypeStruct((N_TOK, SWA_QW), BF16),
        compiler_params=_params(("arbitrary", "arbitrary")),
        name="swa_attn_lat",
    )(q, kd, vd, kd_ctx, vd_ctx, sink)


def _post_kernel(a_ref, w_ref, x_ref, g1_ref, sc2_ref, sh2_ref, lng_ref, lnb_ref, rt_ref, x1_ref, h2_ref, lg_ref):
    y = _dot(a_ref[...], w_ref[...])
    x1 = _layer_norm(ALPHA * x_ref[...] + g1_ref[0] * y, lng_ref[...], lnb_ref[...])
    x1_ref[...] = x1
    h2 = x1 * (1.0 + sc2_ref[0]) + sh2_ref[0]
    for j in range(ROW_TILES):
        h2_ref[pl.ds(j, TM, stride=ROW_TILES), :] = h2[:, j * LANES:(j + 1) * LANES]
    rt = rt_ref[...]
    h_hi = h2.astype(BF16)
    h_lo = (h2 - h_hi.astype(F32)).astype(BF16)
    r_hi = rt.astype(BF16)
    r_lo = (rt - r_hi.astype(F32)).astype(BF16)
    lg_ref[...] = _dot(h_hi, r_hi) + (_dot(h_lo, r_hi) + _dot(h_hi, r_lo))


def _post(a, w_out, x, g1, sc2, sh2, lng, lnb, router_pad, seg_len):
    k = a.shape[1]
    return pl.pallas_call(
        _post_kernel,
        grid=(N_TOK // TM,),
        in_specs=[_row_spec(k), _full_spec((k, D_MODEL)), _row_spec(D_MODEL),
                  _seg_spec(seg_len), _seg_spec(seg_len), _seg_spec(seg_len),
                  _full_spec((1, D_MODEL)), _full_spec((1, D_MODEL)), _full_spec((D_MODEL, LANES))],
        out_specs=[_row_spec(D_MODEL), pl.BlockSpec((TM * ROW_TILES, LANES), lambda r: (r, 0)), _row_spec(LANES)],
        out_shape=[jax.ShapeDtypeStruct((N_TOK, D_MODEL), F32), jax.ShapeDtypeStruct((N_TOK * ROW_TILES, LANES), F32),
                   jax.ShapeDtypeStruct((N_TOK, LANES), F32)],
        compiler_params=_params(("arbitrary",)),
        name="mixer_out",
    )(a, w_out, x, g1, sc2, sh2, lng, lnb, router_pad)


CUM_BLK = 256
REFINE_STEPS = 16
F32_MIN_NORMAL = 1.1754943508222875e-38


def _excl_cumsum_lanes(a, upper):
    outs = []
    carry = jnp.zeros((a.shape[0], 1), F32)
    for b in range(N_TOK // CUM_BLK):
        blk = a[:, b * CUM_BLK:(b + 1) * CUM_BLK]
        outs.append(_dot(blk.astype(BF16), upper) + carry)
        carry = carry + jnp.sum(blk, axis=1, keepdims=True)
    return jnp.concatenate(outs, axis=1)


def _route_kernel(lg_ref, r_ref, tok_ref, boff_ref, idx_ref, pos_ref, rank_scr, w_scr):
    e_rows = N_EXPERTS
    lt = lg_ref[...].T[:e_rows]
    ex = jnp.exp(lt - jnp.max(lt, axis=0, keepdims=True))
    aff = ex / jnp.sum(ex, axis=0, keepdims=True)
    def count_ge(thr):
        return jnp.sum(jnp.where(aff >= thr, 1.0, 0.0), axis=1, keepdims=True)

    cur = jnp.zeros((e_rows, 1), I32)
    for b in range(30, -1, -1):
        cand = cur | jnp.int32(1 << b)
        cur = jnp.where(count_ge(lax.bitcast_convert_type(cand, F32)) >= float(CAP), cand, cur)
    lo = lax.bitcast_convert_type(cur, F32)
    hi = jnp.maximum(lax.bitcast_convert_type(cur + 1, F32), F32_MIN_NORMAL)
    for _ in range(REFINE_STEPS):
        w = hi - lo
        t1, t2, t3 = lo + 0.25 * w, lo + 0.5 * w, lo + 0.75 * w
        ok1, ok2, ok3 = (count_ge(t) >= float(CAP) for t in (t1, t2, t3))
        lo, hi = (jnp.where(ok3, t3, jnp.where(ok2, t2, jnp.where(ok1, t1, lo))),
                  jnp.where(ok1, jnp.where(ok2, jnp.where(ok3, hi, t3), t2), t1))
    gt = aff >= hi
    eq = (aff >= lo) & (aff < hi)
    need = float(CAP) - jnp.sum(jnp.where(gt, 1.0, 0.0), axis=1, keepdims=True)

    ri = lax.broadcasted_iota(I32, (CUM_BLK, CUM_BLK), 0)
    ci = lax.broadcasted_iota(I32, (CUM_BLK, CUM_BLK), 1)
    upper = jnp.where(ri < ci, 1.0, 0.0).astype(BF16)

    tie_rank = _excl_cumsum_lanes(jnp.where(eq, 1.0, 0.0), upper)
    sel = gt | (eq & (tie_rank < need))
    msk = jnp.where(sel, 1.0, 0.0)

    cnt_t = jnp.sum(msk, axis=0, keepdims=True)
    both = _excl_cumsum_lanes(jnp.concatenate([msk, jnp.broadcast_to(cnt_t, (8, N_TOK))], axis=0), upper)
    rank = both[:e_rows]
    off = both[e_rows:e_rows + 1]
    rank_scr[...] = jnp.where(sel, rank, -1.0)

    li = lax.broadcasted_iota(I32, (e_rows, LANES), 0)
    lk = lax.broadcasted_iota(I32, (e_rows, LANES), 1)
    lower = jnp.where(lk < li, 1.0, 0.0).astype(BF16)
    msk_pad = jnp.concatenate([msk, jnp.zeros((LANES - e_rows, N_TOK), F32)], axis=0).astype(BF16)
    pos = off + _dot(lower, msk_pad)

    inv = 1.0 / LANES
    pos_hi = jnp.floor(pos * inv)
    g0 = aff.astype(BF16).astype(F32)
    r1 = aff - g0
    g1 = r1.astype(BF16).astype(F32)
    g2 = (r1 - g1).astype(BF16).astype(F32)
    tt = lax.broadcasted_iota(I32, (1, N_TOK), 1).astype(F32)
    t_hi = jnp.floor(tt * inv)
    off_hi = jnp.floor(off * inv)
    misc = jnp.concatenate([t_hi, tt - t_hi * LANES, off_hi, off - off_hi * LANES, cnt_t,
                            jnp.zeros((3, N_TOK), F32)], axis=0)
    feat = jnp.concatenate([pos_hi, pos - pos_hi * LANES, g0, g1, g2, misc,
                            jnp.zeros((LANES - 5 * e_rows - 8, N_TOK), F32)], axis=0)
    table = feat.T
    tok_ref[...] = table
    w_scr[...] = table.astype(BF16)

    nblk = N_TOK // TOK_BLK
    bt = lax.broadcasted_iota(I32, (N_TOK, LANES), 0) // TOK_BLK
    bb = lax.broadcasted_iota(I32, (N_TOK, LANES), 1)
    before = jnp.where((bt < bb) & (bb <= nblk), 1.0, 0.0).astype(BF16)
    boff = _dot(jnp.broadcast_to(cnt_t, (8, N_TOK)).astype(BF16), before)
    boff_ref[...] = boff.astype(I32)

    sub = lax.broadcasted_iota(I32, (8, LANES), 0)
    ln8 = lax.broadcasted_iota(I32, (8, LANES), 1)
    rb_rows = 128

    def per_expert(e, carry):
        rk = rank_scr[pl.ds(e, 1), :]
        want = jnp.where(sub < 5, sub * e_rows + e, sub + (F_T_HI - 5))
        pick = jnp.where(ln8 == want, 1.0, 0.0).astype(BF16)
        for rb in range(CAP // rb_rows):
            slot = (rb * rb_rows + lax.broadcasted_iota(I32, (rb_rows, 1), 0)).astype(F32)
            onehot = jnp.where(rk == slot, 1.0, 0.0).astype(BF16)
            rows = _dot(onehot, w_scr[...])
            r_ref[pl.ds(e, 1), rb * rb_rows:(rb + 1) * rb_rows, :] = rows[None]
            got = _dot_nt(pick, rows.astype(BF16))
            idx_ref[pl.ds(e, 1), :, rb * rb_rows:(rb + 1) * rb_rows] = (got[5:6] * LANES + got[6:7]).astype(I32)[None]
            pos_ref[pl.ds(e, 1), :, rb * rb_rows:(rb + 1) * rb_rows] = (got[0:1] * LANES + got[1:2]).astype(I32)[None]
        return carry

    lax.fori_loop(0, e_rows, per_expert, 0)


def _route(logits):
    return pl.pallas_call(
        _route_kernel,
        grid=(1,),
        in_specs=[pl.BlockSpec((N_TOK, LANES), lambda i: (0, 0))],
        out_specs=[pl.BlockSpec((N_EXPERTS, CAP, LANES), lambda i: (0, 0, 0)),
                   pl.BlockSpec((N_TOK, LANES), lambda i: (0, 0)),
                   pl.BlockSpec((8, LANES), lambda i: (0, 0)),
                   pl.BlockSpec((N_EXPERTS, 1, CAP), lambda i: (0, 0, 0)),
                   pl.BlockSpec((N_EXPERTS, 1, CAP), lambda i: (0, 0, 0))],
        out_shape=[jax.ShapeDtypeStruct((N_EXPERTS, CAP, LANES), F32),
                   jax.ShapeDtypeStruct((N_TOK, LANES), F32),
                   jax.ShapeDtypeStruct((8, LANES), I32),
                   jax.ShapeDtypeStruct((N_EXPERTS, 1, CAP), I32),
                   jax.ShapeDtypeStruct((N_EXPERTS, 1, CAP), I32)],
        scratch_shapes=[pltpu.VMEM((N_EXPERTS, N_TOK), F32), pltpu.VMEM((N_TOK, LANES), BF16)],
        compiler_params=_params(("arbitrary",)),
        name="moe_route",
    )(logits)


ROWS_E = 2 * CAP
N_FT = EXPERT_FF // FF_TILE
GATHER_UNROLL = 8
PIECE = 256
N_PIECE = FF_TILE // PIECE


def _tile(row):
    return pl.ds(pl.multiple_of(row * ROW_TILES, ROW_TILES), ROW_TILES)


def _ffn_kernel(idx_s, pos_s, hc_hbm, hl_hbm, rc_ref, rl_ref, wg_ref, wu_ref, wd_ref, yc_hbm, yl_hbm,
                xe, xb, acc, yb, gsem, ssem):
    e = pl.program_id(0)
    f = pl.program_id(1)

    def gather_rows(ee):
        base = ee * ROWS_E

        def body(r, carry):
            tc = idx_s[base + r]
            pltpu.make_async_copy(hc_hbm.at[_tile(tc)], xe.at[_tile(r)], gsem.at[0]).start()
            tl = idx_s[base + CAP + r]
            pltpu.make_async_copy(hl_hbm.at[_tile(tl)], xe.at[_tile(CAP + r)], gsem.at[0]).start()
            return carry

        lax.fori_loop(0, CAP, body, 0, unroll=GATHER_UNROLL)

    def wait_scatter():
        pltpu.make_async_copy(yb, yb, ssem.at[0]).wait()

    @pl.when(f == 0)
    def _():
        @pl.when(e == 0)
        def _():
            gather_rows(0)

        pltpu.make_async_copy(xe, xe, gsem.at[0]).wait()
        for j in range(ROW_TILES):
            xb[:, j * LANES:(j + 1) * LANES] = xe[pl.ds(j, ROWS_E, stride=ROW_TILES), :].astype(BF16)

        @pl.when(e + 1 < N_EXPERTS)
        def _():
            gather_rows(e + 1)

        acc[...] = jnp.zeros_like(acc)

    x = xb[...]
    for p in range(N_PIECE):
        cols = slice(p * PIECE, (p + 1) * PIECE)
        g = _dot(x, wg_ref[0, :, cols].astype(BF16))
        u = _dot(x, wu_ref[0, :, cols].astype(BF16))
        hid = (g * _sigmoid(g) * u).astype(BF16)
        acc[...] += _dot(hid, wd_ref[0, cols, :].astype(BF16))

    @pl.when(f == N_FT - 1)
    def _():
        @pl.when(e > 0)
        def _():
            wait_scatter()

        lane = lax.broadcasted_iota(I32, (CAP, LANES), 1)
        mine = (lane == F_G0 + e) | (lane == F_G1 + e) | (lane == F_G2 + e)
        gate = jnp.concatenate(
            [jnp.sum(jnp.where(mine, rc_ref[0], 0.0), axis=1, keepdims=True),
             jnp.sum(jnp.where(mine, rl_ref[0], 0.0), axis=1, keepdims=True)], axis=0)
        for j in range(ROW_TILES):
            yb[pl.ds(j, ROWS_E, stride=ROW_TILES), :] = acc[:, j * LANES:(j + 1) * LANES] * gate
        base = e * ROWS_E

        def body(r, carry):
            pc = pos_s[base + r]
            pltpu.make_async_copy(yb.at[_tile(r)], yc_hbm.at[_tile(pc)], ssem.at[0]).start()
            pl_ = pos_s[base + CAP + r]
            pltpu.make_async_copy(yb.at[_tile(CAP + r)], yl_hbm.at[_tile(pl_)], ssem.at[0]).start()
            return carry

        lax.fori_loop(0, CAP, body, 0, unroll=GATHER_UNROLL)

        @pl.when(e == N_EXPERTS - 1)
        def _():
            wait_scatter()


def _moe_ffn(layer, idx_all, pos_all, h_ctx, h_lat, r_ctx, r_lat, w_gate, w_up, w_down):
    any_spec = pl.BlockSpec(memory_space=pl.ANY)
    e0 = layer * N_EXPERTS
    grid_spec = pltpu.PrefetchScalarGridSpec(
        num_scalar_prefetch=2,
        grid=(N_EXPERTS, N_FT),
        in_specs=[any_spec, any_spec,
                  pl.BlockSpec((1, CAP, LANES), lambda e, f, *_: (e, 0, 0)),
                  pl.BlockSpec((1, CAP, LANES), lambda e, f, *_: (e, 0, 0)),
                  pl.BlockSpec((1, D_MODEL, FF_TILE), lambda e, f, *_: (e0 + e, 0, f)),
                  pl.BlockSpec((1, D_MODEL, FF_TILE), lambda e, f, *_: (e0 + e, 0, f)),
                  pl.BlockSpec((1, FF_TILE, D_MODEL), lambda e, f, *_: (e0 + e, f, 0))],
        out_specs=[any_spec, any_spec],
        scratch_shapes=[pltpu.VMEM((ROWS_E * ROW_TILES, LANES), F32), pltpu.VMEM((ROWS_E, D_MODEL), BF16),
                        pltpu.VMEM((ROWS_E, D_MODEL), F32), pltpu.VMEM((ROWS_E * ROW_TILES, LANES), F32),
                        pltpu.SemaphoreType.DMA((1,)), pltpu.SemaphoreType.DMA((1,))],
    )
    y_shape = jax.ShapeDtypeStruct((N_EXPERTS * CAP * ROW_TILES, LANES), F32)
    return pl.pallas_call(
        _ffn_kernel,
        grid_spec=grid_spec,
        out_shape=[y_shape, y_shape],
        compiler_params=_params(("arbitrary", "arbitrary")),
        name="moe_ffn",
    )(idx_all, pos_all, h_ctx, h_lat, r_ctx, r_lat, w_gate, w_up, w_down)


Y_ROWS = N_EXPERTS * CAP
CHUNK_ROWS = 128
WIN_CHUNKS = 5
WIN = WIN_CHUNKS * CHUNK_ROWS
N_TBLK = N_TOK // TOK_BLK


def _combine_kernel(boff_s, ys_hbm, tok_ref, x_ref, g2_ref, lng_ref, lnb_ref, o_ref, ywin, yext, acc, wsem, esem):
    b = pl.program_id(0)

    def win_start(bb):
        return pl.multiple_of(jnp.minimum((boff_s[bb] // SUBLANES) * SUBLANES, Y_ROWS - WIN), SUBLANES)

    def win_copy(bb, slot):
        rows = pl.ds(pl.multiple_of(win_start(bb) * ROW_TILES, SUBLANES * ROW_TILES), WIN * ROW_TILES)
        return pltpu.make_async_copy(ys_hbm.at[rows], ywin.at[slot], wsem.at[slot])

    @pl.when(b == 0)
    def _():
        win_copy(0, 0).start()

    @pl.when(b + 1 < N_TBLK)
    def _():
        win_copy(b + 1, (b + 1) % 2).start()

    slot = b % 2
    start_w = win_start(b)
    hi_row = boff_s[b + 1]
    tok = tok_ref[...]
    off = tok[:, F_OFF_HI:F_OFF_HI + 1] * LANES + tok[:, F_OFF_LO:F_OFF_LO + 1]
    end = off + tok[:, F_CNT:F_CNT + 1]

    def seg_matrix(first_row, not_before):
        rows = first_row + lax.broadcasted_iota(I32, (TOK_BLK, CHUNK_ROWS), 1)
        rowsf = rows.astype(F32)
        return jnp.where((rowsf >= off) & (rowsf < end) & (rows >= not_before), 1.0, 0.0).astype(BF16)

    def as_matrix(ref, first_row):
        return jnp.concatenate([ref[pl.ds(first_row * ROW_TILES + j, CHUNK_ROWS, stride=ROW_TILES), :]
                                for j in range(ROW_TILES)], axis=1).astype(BF16)

    win_copy(b, slot).wait()
    total = None
    for k in range(WIN_CHUNKS):
        first = start_w + k * CHUNK_ROWS
        part = _dot(seg_matrix(first, first), as_matrix(ywin.at[slot], k * CHUNK_ROWS))
        total = part if total is None else total + part
    acc[...] = total

    n_over = (jnp.maximum(hi_row - (start_w + WIN), 0) + CHUNK_ROWS - 1) // CHUNK_ROWS

    def over(k, carry):
        nominal = start_w + WIN + k * CHUNK_ROWS
        first = pl.multiple_of(jnp.minimum(nominal, Y_ROWS - CHUNK_ROWS), SUBLANES)
        rows = pl.ds(pl.multiple_of(first * ROW_TILES, SUBLANES * ROW_TILES), CHUNK_ROWS * ROW_TILES)
        cp = pltpu.make_async_copy(ys_hbm.at[rows], yext, esem.at[0])
        cp.start()
        cp.wait()
        acc[...] += _dot(seg_matrix(first, nominal), as_matrix(yext, 0))
        return carry

    lax.fori_loop(0, n_over, over, 0)
    o_ref[...] = _layer_norm(ALPHA * x_ref[...] + g2_ref[0] * acc[...], lng_ref[...], lnb_ref[...])


def _combine(boff, ys, tok, x1, g2, lng, lnb, seg_len):
    grid_spec = pltpu.PrefetchScalarGridSpec(
        num_scalar_prefetch=1,
        grid=(N_TBLK,),
        in_specs=[pl.BlockSpec(memory_space=pl.ANY),
                  pl.BlockSpec((TOK_BLK, LANES), lambda b, *_: (b, 0)),
                  pl.BlockSpec((TOK_BLK, D_MODEL), lambda b, *_: (b, 0)),
                  pl.BlockSpec((1, 1, D_MODEL), lambda b, *_: (b * TOK_BLK // seg_len, 0, 0)),
                  pl.BlockSpec((1, D_MODEL), lambda b, *_: (0, 0)),
                  pl.BlockSpec((1, D_MODEL), lambda b, *_: (0, 0))],
        out_specs=pl.BlockSpec((TOK_BLK, D_MODEL), lambda b, *_: (b, 0)),
        scratch_shapes=[pltpu.VMEM((2, WIN * ROW_TILES, LANES), F32), pltpu.VMEM((CHUNK_ROWS * ROW_TILES, LANES), F32),
                        pltpu.VMEM((TOK_BLK, D_MODEL), F32),
                        pltpu.SemaphoreType.DMA((2,)), pltpu.SemaphoreType.DMA((1,))],
    )
    return pl.pallas_call(
        _combine_kernel,
        grid_spec=grid_spec,
        out_shape=jax.ShapeDtypeStruct((N_TOK, D_MODEL), F32),
        compiler_params=_params(("arbitrary",)),
        name="moe_combine",
    )(boff, ys, tok, x1, g2, lng, lnb)


def _rope_tables():
    t = jnp.arange(DEC_SEQ)
    n_freq = MLA_ROPE // 4
    inv_freq = ROPE_BASE ** (-jnp.arange(n_freq, dtype=F32) / n_freq)
    ang_r = (t // GRID_W).astype(F32)[:, None] * inv_freq[None, :]
    ang_c = (t % GRID_W).astype(F32)[:, None] * inv_freq[None, :]
    cos64 = jnp.concatenate([jnp.cos(ang_r)] * 2 + [jnp.cos(ang_c)] * 2, axis=1)
    sin64 = jnp.concatenate([jnp.sin(ang_r)] * 2 + [jnp.sin(ang_c)] * 2, axis=1)
    one, zero = jnp.ones_like(cos64), jnp.zeros_like(sin64)
    mla = (jnp.concatenate([cos64, one], axis=1), jnp.concatenate([sin64, zero], axis=1))
    swa = (jnp.concatenate([cos64, cos64], axis=1), jnp.concatenate([sin64, sin64], axis=1))
    return mla, swa


def _swa_key_layout(k):
    z = jnp.zeros_like(k)
    return jnp.stack([k, z, z, k], axis=2).reshape(k.shape[0], SWA_KW)


def _swa_value_layout(v):
    return jnp.stack([v, v], axis=2).reshape(v.shape[0], SWA_VW)


def kernel(x_prompt, x_sample, cache_mla_ckv, cache_mla_kpe, cache_swa_k, cache_swa_v, c, c_ctx, mod_w, mod_b, ln_gain, ln_bias, mla_w_in, mla_q_gain, mla_kv_gain, mla_w_q_up, mla_w_kv_up, mla_w_out, gm_w_in, gm_v_gain, gm_w_s, gm_b_s, gm_w_out, swa_w_qkv, swa_sink, swa_w_out, moe_router, moe_w_gate, moe_w_up, moe_w_down):
    d = D_MODEL
    xs = [x_prompt.reshape(N_TOK, d), x_sample.reshape(N_TOK, d)]
    seg_lens = [N_TOK, DEC_SEQ]
    cond8 = jnp.concatenate([c_ctx[None, :], c, jnp.zeros((3, d), F32)], axis=0)
    mod = _modulation_all(cond8, mod_w, mod_b)
    rope_mla, rope_swa = _rope_tables()
    w_gate_all = moe_w_gate.reshape(DEPTH * N_EXPERTS, d, EXPERT_FF)
    w_up_all = moe_w_up.reshape(DEPTH * N_EXPERTS, d, EXPERT_FF)
    w_down_all = moe_w_down.reshape(DEPTH * N_EXPERTS, EXPERT_FF, d)

    def mod_rows(i, q):
        m = mod[i, :, q * d:(q + 1) * d]
        return [m[0:1].reshape(1, 1, d), m[1:1 + DEC_BATCH].reshape(DEC_BATCH, 1, d)]

    ckv_out, kpe_out, k_out, v_out = [], [], [], []
    for i in range(DEPTH):
        kind, j = i % 3, i // 3
        sh1, sc1, g1, sh2, sc2, g2 = (mod_rows(i, q) for q in range(6))
        router_pad = jnp.pad(moe_router[i], ((0, 0), (0, LANES - N_EXPERTS)))
        lng = ln_gain[i].reshape(2, 1, d)
        lnb = ln_bias[i].reshape(2, 1, d)

        mixed = []
        if kind == 0:
            w = {
                "w_in": jnp.pad(mla_w_in[j], ((0, 0), (0, MLA_ROPE))).astype(BF16),
                "q_gain": mla_q_gain[j][None, :], "kv_gain": mla_kv_gain[j][None, :],
                "w_q": jnp.pad(mla_w_q_up[j], ((0, 0), (0, 0), (0, MLA_QK_PAD - MLA_NOPE - MLA_ROPE)))
                .reshape(MLA_Q_LORA, MLA_HEADS * MLA_QK_PAD).astype(BF16),
                "w_kv": mla_w_kv_up[j].reshape(MLA_KV_LORA, MLA_HEADS * (MLA_NOPE + MLA_V)).astype(BF16),
            }
            w_out = mla_w_out[j].astype(BF16)
            q, kv, kpe, ckv, kraw = _mla_proj(xs[0], sc1[0], sh1[0], w, seg_lens[0], None)
            ckv_out.append(ckv.reshape(CTX_BATCH, CTX_SEQ, MLA_KV_LORA))
            kpe_out.append(kraw.reshape(CTX_BATCH, CTX_SEQ, MLA_ROPE))
            mixed.append(_mla_attn_ctx(q, kv, kpe))
            q, kv, kpe, _, _ = _mla_proj(xs[1], sc1[1], sh1[1], w, seg_lens[1], rope_mla)
            kv_ctx = _matmul_bf16(cache_mla_ckv[:, j].reshape(DEC_BATCH * PAST_LEN, MLA_KV_LORA).astype(BF16), w["w_kv"])
            kpe_ctx = jnp.pad(cache_mla_kpe[:, j].reshape(DEC_BATCH * PAST_LEN, MLA_ROPE),
                              ((0, 0), (0, LANES - MLA_ROPE))).astype(BF16)
            mixed.append(_mla_attn_lat(q, kv, kpe, kv_ctx, kpe_ctx))
        elif kind == 1:
            w = {"w_in": gm_w_in[j].astype(BF16), "v_gain": gm_v_gain[j][None, :],
                 "w_s": gm_w_s[j].astype(BF16), "b_s": gm_b_s[j][:, :, None]}
            w_out = gm_w_out[j].astype(BF16)
            for g in range(2):
                mixed.append(_gmlp_in(xs[g], sc1[g], sh1[g], w, seg_lens[g]))
        else:
            wq = swa_w_qkv[j]
            nq, nkv = SWA_HEADS * SWA_HEAD_DIM, SWA_KV_HEADS * SWA_HEAD_DIM
            w_ext = jnp.concatenate(
                [wq[:, :nq],
                 _swa_key_layout(wq[:, nq:nq + nkv].reshape(d, SWA_KV_HEADS, SWA_HEAD_DIM)),
                 _swa_value_layout(wq[:, nq + nkv:].reshape(d, SWA_KV_HEADS, SWA_HEAD_DIM))], axis=1).astype(BF16)
            w_out = swa_w_out[j].astype(BF16)
            sink = swa_sink[j][None, :]
            q, kd, vd, kraw, vraw = _swa_proj(xs[0], sc1[0], sh1[0], w_ext, seg_lens[0], None)
            k_out.append(kraw.reshape(CTX_BATCH, CTX_SEQ, SWA_KV_HEADS, SWA_HEAD_DIM))
            v_out.append(vraw.reshape(CTX_BATCH, CTX_SEQ, SWA_KV_HEADS, SWA_HEAD_DIM))
            mixed.append(_swa_attn_ctx(q, kd, vd, sink))
            q, kd, vd, _, _ = _swa_proj(xs[1], sc1[1], sh1[1], w_ext, seg_lens[1], rope_swa)
            rows = DEC_BATCH * PAST_LEN
            kd_ctx = _swa_key_layout(cache_swa_k[:, j].reshape(rows, SWA_KV_HEADS, SWA_HEAD_DIM)).astype(BF16)
            vd_ctx = _swa_value_layout(cache_swa_v[:, j].reshape(rows, SWA_KV_HEADS, SWA_HEAD_DIM)).astype(BF16)
            mixed.append(_swa_attn_lat(q, kd, vd, kd_ctx, vd_ctx, sink))

        x1, h2, routed = [], [], []
        for g in range(2):
            x1_g, h2_g, lg_g = _post(mixed[g], w_out, xs[g], g1[g], sc2[g], sh2[g], lng[0], lnb[0], router_pad, seg_lens[g])
            x1.append(x1_g)
            h2.append(h2_g)
            routed.append(_route(lg_g))
        idx_all = jnp.concatenate([routed[0][3], routed[1][3]], axis=2).reshape(-1)
        pos_all = jnp.concatenate([routed[0][4], routed[1][4]], axis=2).reshape(-1)
        ys = _moe_ffn(i, idx_all, pos_all, h2[0], h2[1], routed[0][0], routed[1][0], w_gate_all, w_up_all, w_down_all)
        for g in range(2):
            boff = routed[g][2][0]
            xs[g] = _combine(boff, ys[g], routed[g][1], x1[g], g2[g], lng[1], lnb[1], seg_lens[g])

    return (xs[0].reshape(CTX_BATCH, CTX_SEQ, d), xs[1].reshape(DEC_BATCH, DEC_SEQ, d),
            jnp.stack(ckv_out, axis=1), jnp.stack(kpe_out, axis=1),
            jnp.stack(k_out, axis=1), jnp.stack(v_out, axis=1))
```

```python
import functools

import jax
import jax.numpy as jnp
from jax import lax
from jax.experimental import pallas as pl
from jax.experimental.pallas import tpu as pltpu

D_MODEL = 1024
N_TOK = 4096
DEPTH = 4
DEC_BATCH = 4
DEC_SEQ = 1024
PAST_LEN = 256
CTX_BATCH = 16
CTX_SEQ = 256
GRID_W = 64
ALPHA = (2.0 * DEPTH) ** 0.25
MLA_HEADS = 8
MLA_Q_LORA = 384
MLA_KV_LORA = 256
MLA_NOPE = 128
MLA_ROPE = 64
MLA_V = 128
MLA_QK_PAD = 256
GM_FF = 6 * D_MODEL
GM_HALF = GM_FF // 2
GM_GROUPS = 8
GM_GW = GM_HALF // GM_GROUPS
CHUNK = 128
SWA_HEADS = 16
SWA_KV_HEADS = 4
SWA_HEAD_DIM = 64
WINDOW = 128
BAND = 128
N_EXPERTS = 16
EXPERT_FF = 2 * D_MODEL
CAP = 2 * N_TOK // N_EXPERTS
ROPE_BASE = 10000.0
NEG_INF = -1e30
EPS = 1e-6
F32 = jnp.float32
BF16 = jnp.bfloat16
I32 = jnp.int32
U32 = jnp.uint32
HIGHEST = lax.Precision.HIGHEST

LANES = 128
SUBLANES = 8
ROW_TILES = D_MODEL // LANES
VMEM_LIMIT = 56 * 1024 * 1024
TM = 256
FF_TILE = 1024
TOK_BLK = 256
MOD_TN = 1536

F_POS_HI, F_POS_LO, F_G0, F_G1, F_G2 = 0, 16, 32, 48, 64
F_T_HI, F_T_LO, F_OFF_HI, F_OFF_LO, F_CNT = 80, 81, 82, 83, 84


def _params(sem, vmem=VMEM_LIMIT):
    return pltpu.CompilerParams(dimension_semantics=sem, vmem_limit_bytes=vmem)


def _dot(a, b):
    return jnp.dot(a, b, preferred_element_type=F32)


def _dot_nt(a, b):
    return lax.dot_general(a, b, (((1,), (1,)), ((), ())), preferred_element_type=F32)


def _sigmoid(x):
    return 1.0 / (1.0 + jnp.exp(-x))


def _gelu_tanh(x):
    return 0.5 * x * (1.0 + jnp.tanh(0.7978845608028654 * (x + 0.044715 * (x * x * x))))


def _layer_norm(t, gain, bias):
    mu = jnp.mean(t, axis=-1, keepdims=True)
    d = t - mu
    var = jnp.mean(d * d, axis=-1, keepdims=True)
    return d * lax.rsqrt(var + EPS) * gain + bias


def _rms_norm(t, gain):
    return t * lax.rsqrt(jnp.mean(t * t, axis=-1, keepdims=True) + EPS) * gain


def _rope128(x, cos, sin):
    lane = lax.broadcasted_iota(I32, x.shape, 1)
    first = (lane % 32) < 16
    swapped = jnp.where(first, -pltpu.roll(x, 112, 1), pltpu.roll(x, 16, 1))
    return x * cos + swapped * sin


def _mod_kernel(cond_ref, w_ref, b_ref, o_ref):
    c = cond_ref[...]
    s = c * _sigmoid(c)
    o_ref[0] = jnp.dot(s, w_ref[0], precision=HIGHEST, preferred_element_type=F32) + b_ref[0]


def _modulation_all(cond8, mod_w, mod_b):
    return pl.pallas_call(
        _mod_kernel,
        grid=(DEPTH, 6 * D_MODEL // MOD_TN),
        in_specs=[pl.BlockSpec((8, D_MODEL), lambda i, n: (0, 0)),
                  pl.BlockSpec((1, D_MODEL, MOD_TN), lambda i, n: (i, 0, n)),
                  pl.BlockSpec((1, 1, MOD_TN), lambda i, n: (i, 0, n))],
        out_specs=pl.BlockSpec((1, 8, MOD_TN), lambda i, n: (i, 0, n)),
        out_shape=jax.ShapeDtypeStruct((DEPTH, 8, 6 * D_MODEL), F32),
        compiler_params=_params(("arbitrary", "arbitrary")),
        name="modulation",
    )(cond8, mod_w, mod_b.reshape(DEPTH, 1, 6 * D_MODEL))


def _seg_spec(seg_len):
    return pl.BlockSpec((1, 1, D_MODEL), lambda r: (r * TM // seg_len, 0, 0))


def _row_spec(width):
    return pl.BlockSpec((TM, width), lambda r: (r, 0))


def _full_spec(shape):
    nd = len(shape)
    return pl.BlockSpec(shape, lambda r: (0,) * nd)


def _mla_proj_kernel(rope, x_ref, sc_ref, sh_ref, win_ref, qg_ref, kvg_ref, wq_ref, wkv_ref, *rest):
    if rope:
        cos_ref, sin_ref, q_ref, kv_ref, kpe_ref, ckv_ref, kraw_ref = rest
    else:
        q_ref, kv_ref, kpe_ref, ckv_ref, kraw_ref = rest
    h = (x_ref[...] * (1.0 + sc_ref[0]) + sh_ref[0]).astype(BF16)
    z = _dot(h, win_ref[...])
    cq = _rms_norm(z[:, :MLA_Q_LORA], qg_ref[...])
    ckv = _rms_norm(z[:, MLA_Q_LORA:MLA_Q_LORA + MLA_KV_LORA], kvg_ref[...])
    kp = z[:, MLA_Q_LORA + MLA_KV_LORA:]
    ckv_ref[...] = ckv
    kraw_ref[...] = kp[:, :MLA_ROPE]
    q = _dot(cq.astype(BF16), wq_ref[...])
    kv_ref[...] = _dot(ckv.astype(BF16), wkv_ref[...]).astype(BF16)
    if rope:
        cos, sin = cos_ref[...], sin_ref[...]
        for hd in range(MLA_HEADS):
            a = hd * MLA_QK_PAD
            q_ref[:, a:a + LANES] = q[:, a:a + LANES].astype(BF16)
            q_ref[:, a + LANES:a + 2 * LANES] = _rope128(q[:, a + LANES:a + 2 * LANES], cos, sin).astype(BF16)
        kp = _rope128(kp, cos, sin)
    else:
        q_ref[...] = q.astype(BF16)
    kpe_ref[...] = kp.astype(BF16)


def _mla_proj(x, sc, sh, w, seg_len, rope_tabs):
    rope = rope_tabs is not None
    in_specs = [_row_spec(D_MODEL), _seg_spec(seg_len), _seg_spec(seg_len),
                _full_spec(w["w_in"].shape), _full_spec((1, MLA_Q_LORA)), _full_spec((1, MLA_KV_LORA)),
                _full_spec(w["w_q"].shape), _full_spec(w["w_kv"].shape)]
    args = [x, sc, sh, w["w_in"], w["q_gain"], w["kv_gain"], w["w_q"], w["w_kv"]]
    if rope:
        nblk = DEC_SEQ // TM
        tab_spec = pl.BlockSpec((TM, LANES), lambda r: (r % nblk, 0))
        in_specs += [tab_spec, tab_spec]
        args += list(rope_tabs)
    wq = MLA_HEADS * MLA_QK_PAD
    return pl.pallas_call(
        functools.partial(_mla_proj_kernel, rope),
        grid=(N_TOK // TM,),
        in_specs=in_specs,
        out_specs=[_row_spec(wq), _row_spec(wq), _row_spec(LANES), _row_spec(MLA_KV_LORA), _row_spec(MLA_ROPE)],
        out_shape=[jax.ShapeDtypeStruct((N_TOK, wq), BF16), jax.ShapeDtypeStruct((N_TOK, wq), BF16),
                   jax.ShapeDtypeStruct((N_TOK, LANES), BF16), jax.ShapeDtypeStruct((N_TOK, MLA_KV_LORA), F32),
                   jax.ShapeDtypeStruct((N_TOK, MLA_ROPE), F32)],
        compiler_params=_params(("arbitrary",)),
        name="mla_proj",
    )(*args)


def _mm_kernel(a_ref, w_ref, o_ref):
    o_ref[...] = _dot(a_ref[...], w_ref[...]).astype(o_ref.dtype)


def _matmul_bf16(a, w):
    m, k = a.shape
    n = w.shape[1]
    return pl.pallas_call(
        _mm_kernel,
        grid=(m // TM,),
        in_specs=[pl.BlockSpec((TM, k), lambda r: (r, 0)), _full_spec(w.shape)],
        out_specs=pl.BlockSpec((TM, n), lambda r: (r, 0)),
        out_shape=jax.ShapeDtypeStruct((m, n), BF16),
        compiler_params=_params(("arbitrary",)),
        name="matmul",
    )(a, w)


MLA_SCALE = (MLA_NOPE + MLA_ROPE) ** -0.5


def _mla_attn_ctx_kernel(q_ref, kv_ref, kpe_ref, o_ref):
    kpe = kpe_ref[...]
    for hd in range(MLA_HEADS):
        a = hd * MLA_QK_PAD
        qh = q_ref[:, a:a + MLA_QK_PAD]
        kh = jnp.concatenate([kv_ref[:, a:a + LANES], kpe], axis=1)
        vh = kv_ref[:, a + LANES:a + 2 * LANES]
        s = _dot_nt(qh, kh) * MLA_SCALE
        p = jnp.exp(s - jnp.max(s, axis=-1, keepdims=True))
        l = jnp.sum(p, axis=-1, keepdims=True)
        o_ref[:, hd * MLA_V:(hd + 1) * MLA_V] = (_dot(p.astype(BF16), vh) / l).astype(BF16)


def _mla_attn_ctx(q, kv, kpe):
    wq = MLA_HEADS * MLA_QK_PAD
    blk = lambda w: pl.BlockSpec((CTX_SEQ, w), lambda b: (b, 0))
    return pl.pallas_call(
        _mla_attn_ctx_kernel,
        grid=(CTX_BATCH,),
        in_specs=[blk(wq), blk(wq), blk(LANES)],
        out_specs=blk(MLA_HEADS * MLA_V),
        out_shape=jax.ShapeDtypeStruct((N_TOK, MLA_HEADS * MLA_V), BF16),
        compiler_params=_params(("arbitrary",)),
        name="mla_attn_ctx",
    )(q, kv, kpe)


def _mla_attn_lat_kernel(q_ref, kvl_ref, kpel_ref, kvc_ref, kpec_ref, o_ref):
    kpel = kpel_ref[...]
    kpec = kpec_ref[...]
    for hd in range(MLA_HEADS):
        a = hd * MLA_QK_PAD
        qh = q_ref[:, a:a + MLA_QK_PAD]
        kc = jnp.concatenate([kvc_ref[:, a:a + LANES], kpec], axis=1)
        kl = jnp.concatenate([kvl_ref[:, a:a + LANES], kpel], axis=1)
        sc = _dot_nt(qh, kc) * MLA_SCALE
        sl = _dot_nt(qh, kl) * MLA_SCALE
        m = jnp.maximum(jnp.max(sc, axis=-1, keepdims=True), jnp.max(sl, axis=-1, keepdims=True))
        pc = jnp.exp(sc - m)
        plat = jnp.exp(sl - m)
        l = jnp.sum(pc, axis=-1, keepdims=True) + jnp.sum(plat, axis=-1, keepdims=True)
        o = _dot(pc.astype(BF16), kvc_ref[:, a + LANES:a + 2 * LANES]) \
            + _dot(plat.astype(BF16), kvl_ref[:, a + LANES:a + 2 * LANES])
        o_ref[:, hd * MLA_V:(hd + 1) * MLA_V] = (o / l).astype(BF16)


def _mla_attn_lat(q, kv, kpe, kv_ctx, kpe_ctx):
    wq = MLA_HEADS * MLA_QK_PAD
    nq = DEC_SEQ // TM
    return pl.pallas_call(
        _mla_attn_lat_kernel,
        grid=(DEC_BATCH, nq),
        in_specs=[pl.BlockSpec((TM, wq), lambda b, i: (b * nq + i, 0)),
                  pl.BlockSpec((DEC_SEQ, wq), lambda b, i: (b, 0)),
                  pl.BlockSpec((DEC_SEQ, LANES), lambda b, i: (b, 0)),
                  pl.BlockSpec((PAST_LEN, wq), lambda b, i: (b, 0)),
                  pl.BlockSpec((PAST_LEN, LANES), lambda b, i: (b, 0))],
        out_specs=pl.BlockSpec((TM, MLA_HEADS * MLA_V), lambda b, i: (b * nq + i, 0)),
        out_shape=jax.ShapeDtypeStruct((N_TOK, MLA_HEADS * MLA_V), BF16),
        compiler_params=_params(("arbitrary", "arbitrary")),
        name="mla_attn_lat",
    )(q, kv, kpe, kv_ctx, kpe_ctx)


def _gmlp_kernel(x_ref, sc_ref, sh_ref, win_ref, vg_ref, ws_ref, bs_ref, p_ref):
    h = (x_ref[...] * (1.0 + sc_ref[0]) + sh_ref[0]).astype(BF16)
    zv = _gelu_tanh(_dot(h, win_ref[:, GM_HALF:]))
    mu = jnp.mean(zv, axis=-1, keepdims=True)
    d = zv - mu
    var = jnp.mean(d * d, axis=-1, keepdims=True)
    vn = (d * lax.rsqrt(var + EPS) * vg_ref[...]).astype(BF16)
    for g in range(GM_GROUPS):
        c0 = g * GM_GW
        u = _gelu_tanh(_dot(h, win_ref[:, c0:c0 + GM_GW]))
        mixed = jnp.concatenate(
            [_dot(ws_ref[g], vn[k * CHUNK:(k + 1) * CHUNK, c0:c0 + GM_GW]) for k in range(TM // CHUNK)], axis=0)
        bias = jnp.concatenate([bs_ref[g]] * (TM // CHUNK), axis=0)
        p_ref[:, c0:c0 + GM_GW] = (u * (mixed + bias)).astype(BF16)


def _gmlp_in(x, sc, sh, w, seg_len):
    return pl.pallas_call(
        _gmlp_kernel,
        grid=(N_TOK // TM,),
        in_specs=[_row_spec(D_MODEL), _seg_spec(seg_len), _seg_spec(seg_len),
                  _full_spec((D_MODEL, GM_FF)), _full_spec((1, GM_HALF)),
                  _full_spec((GM_GROUPS, CHUNK, CHUNK)), _full_spec((GM_GROUPS, CHUNK, 1))],
        out_specs=_row_spec(GM_HALF),
        out_shape=jax.ShapeDtypeStruct((N_TOK, GM_HALF), BF16),
        compiler_params=_params(("arbitrary",)),
        name="gmlp_in",
    )(x, sc, sh, w["w_in"], w["v_gain"], w["w_s"], w["b_s"])


SWA_SCALE = SWA_HEAD_DIM ** -0.5
SWA_QW = SWA_HEADS * SWA_HEAD_DIM
SWA_KW = SWA_KV_HEADS * 2 * LANES
SWA_VW = SWA_KV_HEADS * LANES


def _swa_proj_kernel(rope, x_ref, sc_ref, sh_ref, w_ref, *rest):
    if rope:
        cos_ref, sin_ref, q_ref, kd_ref, vd_ref, kraw_ref, vraw_ref = rest
    else:
        q_ref, kd_ref, vd_ref, kraw_ref, vraw_ref = rest
    h = (x_ref[...] * (1.0 + sc_ref[0]) + sh_ref[0]).astype(BF16)
    z = _dot(h, w_ref[...])
    lane = lax.broadcasted_iota(I32, (TM, LANES), 1)
    low = lane < SWA_HEAD_DIM
    for m in range(SWA_KV_HEADS // 2):
        k0 = z[:, SWA_QW + (2 * m) * 2 * LANES:SWA_QW + (2 * m) * 2 * LANES + LANES]
        k1 = z[:, SWA_QW + (2 * m + 1) * 2 * LANES + LANES:SWA_QW + (2 * m + 2) * 2 * LANES]
        kraw_ref[:, m * LANES:(m + 1) * LANES] = jnp.where(low, k0, k1)
        v0 = z[:, SWA_QW + SWA_KW + (2 * m) * LANES:SWA_QW + SWA_KW + (2 * m + 1) * LANES]
        v1 = z[:, SWA_QW + SWA_KW + (2 * m + 1) * LANES:SWA_QW + SWA_KW + (2 * m + 2) * LANES]
        vraw_ref[:, m * LANES:(m + 1) * LANES] = jnp.where(low, v0, v1)
    vd_ref[...] = z[:, SWA_QW + SWA_KW:].astype(BF16)
    if rope:
        cos, sin = cos_ref[...], sin_ref[...]
        for j in range(SWA_QW // LANES):
            q_ref[:, j * LANES:(j + 1) * LANES] = _rope128(z[:, j * LANES:(j + 1) * LANES], cos, sin).astype(BF16)
        for j in range(SWA_KW // LANES):
            a = SWA_QW + j * LANES
            kd_ref[:, j * LANES:(j + 1) * LANES] = _rope128(z[:, a:a + LANES], cos, sin).astype(BF16)
    else:
        q_ref[...] = z[:, :SWA_QW].astype(BF16)
        kd_ref[...] = z[:, SWA_QW:SWA_QW + SWA_KW].astype(BF16)


def _swa_proj(x, sc, sh, w_ext, seg_len, rope_tabs):
    rope = rope_tabs is not None
    in_specs = [_row_spec(D_MODEL), _seg_spec(seg_len), _seg_spec(seg_len), _full_spec(w_ext.shape)]
    args = [x, sc, sh, w_ext]
    if rope:
        nblk = DEC_SEQ // TM
        tab_spec = pl.BlockSpec((TM, LANES), lambda r: (r % nblk, 0))
        in_specs += [tab_spec, tab_spec]
        args += list(rope_tabs)
    kvw = SWA_KV_HEADS * SWA_HEAD_DIM
    return pl.pallas_call(
        functools.partial(_swa_proj_kernel, rope),
        grid=(N_TOK // TM,),
        in_specs=in_specs,
        out_specs=[_row_spec(SWA_QW), _row_spec(SWA_KW), _row_spec(SWA_VW), _row_spec(kvw), _row_spec(kvw)],
        out_shape=[jax.ShapeDtypeStruct((N_TOK, SWA_QW), BF16), jax.ShapeDtypeStruct((N_TOK, SWA_KW), BF16),
                   jax.ShapeDtypeStruct((N_TOK, SWA_VW), BF16), jax.ShapeDtypeStruct((N_TOK, kvw), F32),
                   jax.ShapeDtypeStruct((N_TOK, kvw), F32)],
        compiler_params=_params(("arbitrary",)),
        name="swa_proj",
    )(*args)


def _swa_heads(q_ref, sink_ref, o_ref, score_fn, value_fn):
    rows = q_ref.shape[0]
    lane = lax.broadcasted_iota(I32, (rows, LANES), 1)
    low = lane < SWA_HEAD_DIM
    stacked, sinks = None, []
    for hk in range(SWA_KV_HEADS):
        c0 = hk * 2 * LANES
        q2 = jnp.concatenate([q_ref[:, c0:c0 + LANES], q_ref[:, c0 + LANES:c0 + 2 * LANES]], axis=0)
        for par in range(2):
            for pr in range(2):
                hd = hk * 4 + pr * 2 + par
                sinks.append(jnp.broadcast_to(sink_ref[:, hd:hd + 1], (rows, 1)))
            blocks = score_fn(hk, par, q2)
            if stacked is None:
                stacked = [[] for _ in blocks]
            for kind, s in zip(stacked, blocks):
                kind.append(s)
    scores = [jnp.concatenate(kind, axis=0) for kind in stacked]
    sk = jnp.concatenate(sinks, axis=0)
    m = sk
    for s in scores:
        m = jnp.maximum(m, jnp.max(s, axis=-1, keepdims=True))
    probs = [jnp.exp(s - m) for s in scores]
    l = jnp.exp(sk - m)
    for p in probs:
        l = l + jnp.sum(p, axis=-1, keepdims=True)
    inv = 1.0 / l
    n = 0
    for hk in range(SWA_KV_HEADS):
        res = []
        for par in range(2):
            blk = slice(n * 2 * rows, (n + 1) * 2 * rows)
            res.append(value_fn(hk, [p[blk] for p in probs]) * inv[blk])
            n += 1
        for pr in range(2):
            c0 = hk * 2 * LANES + pr * LANES
            part = slice(pr * rows, (pr + 1) * rows)
            o_ref[:, c0:c0 + LANES] = jnp.where(low, res[0][part], res[1][part]).astype(BF16)


def _swa_attn_ctx_kernel(q_ref, kd_ref, vd_ref, sink_ref, o_ref):
    def score_fn(hk, par, qp):
        a = hk * 2 * LANES + par * LANES
        return [_dot_nt(qp, kd_ref[:, a:a + LANES]) * SWA_SCALE]

    def value_fn(hk, probs):
        return _dot(probs[0].astype(BF16), vd_ref[:, hk * LANES:(hk + 1) * LANES])

    _swa_heads(q_ref, sink_ref, o_ref, score_fn, value_fn)


def _swa_attn_ctx(q, kd, vd, sink):
    blk = lambda w: pl.BlockSpec((CTX_SEQ, w), lambda b: (b, 0))
    return pl.pallas_call(
        _swa_attn_ctx_kernel,
        grid=(CTX_BATCH,),
        in_specs=[blk(SWA_QW), blk(SWA_KW), blk(SWA_VW), pl.BlockSpec((1, SWA_HEADS), lambda b: (0, 0))],
        out_specs=blk(SWA_QW),
        out_shape=jax.ShapeDtypeStruct((N_TOK, SWA_QW), BF16),
        compiler_params=_params(("arbitrary",)),
        name="swa_attn_ctx",
    )(q, kd, vd, sink)


SWA_WIN = 3 * BAND


def _swa_attn_lat_kernel(q_ref, kdl_ref, vdl_ref, kdc_ref, vdc_ref, sink_ref, o_ref):
    nb = pl.program_id(1)
    start = pl.multiple_of(jnp.clip((nb - 1) * BAND, 0, DEC_SEQ - SWA_WIN), BAND)
    qpos = nb * BAND + (lax.broadcasted_iota(I32, (2 * BAND, SWA_WIN), 0) & (BAND - 1))
    kpos = start + lax.broadcasted_iota(I32, (2 * BAND, SWA_WIN), 1)
    valid = jnp.abs(qpos - kpos) <= WINDOW

    def score_fn(hk, par, qp):
        a = hk * 2 * LANES + par * LANES
        kl = kdl_ref[pl.ds(start, SWA_WIN), a:a + LANES]
        sl = jnp.where(valid, _dot_nt(qp, kl) * SWA_SCALE, NEG_INF)
        sc = _dot_nt(qp, kdc_ref[:, a:a + LANES]) * SWA_SCALE
        return [sl, sc]

    def value_fn(hk, probs):
        vl = vdl_ref[pl.ds(start, SWA_WIN), hk * LANES:(hk + 1) * LANES]
        return _dot(probs[0].astype(BF16), vl) + _dot(probs[1].astype(BF16), vdc_ref[:, hk * LANES:(hk + 1) * LANES])

    _swa_heads(q_ref, sink_ref, o_ref, score_fn, value_fn)


def _swa_attn_lat(q, kd, vd, kd_ctx, vd_ctx, sink):
    nbands = DEC_SEQ // BAND
    return pl.pallas_call(
        _swa_attn_lat_kernel,
        grid=(DEC_BATCH, nbands),
        in_specs=[pl.BlockSpec((BAND, SWA_QW), lambda b, i: (b * nbands + i, 0)),
                  pl.BlockSpec((DEC_SEQ, SWA_KW), lambda b, i: (b, 0)),
                  pl.BlockSpec((DEC_SEQ, SWA_VW), lambda b, i: (b, 0)),
                  pl.BlockSpec((PAST_LEN, SWA_KW), lambda b, i: (b, 0)),
                  pl.BlockSpec((PAST_LEN, SWA_VW), lambda b, i: (b, 0)),
                  pl.BlockSpec((1, SWA_HEADS), lambda b, i: (0, 0))],
        out_specs=pl.BlockSpec((BAND, SWA_QW), lambda b, i: (b * nbands + i, 0)),
        out_shape=jax.ShapeDtypeStruct((N_TOK, SWA_QW), BF16),
        compiler_params=_params(("arbitrary", "arbitrary")),
        name="swa_attn_lat",
    )(q, kd, vd, kd_ctx, vd_ctx, sink)


def _post_kernel(a_ref, w_ref, x_ref, g1_ref, sc2_ref, sh2_ref, lng_ref, lnb_ref, rt_ref, x1_ref, h2_ref, lg_ref):
    y = _dot(a_ref[...], w_ref[...])
    x1 = _layer_norm(ALPHA * x_ref[...] + g1_ref[0] * y, lng_ref[...], lnb_ref[...])
    x1_ref[...] = x1
    h2 = x1 * (1.0 + sc2_ref[0]) + sh2_ref[0]
    for j in range(ROW_TILES):
        h2_ref[pl.ds(j, TM, stride=ROW_TILES), :] = h2[:, j * LANES:(j + 1) * LANES]
    rt = rt_ref[...]
    h_hi = h2.astype(BF16)
    h_lo = (h2 - h_hi.astype(F32)).astype(BF16)
    r_hi = rt.astype(BF16)
    r_lo = (rt - r_hi.astype(F32)).astype(BF16)
    lg_ref[...] = _dot(h_hi, r_hi) + (_dot(h_lo, r_hi) + _dot(h_hi, r_lo))


def _post(a, w_out, x, g1, sc2, sh2, lng, lnb, router_pad, seg_len):
    k = a.shape[1]
    return pl.pallas_call(
        _post_kernel,
        grid=(N_TOK // TM,),
        in_specs=[_row_spec(k), _full_spec((k, D_MODEL)), _row_spec(D_MODEL),
                  _seg_spec(seg_len), _seg_spec(seg_len), _seg_spec(seg_len),
                  _full_spec((1, D_MODEL)), _full_spec((1, D_MODEL)), _full_spec((D_MODEL, LANES))],
        out_specs=[_row_spec(D_MODEL), pl.BlockSpec((TM * ROW_TILES, LANES), lambda r: (r, 0)), _row_spec(LANES)],
        out_shape=[jax.ShapeDtypeStruct((N_TOK, D_MODEL), F32), jax.ShapeDtypeStruct((N_TOK * ROW_TILES, LANES), F32),
                   jax.ShapeDtypeStruct((N_TOK, LANES), F32)],
        compiler_params=_params(("arbitrary",)),
        name="mixer_out",
    )(a, w_out, x, g1, sc2, sh2, lng, lnb, router_pad)


CUM_BLK = 256
REFINE_STEPS = 16
F32_MIN_NORMAL = 1.1754943508222875e-38


def _excl_cumsum_lanes(a, upper):
    outs = []
    carry = jnp.zeros((a.shape[0], 1), F32)
    for b in range(N_TOK // CUM_BLK):
        blk = a[:, b * CUM_BLK:(b + 1) * CUM_BLK]
        outs.append(_dot(blk.astype(BF16), upper) + carry)
        carry = carry + jnp.sum(blk, axis=1, keepdims=True)
    return jnp.concatenate(outs, axis=1)


def _route_kernel(lg_ref, r_ref, tok_ref, boff_ref, idx_ref, pos_ref, rank_scr, w_scr):
    e_rows = N_EXPERTS
    lt = lg_ref[...].T[:e_rows]
    ex = jnp.exp(lt - jnp.max(lt, axis=0, keepdims=True))
    aff = ex / jnp.sum(ex, axis=0, keepdims=True)
    def count_ge(thr):
        return jnp.sum(jnp.where(aff >= thr, 1.0, 0.0), axis=1, keepdims=True)

    cur = jnp.zeros((e_rows, 1), I32)
    for b in range(30, -1, -1):
        cand = cur | jnp.int32(1 << b)
        cur = jnp.where(count_ge(lax.bitcast_convert_type(cand, F32)) >= float(CAP), cand, cur)
    lo = lax.bitcast_convert_type(cur, F32)
    hi = jnp.maximum(lax.bitcast_convert_type(cur + 1, F32), F32_MIN_NORMAL)
    for _ in range(REFINE_STEPS):
        w = hi - lo
        t1, t2, t3 = lo + 0.25 * w, lo + 0.5 * w, lo + 0.75 * w
        ok1, ok2, ok3 = (count_ge(t) >= float(CAP) for t in (t1, t2, t3))
        lo, hi = (jnp.where(ok3, t3, jnp.where(ok2, t2, jnp.where(ok1, t1, lo))),
                  jnp.where(ok1, jnp.where(ok2, jnp.where(ok3, hi, t3), t2), t1))
    gt = aff >= hi
    eq = (aff >= lo) & (aff < hi)
    need = float(CAP) - jnp.sum(jnp.where(gt, 1.0, 0.0), axis=1, keepdims=True)

    ri = lax.broadcasted_iota(I32, (CUM_BLK, CUM_BLK), 0)
    ci = lax.broadcasted_iota(I32, (CUM_BLK, CUM_BLK), 1)
    upper = jnp.where(ri < ci, 1.0, 0.0).astype(BF16)

    tie_rank = _excl_cumsum_lanes(jnp.where(eq, 1.0, 0.0), upper)
    sel = gt | (eq & (tie_rank < need))
    msk = jnp.where(sel, 1.0, 0.0)

    cnt_t = jnp.sum(msk, axis=0, keepdims=True)
    both = _excl_cumsum_lanes(jnp.concatenate([msk, jnp.broadcast_to(cnt_t, (8, N_TOK))], axis=0), upper)
    rank = both[:e_rows]
    off = both[e_rows:e_rows + 1]
    rank_scr[...] = jnp.where(sel, rank, -1.0)

    li = lax.broadcasted_iota(I32, (e_rows, LANES), 0)
    lk = lax.broadcasted_iota(I32, (e_rows, LANES), 1)
    lower = jnp.where(lk < li, 1.0, 0.0).astype(BF16)
    msk_pad = jnp.concatenate([msk, jnp.zeros((LANES - e_rows, N_TOK), F32)], axis=0).astype(BF16)
    pos = off + _dot(lower, msk_pad)

    inv = 1.0 / LANES
    pos_hi = jnp.floor(pos * inv)
    g0 = aff.astype(BF16).astype(F32)
    r1 = aff - g0
    g1 = r1.astype(BF16).astype(F32)
    g2 = (r1 - g1).astype(BF16).astype(F32)
    tt = lax.broadcasted_iota(I32, (1, N_TOK), 1).astype(F32)
    t_hi = jnp.floor(tt * inv)
    off_hi = jnp.floor(off * inv)
    misc = jnp.concatenate([t_hi, tt - t_hi * LANES, off_hi, off - off_hi * LANES, cnt_t,
                            jnp.zeros((3, N_TOK), F32)], axis=0)
    feat = jnp.concatenate([pos_hi, pos - pos_hi * LANES, g0, g1, g2, misc,
                            jnp.zeros((LANES - 5 * e_rows - 8, N_TOK), F32)], axis=0)
    table = feat.T
    tok_ref[...] = table
    w_scr[...] = table.astype(BF16)

    nblk = N_TOK // TOK_BLK
    bt = lax.broadcasted_iota(I32, (N_TOK, LANES), 0) // TOK_BLK
    bb = lax.broadcasted_iota(I32, (N_TOK, LANES), 1)
    before = jnp.where((bt < bb) & (bb <= nblk), 1.0, 0.0).astype(BF16)
    boff = _dot(jnp.broadcast_to(cnt_t, (8, N_TOK)).astype(BF16), before)
    boff_ref[...] = boff.astype(I32)

    sub = lax.broadcasted_iota(I32, (8, LANES), 0)
    ln8 = lax.broadcasted_iota(I32, (8, LANES), 1)
    rb_rows = 128

    def per_expert(e, carry):
        rk = rank_scr[pl.ds(e, 1), :]
        want = jnp.where(sub < 5, sub * e_rows + e, sub + (F_T_HI - 5))
        pick = jnp.where(ln8 == want, 1.0, 0.0).astype(BF16)
        for rb in range(CAP // rb_rows):
            slot = (rb * rb_rows + lax.broadcasted_iota(I32, (rb_rows, 1), 0)).astype(F32)
            onehot = jnp.where(rk == slot, 1.0, 0.0).astype(BF16)
            rows = _dot(onehot, w_scr[...])
            r_ref[pl.ds(e, 1), rb * rb_rows:(rb + 1) * rb_rows, :] = rows[None]
            got = _dot_nt(pick, rows.astype(BF16))
            idx_ref[pl.ds(e, 1), :, rb * rb_rows:(rb + 1) * rb_rows] = (got[5:6] * LANES + got[6:7]).astype(I32)[None]
            pos_ref[pl.ds(e, 1), :, rb * rb_rows:(rb + 1) * rb_rows] = (got[0:1] * LANES + got[1:2]).astype(I32)[None]
        return carry

    lax.fori_loop(0, e_rows, per_expert, 0)


def _route(logits):
    return pl.pallas_call(
        _route_kernel,
        grid=(1,),
        in_specs=[pl.BlockSpec((N_TOK, LANES), lambda i: (0, 0))],
        out_specs=[pl.BlockSpec((N_EXPERTS, CAP, LANES), lambda i: (0, 0, 0)),
                   pl.BlockSpec((N_TOK, LANES), lambda i: (0, 0)),
                   pl.BlockSpec((8, LANES), lambda i: (0, 0)),
                   pl.BlockSpec((N_EXPERTS, 1, CAP), lambda i: (0, 0, 0)),
                   pl.BlockSpec((N_EXPERTS, 1, CAP), lambda i: (0, 0, 0))],
        out_shape=[jax.ShapeDtypeStruct((N_EXPERTS, CAP, LANES), F32),
                   jax.ShapeDtypeStruct((N_TOK, LANES), F32),
                   jax.ShapeDtypeStruct((8, LANES), I32),
                   jax.ShapeDtypeStruct((N_EXPERTS, 1, CAP), I32),
                   jax.ShapeDtypeStruct((N_EXPERTS, 1, CAP), I32)],
        scratch_shapes=[pltpu.VMEM((N_EXPERTS, N_TOK), F32), pltpu.VMEM((N_TOK, LANES), BF16)],
        compiler_params=_params(("arbitrary",)),
        name="moe_route",
    )(logits)


ROWS_E = 2 * CAP
N_FT = EXPERT_FF // FF_TILE
GATHER_UNROLL = 8
PIECE = 256
N_PIECE = FF_TILE // PIECE


def _tile(row):
    return pl.ds(pl.multiple_of(row * ROW_TILES, ROW_TILES), ROW_TILES)


def _ffn_kernel(idx_s, pos_s, hc_hbm, hl_hbm, rc_ref, rl_ref, wg_ref, wu_ref, wd_ref, yc_hbm, yl_hbm,
                xe, xb, acc, yb, gsem, ssem):
    e = pl.program_id(0)
    f = pl.program_id(1)

    def gather_rows(ee):
        base = ee * ROWS_E

        def body(r, carry):
            tc = idx_s[base + r]
            pltpu.make_async_copy(hc_hbm.at[_tile(tc)], xe.at[_tile(r)], gsem.at[0]).start(priority=1)
            tl = idx_s[base + CAP + r]
            pltpu.make_async_copy(hl_hbm.at[_tile(tl)], xe.at[_tile(CAP + r)], gsem.at[0]).start(priority=1)
            return carry

        lax.fori_loop(0, CAP, body, 0, unroll=GATHER_UNROLL)

    def wait_scatter():
        pltpu.make_async_copy(yb, yb, ssem.at[0]).wait()

    @pl.when(f == 0)
    def _():
        @pl.when(e == 0)
        def _():
            gather_rows(0)

        pltpu.make_async_copy(xe, xe, gsem.at[0]).wait()
        for j in range(ROW_TILES):
            xb[:, j * LANES:(j + 1) * LANES] = xe[pl.ds(j, ROWS_E, stride=ROW_TILES), :].astype(BF16)

        @pl.when(e + 1 < N_EXPERTS)
        def _():
            gather_rows(e + 1)

        acc[...] = jnp.zeros_like(acc)

    x = xb[...]
    for p in range(N_PIECE):
        cols = slice(p * PIECE, (p + 1) * PIECE)
        g = _dot(x, wg_ref[0, :, cols].astype(BF16))
        u = _dot(x, wu_ref[0, :, cols].astype(BF16))
        hid = (g * _sigmoid(g) * u).astype(BF16)
        acc[...] += _dot(hid, wd_ref[0, cols, :].astype(BF16))

    @pl.when(f == N_FT - 1)
    def _():
        @pl.when(e > 0)
        def _():
            wait_scatter()

        lane = lax.broadcasted_iota(I32, (CAP, LANES), 1)
        mine = (lane == F_G0 + e) | (lane == F_G1 + e) | (lane == F_G2 + e)
        gate = jnp.concatenate(
            [jnp.sum(jnp.where(mine, rc_ref[0], 0.0), axis=1, keepdims=True),
             jnp.sum(jnp.where(mine, rl_ref[0], 0.0), axis=1, keepdims=True)], axis=0)
        for j in range(ROW_TILES):
            yb[pl.ds(j, ROWS_E, stride=ROW_TILES), :] = acc[:, j * LANES:(j + 1) * LANES] * gate
        base = e * ROWS_E

        def body(r, carry):
            pc = pos_s[base + r]
            pltpu.make_async_copy(yb.at[_tile(r)], yc_hbm.at[_tile(pc)], ssem.at[0]).start(priority=0)
            pl_ = pos_s[base + CAP + r]
            pltpu.make_async_copy(yb.at[_tile(CAP + r)], yl_hbm.at[_tile(pl_)], ssem.at[0]).start(priority=1)
            return carry

        lax.fori_loop(0, CAP, body, 0, unroll=GATHER_UNROLL)

        @pl.when(e == N_EXPERTS - 1)
        def _():
            wait_scatter()


def _moe_ffn(layer, idx_all, pos_all, h_ctx, h_lat, r_ctx, r_lat, w_gate, w_up, w_down):
    any_spec = pl.BlockSpec(memory_space=pl.ANY)
    e0 = layer * N_EXPERTS
    grid_spec = pltpu.PrefetchScalarGridSpec(
        num_scalar_prefetch=2,
        grid=(N_EXPERTS, N_FT),
        in_specs=[any_spec, any_spec,
                  pl.BlockSpec((1, CAP, LANES), lambda e, f, *_: (e, 0, 0)),
                  pl.BlockSpec((1, CAP, LANES), lambda e, f, *_: (e, 0, 0)),
                  pl.BlockSpec((1, D_MODEL, FF_TILE), lambda e, f, *_: (e0 + e, 0, f)),
                  pl.BlockSpec((1, D_MODEL, FF_TILE), lambda e, f, *_: (e0 + e, 0, f)),
                  pl.BlockSpec((1, FF_TILE, D_MODEL), lambda e, f, *_: (e0 + e, f, 0))],
        out_specs=[any_spec, any_spec],
        scratch_shapes=[pltpu.VMEM((ROWS_E * ROW_TILES, LANES), F32), pltpu.VMEM((ROWS_E, D_MODEL), BF16),
                        pltpu.VMEM((ROWS_E, D_MODEL), F32), pltpu.VMEM((ROWS_E * ROW_TILES, LANES), F32),
                        pltpu.SemaphoreType.DMA((1,)), pltpu.SemaphoreType.DMA((1,))],
    )
    y_shape = jax.ShapeDtypeStruct((N_EXPERTS * CAP * ROW_TILES, LANES), F32)
    return pl.pallas_call(
        _ffn_kernel,
        grid_spec=grid_spec,
        out_shape=[y_shape, y_shape],
        compiler_params=_params(("arbitrary", "arbitrary")),
        name="moe_ffn",
    )(idx_all, pos_all, h_ctx, h_lat, r_ctx, r_lat, w_gate, w_up, w_down)


Y_ROWS = N_EXPERTS * CAP
CHUNK_ROWS = 128
WIN_CHUNKS = 5
WIN = WIN_CHUNKS * CHUNK_ROWS
N_TBLK = N_TOK // TOK_BLK


def _combine_kernel(boff_s, ys_hbm, tok_ref, x_ref, g2_ref, lng_ref, lnb_ref, o_ref, ywin, yext, acc, wsem, esem):
    b = pl.program_id(0)

    def win_start(bb):
        return pl.multiple_of(jnp.minimum((boff_s[bb] // SUBLANES) * SUBLANES, Y_ROWS - WIN), SUBLANES)

    def win_copy(bb, slot):
        rows = pl.ds(pl.multiple_of(win_start(bb) * ROW_TILES, SUBLANES * ROW_TILES), WIN * ROW_TILES)
        return pltpu.make_async_copy(ys_hbm.at[rows], ywin.at[slot], wsem.at[slot])

    @pl.when(b == 0)
    def _():
        win_copy(0, 0).start()

    @pl.when(b + 1 < N_TBLK)
    def _():
        win_copy(b + 1, (b + 1) % 2).start()

    slot = b % 2
    start_w = win_start(b)
    hi_row = boff_s[b + 1]
    tok = tok_ref[...]
    off = tok[:, F_OFF_HI:F_OFF_HI + 1] * LANES + tok[:, F_OFF_LO:F_OFF_LO + 1]
    end = off + tok[:, F_CNT:F_CNT + 1]

    def seg_matrix(first_row, not_before):
        rows = first_row + lax.broadcasted_iota(I32, (TOK_BLK, CHUNK_ROWS), 1)
        rowsf = rows.astype(F32)
        return jnp.where((rowsf >= off) & (rowsf < end) & (rows >= not_before), 1.0, 0.0).astype(BF16)

    def as_matrix(ref, first_row):
        return jnp.concatenate([ref[pl.ds(first_row * ROW_TILES + j, CHUNK_ROWS, stride=ROW_TILES), :]
                                for j in range(ROW_TILES)], axis=1).astype(BF16)

    win_copy(b, slot).wait()
    total = None
    for k in range(WIN_CHUNKS):
        first = start_w + k * CHUNK_ROWS
        part = _dot(seg_matrix(first, first), as_matrix(ywin.at[slot], k * CHUNK_ROWS))
        total = part if total is None else total + part
    acc[...] = total

    n_over = (jnp.maximum(hi_row - (start_w + WIN), 0) + CHUNK_ROWS - 1) // CHUNK_ROWS

    def over(k, carry):
        nominal = start_w + WIN + k * CHUNK_ROWS
        first = pl.multiple_of(jnp.minimum(nominal, Y_ROWS - CHUNK_ROWS), SUBLANES)
        rows = pl.ds(pl.multiple_of(first * ROW_TILES, SUBLANES * ROW_TILES), CHUNK_ROWS * ROW_TILES)
        cp = pltpu.make_async_copy(ys_hbm.at[rows], yext, esem.at[0])
        cp.start()
        cp.wait()
        acc[...] += _dot(seg_matrix(first, nominal), as_matrix(yext, 0))
        return carry

    lax.fori_loop(0, n_over, over, 0)
    o_ref[...] = _layer_norm(ALPHA * x_ref[...] + g2_ref[0] * acc[...], lng_ref[...], lnb_ref[...])


def _combine(boff, ys, tok, x1, g2, lng, lnb, seg_len):
    grid_spec = pltpu.PrefetchScalarGridSpec(
        num_scalar_prefetch=1,
        grid=(N_TBLK,),
        in_specs=[pl.BlockSpec(memory_space=pl.ANY),
                  pl.BlockSpec((TOK_BLK, LANES), lambda b, *_: (b, 0)),
                  pl.BlockSpec((TOK_BLK, D_MODEL), lambda b, *_: (b, 0)),
                  pl.BlockSpec((1, 1, D_MODEL), lambda b, *_: (b * TOK_BLK // seg_len, 0, 0)),
                  pl.BlockSpec((1, D_MODEL), lambda b, *_: (0, 0)),
                  pl.BlockSpec((1, D_MODEL), lambda b, *_: (0, 0))],
        out_specs=pl.BlockSpec((TOK_BLK, D_MODEL), lambda b, *_: (b, 0)),
        scratch_shapes=[pltpu.VMEM((2, WIN * ROW_TILES, LANES), F32), pltpu.VMEM((CHUNK_ROWS * ROW_TILES, LANES), F32),
                        pltpu.VMEM((TOK_BLK, D_MODEL), F32),
                        pltpu.SemaphoreType.DMA((2,)), pltpu.SemaphoreType.DMA((1,))],
    )
    return pl.pallas_call(
        _combine_kernel,
        grid_spec=grid_spec,
        out_shape=jax.ShapeDtypeStruct((N_TOK, D_MODEL), F32),
        compiler_params=_params(("arbitrary",)),
        name="moe_combine",
    )(boff, ys, tok, x1, g2, lng, lnb)


def _rope_tables():
    t = jnp.arange(DEC_SEQ)
    n_freq = MLA_ROPE // 4
    inv_freq = ROPE_BASE ** (-jnp.arange(n_freq, dtype=F32) / n_freq)
    ang_r = (t // GRID_W).astype(F32)[:, None] * inv_freq[None, :]
    ang_c = (t % GRID_W).astype(F32)[:, None] * inv_freq[None, :]
    cos64 = jnp.concatenate([jnp.cos(ang_r)] * 2 + [jnp.cos(ang_c)] * 2, axis=1)
    sin64 = jnp.concatenate([jnp.sin(ang_r)] * 2 + [jnp.sin(ang_c)] * 2, axis=1)
    one, zero = jnp.ones_like(cos64), jnp.zeros_like(sin64)
    mla = (jnp.concatenate([cos64, one], axis=1), jnp.concatenate([sin64, zero], axis=1))
    swa = (jnp.concatenate([cos64, cos64], axis=1), jnp.concatenate([sin64, sin64], axis=1))
    return mla, swa


def _swa_key_layout(k):
    z = jnp.zeros_like(k)
    return jnp.stack([k, z, z, k], axis=2).reshape(k.shape[0], SWA_KW)


def _swa_value_layout(v):
    return jnp.stack([v, v], axis=2).reshape(v.shape[0], SWA_VW)


def kernel(x_prompt, x_sample, cache_mla_ckv, cache_mla_kpe, cache_swa_k, cache_swa_v, c, c_ctx, mod_w, mod_b, ln_gain, ln_bias, mla_w_in, mla_q_gain, mla_kv_gain, mla_w_q_up, mla_w_kv_up, mla_w_out, gm_w_in, gm_v_gain, gm_w_s, gm_b_s, gm_w_out, swa_w_qkv, swa_sink, swa_w_out, moe_router, moe_w_gate, moe_w_up, moe_w_down):
    d = D_MODEL
    xs = [x_prompt.reshape(N_TOK, d), x_sample.reshape(N_TOK, d)]
    seg_lens = [N_TOK, DEC_SEQ]
    cond8 = jnp.concatenate([c_ctx[None, :], c, jnp.zeros((3, d), F32)], axis=0)
    mod = _modulation_all(cond8, mod_w, mod_b)
    rope_mla, rope_swa = _rope_tables()
    w_gate_all = moe_w_gate.reshape(DEPTH * N_EXPERTS, d, EXPERT_FF)
    w_up_all = moe_w_up.reshape(DEPTH * N_EXPERTS, d, EXPERT_FF)
    w_down_all = moe_w_down.reshape(DEPTH * N_EXPERTS, EXPERT_FF, d)

    def mod_rows(i, q):
        m = mod[i, :, q * d:(q + 1) * d]
        return [m[0:1].reshape(1, 1, d), m[1:1 + DEC_BATCH].reshape(DEC_BATCH, 1, d)]

    ckv_out, kpe_out, k_out, v_out = [], [], [], []
    for i in range(DEPTH):
        kind, j = i % 3, i // 3
        sh1, sc1, g1, sh2, sc2, g2 = (mod_rows(i, q) for q in range(6))
        router_pad = jnp.pad(moe_router[i], ((0, 0), (0, LANES - N_EXPERTS)))
        lng = ln_gain[i].reshape(2, 1, d)
        lnb = ln_bias[i].reshape(2, 1, d)

        mixed = []
        if kind == 0:
            w = {
                "w_in": jnp.pad(mla_w_in[j], ((0, 0), (0, MLA_ROPE))).astype(BF16),
                "q_gain": mla_q_gain[j][None, :], "kv_gain": mla_kv_gain[j][None, :],
                "w_q": jnp.pad(mla_w_q_up[j], ((0, 0), (0, 0), (0, MLA_QK_PAD - MLA_NOPE - MLA_ROPE)))
                .reshape(MLA_Q_LORA, MLA_HEADS * MLA_QK_PAD).astype(BF16),
                "w_kv": mla_w_kv_up[j].reshape(MLA_KV_LORA, MLA_HEADS * (MLA_NOPE + MLA_V)).astype(BF16),
            }
            w_out = mla_w_out[j].astype(BF16)
            q, kv, kpe, ckv, kraw = _mla_proj(xs[0], sc1[0], sh1[0], w, seg_lens[0], None)
            ckv_out.append(ckv.reshape(CTX_BATCH, CTX_SEQ, MLA_KV_LORA))
            kpe_out.append(kraw.reshape(CTX_BATCH, CTX_SEQ, MLA_ROPE))
            mixed.append(_mla_attn_ctx(q, kv, kpe))
            q, kv, kpe, _, _ = _mla_proj(xs[1], sc1[1], sh1[1], w, seg_lens[1], rope_mla)
            kv_ctx = _matmul_bf16(cache_mla_ckv[:, j].reshape(DEC_BATCH * PAST_LEN, MLA_KV_LORA).astype(BF16), w["w_kv"])
            kpe_ctx = jnp.pad(cache_mla_kpe[:, j].reshape(DEC_BATCH * PAST_LEN, MLA_ROPE),
                              ((0, 0), (0, LANES - MLA_ROPE))).astype(BF16)
            mixed.append(_mla_attn_lat(q, kv, kpe, kv_ctx, kpe_ctx))
        elif kind == 1:
            w = {"w_in": gm_w_in[j].astype(BF16), "v_gain": gm_v_gain[j][None, :],
                 "w_s": gm_w_s[j].astype(BF16), "b_s": gm_b_s[j][:, :, None]}
            w_out = gm_w_out[j].astype(BF16)
            for g in range(2):
                mixed.append(_gmlp_in(xs[g], sc1[g], sh1[g], w, seg_lens[g]))
        else:
            wq = swa_w_qkv[j]
            nq, nkv = SWA_HEADS * SWA_HEAD_DIM, SWA_KV_HEADS * SWA_HEAD_DIM
            w_ext = jnp.concatenate(
                [wq[:, :nq],
                 _swa_key_layout(wq[:, nq:nq + nkv].reshape(d, SWA_KV_HEADS, SWA_HEAD_DIM)),
                 _swa_value_layout(wq[:, nq + nkv:].reshape(d, SWA_KV_HEADS, SWA_HEAD_DIM))], axis=1).astype(BF16)
            w_out = swa_w_out[j].astype(BF16)
            sink = swa_sink[j][None, :]
            q, kd, vd, kraw, vraw = _swa_proj(xs[0], sc1[0], sh1[0], w_ext, seg_lens[0], None)
            k_out.append(kraw.reshape(CTX_BATCH, CTX_SEQ, SWA_KV_HEADS, SWA_HEAD_DIM))
            v_out.append(vraw.reshape(CTX_BATCH, CTX_SEQ, SWA_KV_HEADS, SWA_HEAD_DIM))
            mixed.append(_swa_attn_ctx(q, kd, vd, sink))
            q, kd, vd, _, _ = _swa_proj(xs[1], sc1[1], sh1[1], w_ext, seg_lens[1], rope_swa)
            rows = DEC_BATCH * PAST_LEN
            kd_ctx = _swa_key_layout(cache_swa_k[:, j].reshape(rows, SWA_KV_HEADS, SWA_HEAD_DIM)).astype(BF16)
            vd_ctx = _swa_value_layout(cache_swa_v[:, j].reshape(rows, SWA_KV_HEADS, SWA_HEAD_DIM)).astype(BF16)
            mixed.append(_swa_attn_lat(q, kd, vd, kd_ctx, vd_ctx, sink))

        x1, h2, routed = [], [], []
        for g in range(2):
            x1_g, h2_g, lg_g = _post(mixed[g], w_out, xs[g], g1[g], sc2[g], sh2[g], lng[0], lnb[0], router_pad, seg_lens[g])
            x1.append(x1_g)
            h2.append(h2_g)
            routed.append(_route(lg_g))
        idx_all = jnp.concatenate([routed[0][3], routed[1][3]], axis=2).reshape(-1)
        pos_all = jnp.concatenate([routed[0][4], routed[1][4]], axis=2).reshape(-1)
        ys = _moe_ffn(i, idx_all, pos_all, h2[0], h2[1], routed[0][0], routed[1][0], w_gate_all, w_up_all, w_down_all)
        for g in range(2):
            boff = routed[g][2][0]
            xs[g] = _combine(boff, ys[g], routed[g][1], x1[g], g2[g], lng[1], lnb[1], seg_lens[g])

    return (xs[0].reshape(CTX_BATCH, CTX_SEQ, d), xs[1].reshape(DEC_BATCH, DEC_SEQ, d),
            jnp.stack(ckv_out, axis=1), jnp.stack(kpe_out, axis=1),
            jnp.stack(k_out, axis=1), jnp.stack(v_out, axis=1))
```

```python
import functools

import jax
import jax.numpy as jnp
from jax import lax
from jax.experimental import pallas as pl
from jax.experimental.pallas import tpu as pltpu

D_MODEL = 1024
N_TOK = 4096
DEPTH = 4
DEC_BATCH = 4
DEC_SEQ = 1024
PAST_LEN = 256
CTX_BATCH = 16
CTX_SEQ = 256
GRID_W = 64
ALPHA = (2.0 * DEPTH) ** 0.25
MLA_HEADS = 8
MLA_Q_LORA = 384
MLA_KV_LORA = 256
MLA_NOPE = 128
MLA_ROPE = 64
MLA_V = 128
MLA_QK_PAD = 256
GM_FF = 6 * D_MODEL
GM_HALF = GM_FF // 2
GM_GROUPS = 8
GM_GW = GM_HALF // GM_GROUPS
CHUNK = 128
SWA_HEADS = 16
SWA_KV_HEADS = 4
SWA_HEAD_DIM = 64
WINDOW = 128
BAND = 128
N_EXPERTS = 16
EXPERT_FF = 2 * D_MODEL
CAP = 2 * N_TOK // N_EXPERTS
ROPE_BASE = 10000.0
NEG_INF = -1e30
EPS = 1e-6
F32 = jnp.float32
BF16 = jnp.bfloat16
I32 = jnp.int32
U32 = jnp.uint32
HIGHEST = lax.Precision.HIGHEST

LANES = 128
SUBLANES = 8
ROW_TILES = D_MODEL // LANES
VMEM_LIMIT = 56 * 1024 * 1024
TM = 256
FF_TILE = 1024
TOK_BLK = 256
MOD_TN = 1536

F_POS_HI, F_POS_LO, F_G0, F_G1, F_G2 = 0, 16, 32, 48, 64
F_T_HI, F_T_LO, F_OFF_HI, F_OFF_LO, F_CNT = 80, 81, 82, 83, 84


def _params(sem, vmem=VMEM_LIMIT):
    return pltpu.CompilerParams(dimension_semantics=sem, vmem_limit_bytes=vmem)


def _dot(a, b):
    return jnp.dot(a, b, preferred_element_type=F32)


def _dot_nt(a, b):
    return lax.dot_general(a, b, (((1,), (1,)), ((), ())), preferred_element_type=F32)


def _sigmoid(x):
    return 1.0 / (1.0 + jnp.exp(-x))


def _gelu_tanh(x):
    return 0.5 * x * (1.0 + jnp.tanh(0.7978845608028654 * (x + 0.044715 * (x * x * x))))


def _layer_norm(t, gain, bias):
    mu = jnp.mean(t, axis=-1, keepdims=True)
    d = t - mu
    var = jnp.mean(d * d, axis=-1, keepdims=True)
    return d * lax.rsqrt(var + EPS) * gain + bias


def _rms_norm(t, gain):
    return t * lax.rsqrt(jnp.mean(t * t, axis=-1, keepdims=True) + EPS) * gain


def _rope128(x, cos, sin):
    lane = lax.broadcasted_iota(I32, x.shape, 1)
    first = (lane % 32) < 16
    swapped = jnp.where(first, -pltpu.roll(x, 112, 1), pltpu.roll(x, 16, 1))
    return x * cos + swapped * sin


def _mod_kernel(cond_ref, w_ref, b_ref, o_ref):
    c = cond_ref[...]
    s = c * _sigmoid(c)
    o_ref[0] = jnp.dot(s, w_ref[0], precision=HIGHEST, preferred_element_type=F32) + b_ref[0]


def _modulation_all(cond8, mod_w, mod_b):
    return pl.pallas_call(
        _mod_kernel,
        grid=(DEPTH, 6 * D_MODEL // MOD_TN),
        in_specs=[pl.BlockSpec((8, D_MODEL), lambda i, n: (0, 0)),
                  pl.BlockSpec((1, D_MODEL, MOD_TN), lambda i, n: (i, 0, n)),
                  pl.BlockSpec((1, 1, MOD_TN), lambda i, n: (i, 0, n))],
        out_specs=pl.BlockSpec((1, 8, MOD_TN), lambda i, n: (i, 0, n)),
        out_shape=jax.ShapeDtypeStruct((DEPTH, 8, 6 * D_MODEL), F32),
        compiler_params=_params(("arbitrary", "arbitrary")),
        name="modulation",
    )(cond8, mod_w, mod_b.reshape(DEPTH, 1, 6 * D_MODEL))


def _seg_spec(seg_len):
    return pl.BlockSpec((1, 1, D_MODEL), lambda r: (r * TM // seg_len, 0, 0))


def _row_spec(width):
    return pl.BlockSpec((TM, width), lambda r: (r, 0))


def _full_spec(shape):
    nd = len(shape)
    return pl.BlockSpec(shape, lambda r: (0,) * nd)


def _mla_proj_kernel(rope, x_ref, sc_ref, sh_ref, win_ref, qg_ref, kvg_ref, wq_ref, wkv_ref, *rest):
    if rope:
        cos_ref, sin_ref, q_ref, kv_ref, kpe_ref, ckv_ref, kraw_ref = rest
    else:
        q_ref, kv_ref, kpe_ref, ckv_ref, kraw_ref = rest
    h = (x_ref[...] * (1.0 + sc_ref[0]) + sh_ref[0]).astype(BF16)
    z = _dot(h, win_ref[...])
    cq = _rms_norm(z[:, :MLA_Q_LORA], qg_ref[...])
    ckv = _rms_norm(z[:, MLA_Q_LORA:MLA_Q_LORA + MLA_KV_LORA], kvg_ref[...])
    kp = z[:, MLA_Q_LORA + MLA_KV_LORA:]
    ckv_ref[...] = ckv
    kraw_ref[...] = kp[:, :MLA_ROPE]
    q = _dot(cq.astype(BF16), wq_ref[...])
    kv_ref[...] = _dot(ckv.astype(BF16), wkv_ref[...]).astype(BF16)
    if rope:
        cos, sin = cos_ref[...], sin_ref[...]
        for hd in range(MLA_HEADS):
            a = hd * MLA_QK_PAD
            q_ref[:, a:a + LANES] = q[:, a:a + LANES].astype(BF16)
            q_ref[:, a + LANES:a + 2 * LANES] = _rope128(q[:, a + LANES:a + 2 * LANES], cos, sin).astype(BF16)
        kp = _rope128(kp, cos, sin)
    else:
        q_ref[...] = q.astype(BF16)
    kpe_ref[...] = kp.astype(BF16)


def _mla_proj(x, sc, sh, w, seg_len, rope_tabs):
    rope = rope_tabs is not None
    in_specs = [_row_spec(D_MODEL), _seg_spec(seg_len), _seg_spec(seg_len),
                _full_spec(w["w_in"].shape), _full_spec((1, MLA_Q_LORA)), _full_spec((1, MLA_KV_LORA)),
                _full_spec(w["w_q"].shape), _full_spec(w["w_kv"].shape)]
    args = [x, sc, sh, w["w_in"], w["q_gain"], w["kv_gain"], w["w_q"], w["w_kv"]]
    if rope:
        nblk = DEC_SEQ // TM
        tab_spec = pl.BlockSpec((TM, LANES), lambda r: (r % nblk, 0))
        in_specs += [tab_spec, tab_spec]
        args += list(rope_tabs)
    wq = MLA_HEADS * MLA_QK_PAD
    return pl.pallas_call(
        functools.partial(_mla_proj_kernel, rope),
        grid=(N_TOK // TM,),
        in_specs=in_specs,
        out_specs=[_row_spec(wq), _row_spec(wq), _row_spec(LANES), _row_spec(MLA_KV_LORA), _row_spec(MLA_ROPE)],
        out_shape=[jax.ShapeDtypeStruct((N_TOK, wq), BF16), jax.ShapeDtypeStruct((N_TOK, wq), BF16),
                   jax.ShapeDtypeStruct((N_TOK, LANES), BF16), jax.ShapeDtypeStruct((N_TOK, MLA_KV_LORA), F32),
                   jax.ShapeDtypeStruct((N_TOK, MLA_ROPE), F32)],
        compiler_params=_params(("arbitrary",)),
        name="mla_proj",
    )(*args)


def _mm_kernel(a_ref, w_ref, o_ref):
    o_ref[...] = _dot(a_ref[...], w_ref[...]).astype(o_ref.dtype)


def _matmul_bf16(a, w):
    m, k = a.shape
    n = w.shape[1]
    return pl.pallas_call(
        _mm_kernel,
        grid=(m // TM,),
        in_specs=[pl.BlockSpec((TM, k), lambda r: (r, 0)), _full_spec(w.shape)],
        out_specs=pl.BlockSpec((TM, n), lambda r: (r, 0)),
        out_shape=jax.ShapeDtypeStruct((m, n), BF16),
        compiler_params=_params(("arbitrary",)),
        name="matmul",
    )(a, w)


MLA_SCALE = (MLA_NOPE + MLA_ROPE) ** -0.5


def _mla_attn_ctx_kernel(q_ref, kv_ref, kpe_ref, o_ref):
    kpe = kpe_ref[...]
    for hd in range(MLA_HEADS):
        a = hd * MLA_QK_PAD
        qh = q_ref[:, a:a + MLA_QK_PAD]
        kh = jnp.concatenate([kv_ref[:, a:a + LANES], kpe], axis=1)
        vh = kv_ref[:, a + LANES:a + 2 * LANES]
        s = _dot_nt(qh, kh) * MLA_SCALE
        p = jnp.exp(s - jnp.max(s, axis=-1, keepdims=True))
        l = jnp.sum(p, axis=-1, keepdims=True)
        o_ref[:, hd * MLA_V:(hd + 1) * MLA_V] = (_dot(p.astype(BF16), vh) / l).astype(BF16)


def _mla_attn_ctx(q, kv, kpe):
    wq = MLA_HEADS * MLA_QK_PAD
    blk = lambda w: pl.BlockSpec((CTX_SEQ, w), lambda b: (b, 0))
    return pl.pallas_call(
        _mla_attn_ctx_kernel,
        grid=(CTX_BATCH,),
        in_specs=[blk(wq), blk(wq), blk(LANES)],
        out_specs=blk(MLA_HEADS * MLA_V),
        out_shape=jax.ShapeDtypeStruct((N_TOK, MLA_HEADS * MLA_V), BF16),
        compiler_params=_params(("arbitrary",)),
        name="mla_attn_ctx",
    )(q, kv, kpe)


def _mla_attn_lat_kernel(q_ref, kvl_ref, kpel_ref, kvc_ref, kpec_ref, o_ref):
    kpel = kpel_ref[...]
    kpec = kpec_ref[...]
    for hd in range(MLA_HEADS):
        a = hd * MLA_QK_PAD
        qh = q_ref[:, a:a + MLA_QK_PAD]
        kc = jnp.concatenate([kvc_ref[:, a:a + LANES], kpec], axis=1)
        kl = jnp.concatenate([kvl_ref[:, a:a + LANES], kpel], axis=1)
        sc = _dot_nt(qh, kc) * MLA_SCALE
        sl = _dot_nt(qh, kl) * MLA_SCALE
        m = jnp.maximum(jnp.max(sc, axis=-1, keepdims=True), jnp.max(sl, axis=-1, keepdims=True))
        pc = jnp.exp(sc - m)
        plat = jnp.exp(sl - m)
        l = jnp.sum(pc, axis=-1, keepdims=True) + jnp.sum(plat, axis=-1, keepdims=True)
        o = _dot(pc.astype(BF16), kvc_ref[:, a + LANES:a + 2 * LANES]) \
            + _dot(plat.astype(BF16), kvl_ref[:, a + LANES:a + 2 * LANES])
        o_ref[:, hd * MLA_V:(hd + 1) * MLA_V] = (o / l).astype(BF16)


def _mla_attn_lat(q, kv, kpe, kv_ctx, kpe_ctx):
    wq = MLA_HEADS * MLA_QK_PAD
    nq = DEC_SEQ // TM
    return pl.pallas_call(
        _mla_attn_lat_kernel,
        grid=(DEC_BATCH, nq),
        in_specs=[pl.BlockSpec((TM, wq), lambda b, i: (b * nq + i, 0)),
                  pl.BlockSpec((DEC_SEQ, wq), lambda b, i: (b, 0)),
                  pl.BlockSpec((DEC_SEQ, LANES), lambda b, i: (b, 0)),
                  pl.BlockSpec((PAST_LEN, wq), lambda b, i: (b, 0)),
                  pl.BlockSpec((PAST_LEN, LANES), lambda b, i: (b, 0))],
        out_specs=pl.BlockSpec((TM, MLA_HEADS * MLA_V), lambda b, i: (b * nq + i, 0)),
        out_shape=jax.ShapeDtypeStruct((N_TOK, MLA_HEADS * MLA_V), BF16),
        compiler_params=_params(("arbitrary", "arbitrary")),
        name="mla_attn_lat",
    )(q, kv, kpe, kv_ctx, kpe_ctx)


def _gmlp_kernel(x_ref, sc_ref, sh_ref, win_ref, vg_ref, ws_ref, bs_ref, p_ref):
    h = (x_ref[...] * (1.0 + sc_ref[0]) + sh_ref[0]).astype(BF16)
    zv = _gelu_tanh(_dot(h, win_ref[:, GM_HALF:]))
    mu = jnp.mean(zv, axis=-1, keepdims=True)
    d = zv - mu
    var = jnp.mean(d * d, axis=-1, keepdims=True)
    vn = (d * lax.rsqrt(var + EPS) * vg_ref[...]).astype(BF16)
    for g in range(GM_GROUPS):
        c0 = g * GM_GW
        u = _gelu_tanh(_dot(h, win_ref[:, c0:c0 + GM_GW]))
        mixed = jnp.concatenate(
            [_dot(ws_ref[g], vn[k * CHUNK:(k + 1) * CHUNK, c0:c0 + GM_GW]) for k in range(TM // CHUNK)], axis=0)
        bias = jnp.concatenate([bs_ref[g]] * (TM // CHUNK), axis=0)
        p_ref[:, c0:c0 + GM_GW] = (u * (mixed + bias)).astype(BF16)


def _gmlp_in(x, sc, sh, w, seg_len):
    return pl.pallas_call(
        _gmlp_kernel,
        grid=(N_TOK // TM,),
        in_specs=[_row_spec(D_MODEL), _seg_spec(seg_len), _seg_spec(seg_len),
                  _full_spec((D_MODEL, GM_FF)), _full_spec((1, GM_HALF)),
                  _full_spec((GM_GROUPS, CHUNK, CHUNK)), _full_spec((GM_GROUPS, CHUNK, 1))],
        out_specs=_row_spec(GM_HALF),
        out_shape=jax.ShapeDtypeStruct((N_TOK, GM_HALF), BF16),
        compiler_params=_params(("arbitrary",)),
        name="gmlp_in",
    )(x, sc, sh, w["w_in"], w["v_gain"], w["w_s"], w["b_s"])


SWA_SCALE = SWA_HEAD_DIM ** -0.5
SWA_QW = SWA_HEADS * SWA_HEAD_DIM
SWA_KW = SWA_KV_HEADS * 2 * LANES
SWA_VW = SWA_KV_HEADS * LANES
SWA_GROUP = SWA_KV_HEADS


def _swa_proj_kernel(rope, x_ref, sc_ref, sh_ref, w_ref, *rest):
    if rope:
        cos_ref, sin_ref, q_ref, kd_ref, vd_ref, kraw_ref, vraw_ref = rest
    else:
        q_ref, kd_ref, vd_ref, kraw_ref, vraw_ref = rest
    h = (x_ref[...] * (1.0 + sc_ref[0]) + sh_ref[0]).astype(BF16)
    z = _dot(h, w_ref[...])
    lane = lax.broadcasted_iota(I32, (TM, LANES), 1)
    low = lane < SWA_HEAD_DIM
    for m in range(SWA_KV_HEADS // 2):
        k0 = z[:, SWA_QW + (2 * m) * 2 * LANES:SWA_QW + (2 * m) * 2 * LANES + LANES]
        k1 = z[:, SWA_QW + (2 * m + 1) * 2 * LANES + LANES:SWA_QW + (2 * m + 2) * 2 * LANES]
        kraw_ref[:, m * LANES:(m + 1) * LANES] = jnp.where(low, k0, k1)
        v0 = z[:, SWA_QW + SWA_KW + (2 * m) * LANES:SWA_QW + SWA_KW + (2 * m + 1) * LANES]
        v1 = z[:, SWA_QW + SWA_KW + (2 * m + 1) * LANES:SWA_QW + SWA_KW + (2 * m + 2) * LANES]
        vraw_ref[:, m * LANES:(m + 1) * LANES] = jnp.where(low, v0, v1)
    vd_ref[...] = z[:, SWA_QW + SWA_KW:].astype(BF16)
    if rope:
        cos, sin = cos_ref[...], sin_ref[...]
        for j in range(SWA_QW // LANES):
            q_ref[:, j * LANES:(j + 1) * LANES] = _rope128(z[:, j * LANES:(j + 1) * LANES], cos, sin).astype(BF16)
        for j in range(SWA_KW // LANES):
            a = SWA_QW + j * LANES
            kd_ref[:, j * LANES:(j + 1) * LANES] = _rope128(z[:, a:a + LANES], cos, sin).astype(BF16)
    else:
        q_ref[...] = z[:, :SWA_QW].astype(BF16)
        kd_ref[...] = z[:, SWA_QW:SWA_QW + SWA_KW].astype(BF16)


def _swa_proj(x, sc, sh, w_ext, seg_len, rope_tabs):
    rope = rope_tabs is not None
    in_specs = [_row_spec(D_MODEL), _seg_spec(seg_len), _seg_spec(seg_len), _full_spec(w_ext.shape)]
    args = [x, sc, sh, w_ext]
    if rope:
        nblk = DEC_SEQ // TM
        tab_spec = pl.BlockSpec((TM, LANES), lambda r: (r % nblk, 0))
        in_specs += [tab_spec, tab_spec]
        args += list(rope_tabs)
    kvw = SWA_KV_HEADS * SWA_HEAD_DIM
    return pl.pallas_call(
        functools.partial(_swa_proj_kernel, rope),
        grid=(N_TOK // TM,),
        in_specs=in_specs,
        out_specs=[_row_spec(SWA_QW), _row_spec(SWA_KW), _row_spec(SWA_VW), _row_spec(kvw), _row_spec(kvw)],
        out_shape=[jax.ShapeDtypeStruct((N_TOK, SWA_QW), BF16), jax.ShapeDtypeStruct((N_TOK, SWA_KW), BF16),
                   jax.ShapeDtypeStruct((N_TOK, SWA_VW), BF16), jax.ShapeDtypeStruct((N_TOK, kvw), F32),
                   jax.ShapeDtypeStruct((N_TOK, kvw), F32)],
        compiler_params=_params(("arbitrary",)),
        name="swa_proj",
    )(*args)


def _swa_heads(q_ref, sink_ref, o_ref, score_fn, value_fn):
    rows = q_ref.shape[0]
    lane = lax.broadcasted_iota(I32, (rows, LANES), 1)
    low = lane < SWA_HEAD_DIM
    for hk0 in range(0, SWA_KV_HEADS, SWA_GROUP):
        stacked, sinks = None, []
        for hk in range(hk0, hk0 + SWA_GROUP):
            c0 = hk * 2 * LANES
            q2 = jnp.concatenate([q_ref[:, c0:c0 + LANES], q_ref[:, c0 + LANES:c0 + 2 * LANES]], axis=0)
            for par in range(2):
                for pr in range(2):
                    hd = hk * 4 + pr * 2 + par
                    sinks.append(jnp.broadcast_to(sink_ref[:, hd:hd + 1], (rows, 1)))
                blocks = score_fn(hk, par, q2)
                if stacked is None:
                    stacked = [[] for _ in blocks]
                for kind, s in zip(stacked, blocks):
                    kind.append(s)
        scores = [jnp.concatenate(kind, axis=0) for kind in stacked]
        sk = jnp.concatenate(sinks, axis=0)
        m = sk
        for s in scores:
            m = jnp.maximum(m, jnp.max(s, axis=-1, keepdims=True))
        probs = [jnp.exp(s - m) for s in scores]
        l = jnp.exp(sk - m)
        for p in probs:
            l = l + jnp.sum(p, axis=-1, keepdims=True)
        inv = 1.0 / l
        n = 0
        for hk in range(hk0, hk0 + SWA_GROUP):
            res = []
            for par in range(2):
                blk = slice(n * 2 * rows, (n + 1) * 2 * rows)
                res.append(value_fn(hk, [p[blk] for p in probs]) * inv[blk])
                n += 1
            for pr in range(2):
                c0 = hk * 2 * LANES + pr * LANES
                part = slice(pr * rows, (pr + 1) * rows)
                o_ref[:, c0:c0 + LANES] = jnp.where(low, res[0][part], res[1][part]).astype(BF16)


def _swa_attn_ctx_kernel(q_ref, kd_ref, vd_ref, sink_ref, o_ref):
    def score_fn(hk, par, qp):
        a = hk * 2 * LANES + par * LANES
        return [_dot_nt(qp, kd_ref[:, a:a + LANES]) * SWA_SCALE]

    def value_fn(hk, probs):
        return _dot(probs[0].astype(BF16), vd_ref[:, hk * LANES:(hk + 1) * LANES])

    _swa_heads(q_ref, sink_ref, o_ref, score_fn, value_fn)


def _swa_attn_ctx(q, kd, vd, sink):
    blk = lambda w: pl.BlockSpec((CTX_SEQ, w), lambda b: (b, 0))
    return pl.pallas_call(
        _swa_attn_ctx_kernel,
        grid=(CTX_BATCH,),
        in_specs=[blk(SWA_QW), blk(SWA_KW), blk(SWA_VW), pl.BlockSpec((1, SWA_HEADS), lambda b: (0, 0))],
        out_specs=blk(SWA_QW),
        out_shape=jax.ShapeDtypeStruct((N_TOK, SWA_QW), BF16),
        compiler_params=_params(("arbitrary",)),
        name="swa_attn_ctx",
    )(q, kd, vd, sink)


SWA_WIN = 3 * BAND


def _swa_attn_lat_kernel(q_ref, kdl_ref, vdl_ref, kdc_ref, vdc_ref, sink_ref, o_ref):
    nb = pl.program_id(1)
    start = pl.multiple_of(jnp.clip((nb - 1) * BAND, 0, DEC_SEQ - SWA_WIN), BAND)
    qpos = nb * BAND + (lax.broadcasted_iota(I32, (2 * BAND, SWA_WIN), 0) & (BAND - 1))
    kpos = start + lax.broadcasted_iota(I32, (2 * BAND, SWA_WIN), 1)
    valid = jnp.abs(qpos - kpos) <= WINDOW

    def score_fn(hk, par, qp):
        a = hk * 2 * LANES + par * LANES
        kl = kdl_ref[pl.ds(start, SWA_WIN), a:a + LANES]
        sl = jnp.where(valid, _dot_nt(qp, kl) * SWA_SCALE, NEG_INF)
        sc = _dot_nt(qp, kdc_ref[:, a:a + LANES]) * SWA_SCALE
        return [sl, sc]

    def value_fn(hk, probs):
        vl = vdl_ref[pl.ds(start, SWA_WIN), hk * LANES:(hk + 1) * LANES]
        return _dot(probs[0].astype(BF16), vl) + _dot(probs[1].astype(BF16), vdc_ref[:, hk * LANES:(hk + 1) * LANES])

    _swa_heads(q_ref, sink_ref, o_ref, score_fn, value_fn)


def _swa_attn_lat(q, kd, vd, kd_ctx, vd_ctx, sink):
    nbands = DEC_SEQ // BAND
    return pl.pallas_call(
        _swa_attn_lat_kernel,
        grid=(DEC_BATCH, nbands),
        in_specs=[pl.BlockSpec((BAND, SWA_QW), lambda b, i: (b * nbands + i, 0)),
                  pl.BlockSpec((DEC_SEQ, SWA_KW), lambda b, i: (b, 0)),
                  pl.BlockSpec((DEC_SEQ, SWA_VW), lambda b, i: (b, 0)),
                  pl.BlockSpec((PAST_LEN, SWA_KW), lambda b, i: (b, 0)),
                  pl.BlockSpec((PAST_LEN, SWA_VW), lambda b, i: (b, 0)),
                  pl.BlockSpec((1, SWA_HEADS), lambda b, i: (0, 0))],
        out_specs=pl.BlockSpec((BAND, SWA_QW), lambda b, i: (b * nbands + i, 0)),
        out_shape=jax.ShapeDtypeStruct((N_TOK, SWA_QW), BF16),
        compiler_params=_params(("arbitrary", "arbitrary")),
        name="swa_attn_lat",
    )(q, kd, vd, kd_ctx, vd_ctx, sink)


def _post_kernel(a_ref, w_ref, x_ref, g1_ref, sc2_ref, sh2_ref, lng_ref, lnb_ref, rt_ref, x1_ref, h2_ref, lg_ref):
    y = _dot(a_ref[...], w_ref[...])
    x1 = _layer_norm(ALPHA * x_ref[...] + g1_ref[0] * y, lng_ref[...], lnb_ref[...])
    x1_ref[...] = x1
    h2 = x1 * (1.0 + sc2_ref[0]) + sh2_ref[0]
    for j in range(ROW_TILES):
        h2_ref[pl.ds(j, TM, stride=ROW_TILES), :] = h2[:, j * LANES:(j + 1) * LANES]
    rt = rt_ref[...]
    h_hi = h2.astype(BF16)
    h_lo = (h2 - h_hi.astype(F32)).astype(BF16)
    r_hi = rt.astype(BF16)
    r_lo = (rt - r_hi.astype(F32)).astype(BF16)
    lg_ref[...] = _dot(h_hi, r_hi) + (_dot(h_lo, r_hi) + _dot(h_hi, r_lo))


def _post(a, w_out, x, g1, sc2, sh2, lng, lnb, router_pad, seg_len):
    k = a.shape[1]
    return pl.pallas_call(
        _post_kernel,
        grid=(N_TOK // TM,),
        in_specs=[_row_spec(k), _full_spec((k, D_MODEL)), _row_spec(D_MODEL),
                  _seg_spec(seg_len), _seg_spec(seg_len), _seg_spec(seg_len),
                  _full_spec((1, D_MODEL)), _full_spec((1, D_MODEL)), _full_spec((D_MODEL, LANES))],
        out_specs=[_row_spec(D_MODEL), pl.BlockSpec((TM * ROW_TILES, LANES), lambda r: (r, 0)), _row_spec(LANES)],
        out_shape=[jax.ShapeDtypeStruct((N_TOK, D_MODEL), F32), jax.ShapeDtypeStruct((N_TOK * ROW_TILES, LANES), F32),
                   jax.ShapeDtypeStruct((N_TOK, LANES), F32)],
        compiler_params=_params(("arbitrary",)),
        name="mixer_out",
    )(a, w_out, x, g1, sc2, sh2, lng, lnb, router_pad)


CUM_BLK = 256
REFINE_STEPS = 16
F32_MIN_NORMAL = 1.1754943508222875e-38


def _excl_cumsum_lanes(a, upper):
    outs = []
    carry = jnp.zeros((a.shape[0], 1), F32)
    for b in range(N_TOK // CUM_BLK):
        blk = a[:, b * CUM_BLK:(b + 1) * CUM_BLK]
        outs.append(_dot(blk.astype(BF16), upper) + carry)
        carry = carry + jnp.sum(blk, axis=1, keepdims=True)
    return jnp.concatenate(outs, axis=1)


def _route_kernel(lg_ref, r_ref, tok_ref, boff_ref, idx_ref, pos_ref, rank_scr, w_scr):
    e_rows = N_EXPERTS
    lt = lg_ref[...].T[:e_rows]
    ex = jnp.exp(lt - jnp.max(lt, axis=0, keepdims=True))
    aff = ex / jnp.sum(ex, axis=0, keepdims=True)
    def count_ge(thr):
        return jnp.sum(jnp.where(aff >= thr, 1.0, 0.0), axis=1, keepdims=True)

    cur = jnp.zeros((e_rows, 1), I32)
    for b in range(30, -1, -1):
        cand = cur | jnp.int32(1 << b)
        cur = jnp.where(count_ge(lax.bitcast_convert_type(cand, F32)) >= float(CAP), cand, cur)
    lo = lax.bitcast_convert_type(cur, F32)
    hi = jnp.maximum(lax.bitcast_convert_type(cur + 1, F32), F32_MIN_NORMAL)
    for _ in range(REFINE_STEPS):
        w = hi - lo
        t1, t2, t3 = lo + 0.25 * w, lo + 0.5 * w, lo + 0.75 * w
        ok1, ok2, ok3 = (count_ge(t) >= float(CAP) for t in (t1, t2, t3))
        lo, hi = (jnp.where(ok3, t3, jnp.where(ok2, t2, jnp.where(ok1, t1, lo))),
                  jnp.where(ok1, jnp.where(ok2, jnp.where(ok3, hi, t3), t2), t1))
    gt = aff >= hi
    eq = (aff >= lo) & (aff < hi)
    need = float(CAP) - jnp.sum(jnp.where(gt, 1.0, 0.0), axis=1, keepdims=True)

    ri = lax.broadcasted_iota(I32, (CUM_BLK, CUM_BLK), 0)
    ci = lax.broadcasted_iota(I32, (CUM_BLK, CUM_BLK), 1)
    upper = jnp.where(ri < ci, 1.0, 0.0).astype(BF16)

    tie_rank = _excl_cumsum_lanes(jnp.where(eq, 1.0, 0.0), upper)
    sel = gt | (eq & (tie_rank < need))
    msk = jnp.where(sel, 1.0, 0.0)

    cnt_t = jnp.sum(msk, axis=0, keepdims=True)
    both = _excl_cumsum_lanes(jnp.concatenate([msk, jnp.broadcast_to(cnt_t, (8, N_TOK))], axis=0), upper)
    rank = both[:e_rows]
    off = both[e_rows:e_rows + 1]
    rank_scr[...] = jnp.where(sel, rank, -1.0)

    li = lax.broadcasted_iota(I32, (e_rows, LANES), 0)
    lk = lax.broadcasted_iota(I32, (e_rows, LANES), 1)
    lower = jnp.where(lk < li, 1.0, 0.0).astype(BF16)
    msk_pad = jnp.concatenate([msk, jnp.zeros((LANES - e_rows, N_TOK), F32)], axis=0).astype(BF16)
    pos = off + _dot(lower, msk_pad)

    inv = 1.0 / LANES
    pos_hi = jnp.floor(pos * inv)
    g0 = aff.astype(BF16).astype(F32)
    r1 = aff - g0
    g1 = r1.astype(BF16).astype(F32)
    g2 = (r1 - g1).astype(BF16).astype(F32)
    tt = lax.broadcasted_iota(I32, (1, N_TOK), 1).astype(F32)
    t_hi = jnp.floor(tt * inv)
    off_hi = jnp.floor(off * inv)
    misc = jnp.concatenate([t_hi, tt - t_hi * LANES, off_hi, off - off_hi * LANES, cnt_t,
                            jnp.zeros((3, N_TOK), F32)], axis=0)
    feat = jnp.concatenate([pos_hi, pos - pos_hi * LANES, g0, g1, g2, misc,
                            jnp.zeros((LANES - 5 * e_rows - 8, N_TOK), F32)], axis=0)
    table = feat.T
    tok_ref[...] = table
    w_scr[...] = table.astype(BF16)

    nblk = N_TOK // TOK_BLK
    bt = lax.broadcasted_iota(I32, (N_TOK, LANES), 0) // TOK_BLK
    bb = lax.broadcasted_iota(I32, (N_TOK, LANES), 1)
    before = jnp.where((bt < bb) & (bb <= nblk), 1.0, 0.0).astype(BF16)
    boff = _dot(jnp.broadcast_to(cnt_t, (8, N_TOK)).astype(BF16), before)
    boff_ref[...] = boff.astype(I32)

    sub = lax.broadcasted_iota(I32, (8, LANES), 0)
    ln8 = lax.broadcasted_iota(I32, (8, LANES), 1)
    rb_rows = 128

    def per_expert(e, carry):
        rk = rank_scr[pl.ds(e, 1), :]
        want = jnp.where(sub < 5, sub * e_rows + e, sub + (F_T_HI - 5))
        pick = jnp.where(ln8 == want, 1.0, 0.0).astype(BF16)
        for rb in range(CAP // rb_rows):
            slot = (rb * rb_rows + lax.broadcasted_iota(I32, (rb_rows, 1), 0)).astype(F32)
            onehot = jnp.where(rk == slot, 1.0, 0.0).astype(BF16)
            rows = _dot(onehot, w_scr[...])
            r_ref[pl.ds(e, 1), rb * rb_rows:(rb + 1) * rb_rows, :] = rows[None]
            got = _dot_nt(pick, rows.astype(BF16))
            idx_ref[pl.ds(e, 1), :, rb * rb_rows:(rb + 1) * rb_rows] = (got[5:6] * LANES + got[6:7]).astype(I32)[None]
            pos_ref[pl.ds(e, 1), :, rb * rb_rows:(rb + 1) * rb_rows] = (got[0:1] * LANES + got[1:2]).astype(I32)[None]
        return carry

    lax.fori_loop(0, e_rows, per_expert, 0)


def _route(logits):
    return pl.pallas_call(
        _route_kernel,
        grid=(1,),
        in_specs=[pl.BlockSpec((N_TOK, LANES), lambda i: (0, 0))],
        out_specs=[pl.BlockSpec((N_EXPERTS, CAP, LANES), lambda i: (0, 0, 0)),
                   pl.BlockSpec((N_TOK, LANES), lambda i: (0, 0)),
                   pl.BlockSpec((8, LANES), lambda i: (0, 0)),
                   pl.BlockSpec((N_EXPERTS, 1, CAP), lambda i: (0, 0, 0)),
                   pl.BlockSpec((N_EXPERTS, 1, CAP), lambda i: (0, 0, 0))],
        out_shape=[jax.ShapeDtypeStruct((N_EXPERTS, CAP, LANES), F32),
                   jax.ShapeDtypeStruct((N_TOK, LANES), F32),
                   jax.ShapeDtypeStruct((8, LANES), I32),
                   jax.ShapeDtypeStruct((N_EXPERTS, 1, CAP), I32),
                   jax.ShapeDtypeStruct((N_EXPERTS, 1, CAP), I32)],
        scratch_shapes=[pltpu.VMEM((N_EXPERTS, N_TOK), F32), pltpu.VMEM((N_TOK, LANES), BF16)],
        compiler_params=_params(("arbitrary",)),
        name="moe_route",
    )(logits)


ROWS_E = 2 * CAP
N_FT = EXPERT_FF // FF_TILE
GATHER_UNROLL = 8
PIECE = 256
N_PIECE = FF_TILE // PIECE


def _tile(row):
    return pl.ds(pl.multiple_of(row * ROW_TILES, ROW_TILES), ROW_TILES)


def _ffn_kernel(idx_s, pos_s, hc_hbm, hl_hbm, rc_ref, rl_ref, wg_ref, wu_ref, wd_ref, yc_hbm, yl_hbm,
                xe, xb, acc, yb, gsem, ssem):
    e = pl.program_id(0)
    f = pl.program_id(1)

    def gather_rows(ee):
        base = ee * ROWS_E

        def body(r, carry):
            tc = idx_s[base + r]
            pltpu.make_async_copy(hc_hbm.at[_tile(tc)], xe.at[_tile(r)], gsem.at[0]).start(priority=0)
            tl = idx_s[base + CAP + r]
            pltpu.make_async_copy(hl_hbm.at[_tile(tl)], xe.at[_tile(CAP + r)], gsem.at[0]).start(priority=1)
            return carry

        lax.fori_loop(0, CAP, body, 0, unroll=GATHER_UNROLL)

    def wait_scatter():
        pltpu.make_async_copy(yb, yb, ssem.at[0]).wait()

    @pl.when(f == 0)
    def _():
        @pl.when(e == 0)
        def _():
            gather_rows(0)

        pltpu.make_async_copy(xe, xe, gsem.at[0]).wait()
        for j in range(ROW_TILES):
            xb[:, j * LANES:(j + 1) * LANES] = xe[pl.ds(j, ROWS_E, stride=ROW_TILES), :].astype(BF16)

        @pl.when(e + 1 < N_EXPERTS)
        def _():
            gather_rows(e + 1)

        acc[...] = jnp.zeros_like(acc)

    x = xb[...]
    for p in range(N_PIECE):
        cols = slice(p * PIECE, (p + 1) * PIECE)
        g = _dot(x, wg_ref[0, :, cols].astype(BF16))
        u = _dot(x, wu_ref[0, :, cols].astype(BF16))
        hid = (g * _sigmoid(g) * u).astype(BF16)
        acc[...] += _dot(hid, wd_ref[0, cols, :].astype(BF16))

    @pl.when(f == N_FT - 1)
    def _():
        @pl.when(e > 0)
        def _():
            wait_scatter()

        lane = lax.broadcasted_iota(I32, (CAP, LANES), 1)
        mine = (lane == F_G0 + e) | (lane == F_G1 + e) | (lane == F_G2 + e)
        gate = jnp.concatenate(
            [jnp.sum(jnp.where(mine, rc_ref[0], 0.0), axis=1, keepdims=True),
             jnp.sum(jnp.where(mine, rl_ref[0], 0.0), axis=1, keepdims=True)], axis=0)
        for j in range(ROW_TILES):
            yb[pl.ds(j, ROWS_E, stride=ROW_TILES), :] = acc[:, j * LANES:(j + 1) * LANES] * gate
        base = e * ROWS_E

        def body(r, carry):
            pc = pos_s[base + r]
            pltpu.make_async_copy(yb.at[_tile(r)], yc_hbm.at[_tile(pc)], ssem.at[0]).start(priority=0)
            pl_ = pos_s[base + CAP + r]
            pltpu.make_async_copy(yb.at[_tile(CAP + r)], yl_hbm.at[_tile(pl_)], ssem.at[0]).start(priority=1)
            return carry

        lax.fori_loop(0, CAP, body, 0, unroll=GATHER_UNROLL)

        @pl.when(e == N_EXPERTS - 1)
        def _():
            wait_scatter()


def _moe_ffn(layer, idx_all, pos_all, h_ctx, h_lat, r_ctx, r_lat, w_gate, w_up, w_down):
    any_spec = pl.BlockSpec(memory_space=pl.ANY)
    e0 = layer * N_EXPERTS
    grid_spec = pltpu.PrefetchScalarGridSpec(
        num_scalar_prefetch=2,
        grid=(N_EXPERTS, N_FT),
        in_specs=[any_spec, any_spec,
                  pl.BlockSpec((1, CAP, LANES), lambda e, f, *_: (e, 0, 0)),
                  pl.BlockSpec((1, CAP, LANES), lambda e, f, *_: (e, 0, 0)),
                  pl.BlockSpec((1, D_MODEL, FF_TILE), lambda e, f, *_: (e0 + e, 0, f)),
                  pl.BlockSpec((1, D_MODEL, FF_TILE), lambda e, f, *_: (e0 + e, 0, f)),
                  pl.BlockSpec((1, FF_TILE, D_MODEL), lambda e, f, *_: (e0 + e, f, 0))],
        out_specs=[any_spec, any_spec],
        scratch_shapes=[pltpu.VMEM((ROWS_E * ROW_TILES, LANES), F32), pltpu.VMEM((ROWS_E, D_MODEL), BF16),
                        pltpu.VMEM((ROWS_E, D_MODEL), F32), pltpu.VMEM((ROWS_E * ROW_TILES, LANES), F32),
                        pltpu.SemaphoreType.DMA((1,)), pltpu.SemaphoreType.DMA((1,))],
    )
    y_shape = jax.ShapeDtypeStruct((N_EXPERTS * CAP * ROW_TILES, LANES), F32)
    return pl.pallas_call(
        _ffn_kernel,
        grid_spec=grid_spec,
        out_shape=[y_shape, y_shape],
        compiler_params=_params(("arbitrary", "arbitrary")),
        name="moe_ffn",
    )(idx_all, pos_all, h_ctx, h_lat, r_ctx, r_lat, w_gate, w_up, w_down)


Y_ROWS = N_EXPERTS * CAP
CHUNK_ROWS = 128
WIN_CHUNKS = 5
WIN = WIN_CHUNKS * CHUNK_ROWS
N_TBLK = N_TOK // TOK_BLK


def _combine_kernel(boff_s, ys_hbm, tok_ref, x_ref, g2_ref, lng_ref, lnb_ref, o_ref, ywin, yext, acc, wsem, esem):
    b = pl.program_id(0)

    def win_start(bb):
        return pl.multiple_of(jnp.minimum((boff_s[bb] // SUBLANES) * SUBLANES, Y_ROWS - WIN), SUBLANES)

    def win_copy(bb, slot):
        rows = pl.ds(pl.multiple_of(win_start(bb) * ROW_TILES, SUBLANES * ROW_TILES), WIN * ROW_TILES)
        return pltpu.make_async_copy(ys_hbm.at[rows], ywin.at[slot], wsem.at[slot])

    @pl.when(b == 0)
    def _():
        win_copy(0, 0).start()

    @pl.when(b + 1 < N_TBLK)
    def _():
        win_copy(b + 1, (b + 1) % 2).start()

    slot = b % 2
    start_w = win_start(b)
    hi_row = boff_s[b + 1]
    tok = tok_ref[...]
    off = tok[:, F_OFF_HI:F_OFF_HI + 1] * LANES + tok[:, F_OFF_LO:F_OFF_LO + 1]
    end = off + tok[:, F_CNT:F_CNT + 1]

    def seg_matrix(first_row, not_before):
        rows = first_row + lax.broadcasted_iota(I32, (TOK_BLK, CHUNK_ROWS), 1)
        rowsf = rows.astype(F32)
        return jnp.where((rowsf >= off) & (rowsf < end) & (rows >= not_before), 1.0, 0.0).astype(BF16)

    def as_matrix(ref, first_row):
        return jnp.concatenate([ref[pl.ds(first_row * ROW_TILES + j, CHUNK_ROWS, stride=ROW_TILES), :]
                                for j in range(ROW_TILES)], axis=1).astype(BF16)

    win_copy(b, slot).wait()
    total = None
    for k in range(WIN_CHUNKS):
        first = start_w + k * CHUNK_ROWS
        part = _dot(seg_matrix(first, first), as_matrix(ywin.at[slot], k * CHUNK_ROWS))
        total = part if total is None else total + part
    acc[...] = total

    n_over = (jnp.maximum(hi_row - (start_w + WIN), 0) + CHUNK_ROWS - 1) // CHUNK_ROWS

    def over(k, carry):
        nominal = start_w + WIN + k * CHUNK_ROWS
        first = pl.multiple_of(jnp.minimum(nominal, Y_ROWS - CHUNK_ROWS), SUBLANES)
        rows = pl.ds(pl.multiple_of(first * ROW_TILES, SUBLANES * ROW_TILES), CHUNK_ROWS * ROW_TILES)
        cp = pltpu.make_async_copy(ys_hbm.at[rows], yext, esem.at[0])
        cp.start()
        cp.wait()
        acc[...] += _dot(seg_matrix(first, nominal), as_matrix(yext, 0))
        return carry

    lax.fori_loop(0, n_over, over, 0)
    o_ref[...] = _layer_norm(ALPHA * x_ref[...] + g2_ref[0] * acc[...], lng_ref[...], lnb_ref[...])


def _combine(boff, ys, tok, x1, g2, lng, lnb, seg_len):
    grid_spec = pltpu.PrefetchScalarGridSpec(
        num_scalar_prefetch=1,
        grid=(N_TBLK,),
        in_specs=[pl.BlockSpec(memory_space=pl.ANY),
                  pl.BlockSpec((TOK_BLK, LANES), lambda b, *_: (b, 0)),
                  pl.BlockSpec((TOK_BLK, D_MODEL), lambda b, *_: (b, 0)),
                  pl.BlockSpec((1, 1, D_MODEL), lambda b, *_: (b * TOK_BLK // seg_len, 0, 0)),
                  pl.BlockSpec((1, D_MODEL), lambda b, *_: (0, 0)),
                  pl.BlockSpec((1, D_MODEL), lambda b, *_: (0, 0))],
        out_specs=pl.BlockSpec((TOK_BLK, D_MODEL), lambda b, *_: (b, 0)),
        scratch_shapes=[pltpu.VMEM((2, WIN * ROW_TILES, LANES), F32), pltpu.VMEM((CHUNK_ROWS * ROW_TILES, LANES), F32),
                        pltpu.VMEM((TOK_BLK, D_MODEL), F32),
                        pltpu.SemaphoreType.DMA((2,)), pltpu.SemaphoreType.DMA((1,))],
    )
    return pl.pallas_call(
        _combine_kernel,
        grid_spec=grid_spec,
        out_shape=jax.ShapeDtypeStruct((N_TOK, D_MODEL), F32),
        compiler_params=_params(("arbitrary",)),
        name="moe_combine",
    )(boff, ys, tok, x1, g2, lng, lnb)


def _rope_tables():
    t = jnp.arange(DEC_SEQ)
    n_freq = MLA_ROPE // 4
    inv_freq = ROPE_BASE ** (-jnp.arange(n_freq, dtype=F32) / n_freq)
    ang_r = (t // GRID_W).astype(F32)[:, None] * inv_freq[None, :]
    ang_c = (t % GRID_W).astype(F32)[:, None] * inv_freq[None, :]
    cos64 = jnp.concatenate([jnp.cos(ang_r)] * 2 + [jnp.cos(ang_c)] * 2, axis=1)
    sin64 = jnp.concatenate([jnp.sin(ang_r)] * 2 + [jnp.sin(ang_c)] * 2, axis=1)
    one, zero = jnp.ones_like(cos64), jnp.zeros_like(sin64)
    mla = (jnp.concatenate([cos64, one], axis=1), jnp.concatenate([sin64, zero], axis=1))
    swa = (jnp.concatenate([cos64, cos64], axis=1), jnp.concatenate([sin64, sin64], axis=1))
    return mla, swa


def _swa_key_layout(k):
    z = jnp.zeros_like(k)
    return jnp.stack([k, z, z, k], axis=2).reshape(k.shape[0], SWA_KW)


def _swa_value_layout(v):
    return jnp.stack([v, v], axis=2).reshape(v.shape[0], SWA_VW)


def kernel(x_prompt, x_sample, cache_mla_ckv, cache_mla_kpe, cache_swa_k, cache_swa_v, c, c_ctx, mod_w, mod_b, ln_gain, ln_bias, mla_w_in, mla_q_gain, mla_kv_gain, mla_w_q_up, mla_w_kv_up, mla_w_out, gm_w_in, gm_v_gain, gm_w_s, gm_b_s, gm_w_out, swa_w_qkv, swa_sink, swa_w_out, moe_router, moe_w_gate, moe_w_up, moe_w_down):
    d = D_MODEL
    xs = [x_prompt.reshape(N_TOK, d), x_sample.reshape(N_TOK, d)]
    seg_lens = [N_TOK, DEC_SEQ]
    cond8 = jnp.concatenate([c_ctx[None, :], c, jnp.zeros((3, d), F32)], axis=0)
    mod = _modulation_all(cond8, mod_w, mod_b)
    rope_mla, rope_swa = _rope_tables()
    w_gate_all = moe_w_gate.reshape(DEPTH * N_EXPERTS, d, EXPERT_FF)
    w_up_all = moe_w_up.reshape(DEPTH * N_EXPERTS, d, EXPERT_FF)
    w_down_all = moe_w_down.reshape(DEPTH * N_EXPERTS, EXPERT_FF, d)

    def mod_rows(i, q):
        m = mod[i, :, q * d:(q + 1) * d]
        return [m[0:1].reshape(1, 1, d), m[1:1 + DEC_BATCH].reshape(DEC_BATCH, 1, d)]

    ckv_out, kpe_out, k_out, v_out = [], [], [], []
    for i in range(DEPTH):
        kind, j = i % 3, i // 3
        sh1, sc1, g1, sh2, sc2, g2 = (mod_rows(i, q) for q in range(6))
        router_pad = jnp.pad(moe_router[i], ((0, 0), (0, LANES - N_EXPERTS)))
        lng = ln_gain[i].reshape(2, 1, d)
        lnb = ln_bias[i].reshape(2, 1, d)

        mixed = []
        if kind == 0:
            w = {
                "w_in": jnp.pad(mla_w_in[j], ((0, 0), (0, MLA_ROPE))).astype(BF16),
                "q_gain": mla_q_gain[j][None, :], "kv_gain": mla_kv_gain[j][None, :],
                "w_q": jnp.pad(mla_w_q_up[j], ((0, 0), (0, 0), (0, MLA_QK_PAD - MLA_NOPE - MLA_ROPE)))
                .reshape(MLA_Q_LORA, MLA_HEADS * MLA_QK_PAD).astype(BF16),
                "w_kv": mla_w_kv_up[j].reshape(MLA_KV_LORA, MLA_HEADS * (MLA_NOPE + MLA_V)).astype(BF16),
            }
            w_out = mla_w_out[j].astype(BF16)
            q, kv, kpe, ckv, kraw = _mla_proj(xs[0], sc1[0], sh1[0], w, seg_lens[0], None)
            ckv_out.append(ckv.reshape(CTX_BATCH, CTX_SEQ, MLA_KV_LORA))
            kpe_out.append(kraw.reshape(CTX_BATCH, CTX_SEQ, MLA_ROPE))
            mixed.append(_mla_attn_ctx(q, kv, kpe))
            q, kv, kpe, _, _ = _mla_proj(xs[1], sc1[1], sh1[1], w, seg_lens[1], rope_mla)
            kv_ctx = _matmul_bf16(cache_mla_ckv[:, j].reshape(DEC_BATCH * PAST_LEN, MLA_KV_LORA).astype(BF16), w["w_kv"])
            kpe_ctx = jnp.pad(cache_mla_kpe[:, j].reshape(DEC_BATCH * PAST_LEN, MLA_ROPE),
                              ((0, 0), (0, LANES - MLA_ROPE))).astype(BF16)
            mixed.append(_mla_attn_lat(q, kv, kpe, kv_ctx, kpe_ctx))
        elif kind == 1:
            w = {"w_in": gm_w_in[j].astype(BF16), "v_gain": gm_v_gain[j][None, :],
                 "w_s": gm_w_s[j].astype(BF16), "b_s": gm_b_s[j][:, :, None]}
            w_out = gm_w_out[j].astype(BF16)
            for g in range(2):
                mixed.append(_gmlp_in(xs[g], sc1[g], sh1[g], w, seg_lens[g]))
        else:
            wq = swa_w_qkv[j]
            nq, nkv = SWA_HEADS * SWA_HEAD_DIM, SWA_KV_HEADS * SWA_HEAD_DIM
            w_ext = jnp.concatenate(
                [wq[:, :nq],
                 _swa_key_layout(wq[:, nq:nq + nkv].reshape(d, SWA_KV_HEADS, SWA_HEAD_DIM)),
                 _swa_value_layout(wq[:, nq + nkv:].reshape(d, SWA_KV_HEADS, SWA_HEAD_DIM))], axis=1).astype(BF16)
            w_out = swa_w_out[j].astype(BF16)
            sink = swa_sink[j][None, :]
            q, kd, vd, kraw, vraw = _swa_proj(xs[0], sc1[0], sh1[0], w_ext, seg_lens[0], None)
            k_out.append(kraw.reshape(CTX_BATCH, CTX_SEQ, SWA_KV_HEADS, SWA_HEAD_DIM))
            v_out.append(vraw.reshape(CTX_BATCH, CTX_SEQ, SWA_KV_HEADS, SWA_HEAD_DIM))
            mixed.append(_swa_attn_ctx(q, kd, vd, sink))
            q, kd, vd, _, _ = _swa_proj(xs[1], sc1[1], sh1[1], w_ext, seg_lens[1], rope_swa)
            rows = DEC_BATCH * PAST_LEN
            kd_ctx = _swa_key_layout(cache_swa_k[:, j].reshape(rows, SWA_KV_HEADS, SWA_HEAD_DIM)).astype(BF16)
            vd_ctx = _swa_value_layout(cache_swa_v[:, j].reshape(rows, SWA_KV_HEADS, SWA_HEAD_DIM)).astype(BF16)
            mixed.append(_swa_attn_lat(q, kd, vd, kd_ctx, vd_ctx, sink))

        x1, h2, routed = [], [], []
        for g in range(2):
            x1_g, h2_g, lg_g = _post(mixed[g], w_out, xs[g], g1[g], sc2[g], sh2[g], lng[0], lnb[0], router_pad, seg_lens[g])
            x1.append(x1_g)
            h2.append(h2_g)
            routed.append(_route(lg_g))
        idx_all = jnp.concatenate([routed[0][3], routed[1][3]], axis=2).reshape(-1)
        pos_all = jnp.concatenate([routed[0][4], routed[1][4]], axis=2).reshape(-1)
        ys = _moe_ffn(i, idx_all, pos_all, h2[0], h2[1], routed[0][0], routed[1][0], w_gate_all, w_up_all, w_down_all)
        for g in range(2):
            boff = routed[g][2][0]
            xs[g] = _combine(boff, ys[g], routed[g][1], x1[g], g2[g], lng[1], lnb[1], seg_lens[g])

    return (xs[0].reshape(CTX_BATCH, CTX_SEQ, d), xs[1].reshape(DEC_BATCH, DEC_SEQ, d),
            jnp.stack(ckv_out, axis=1), jnp.stack(kpe_out, axis=1),
            jnp.stack(k_out, axis=1), jnp.stack(v_out, axis=1))
```

```python
import functools

import jax
import jax.numpy as jnp
from jax import lax
from jax.experimental import pallas as pl
from jax.experimental.pallas import tpu as pltpu

D_MODEL = 1024
N_TOK = 4096
DEPTH = 4
DEC_BATCH = 4
DEC_SEQ = 1024
PAST_LEN = 256
CTX_BATCH = 16
CTX_SEQ = 256
GRID_W = 64
ALPHA = (2.0 * DEPTH) ** 0.25
MLA_HEADS = 8
MLA_Q_LORA = 384
MLA_KV_LORA = 256
MLA_NOPE = 128
MLA_ROPE = 64
MLA_V = 128
MLA_QK_PAD = 256
GM_FF = 6 * D_MODEL
GM_HALF = GM_FF // 2
GM_GROUPS = 8
GM_GW = GM_HALF // GM_GROUPS
CHUNK = 128
SWA_HEADS = 16
SWA_KV_HEADS = 4
SWA_HEAD_DIM = 64
WINDOW = 128
BAND = 128
N_EXPERTS = 16
EXPERT_FF = 2 * D_MODEL
CAP = 2 * N_TOK // N_EXPERTS
ROPE_BASE = 10000.0
NEG_INF = -1e30
EPS = 1e-6
F32 = jnp.float32
BF16 = jnp.bfloat16
I32 = jnp.int32
U32 = jnp.uint32
HIGHEST = lax.Precision.HIGHEST

LANES = 128
SUBLANES = 8
ROW_TILES = D_MODEL // LANES
VMEM_LIMIT = 56 * 1024 * 1024
TM = 256
FF_TILE = 1024
TOK_BLK = 256
MOD_TN = 1536

F_SLOT_HI, F_SLOT_LO, F_G0, F_G1, F_G2 = 0, 16, 32, 48, 64
F_T_HI, F_T_LO = 80, 81


def _params(sem, vmem=VMEM_LIMIT):
    return pltpu.CompilerParams(dimension_semantics=sem, vmem_limit_bytes=vmem)


def _dot(a, b):
    return jnp.dot(a, b, preferred_element_type=F32)


def _dot_nt(a, b):
    return lax.dot_general(a, b, (((1,), (1,)), ((), ())), preferred_element_type=F32)


def _sigmoid(x):
    return 1.0 / (1.0 + jnp.exp(-x))


def _gelu_tanh(x):
    return 0.5 * x * (1.0 + jnp.tanh(0.7978845608028654 * (x + 0.044715 * (x * x * x))))


def _layer_norm(t, gain, bias):
    mu = jnp.mean(t, axis=-1, keepdims=True)
    d = t - mu
    var = jnp.mean(d * d, axis=-1, keepdims=True)
    return d * lax.rsqrt(var + EPS) * gain + bias


def _rms_norm(t, gain):
    return t * lax.rsqrt(jnp.mean(t * t, axis=-1, keepdims=True) + EPS) * gain


def _rope128(x, cos, sin):
    lane = lax.broadcasted_iota(I32, x.shape, 1)
    first = (lane % 32) < 16
    swapped = jnp.where(first, -pltpu.roll(x, 112, 1), pltpu.roll(x, 16, 1))
    return x * cos + swapped * sin


def _mod_kernel(cond_ref, w_ref, b_ref, o_ref):
    c = cond_ref[...]
    s = c * _sigmoid(c)
    o_ref[0] = jnp.dot(s, w_ref[0], precision=HIGHEST, preferred_element_type=F32) + b_ref[0]


def _modulation_all(cond8, mod_w, mod_b):
    return pl.pallas_call(
        _mod_kernel,
        grid=(DEPTH, 6 * D_MODEL // MOD_TN),
        in_specs=[pl.BlockSpec((8, D_MODEL), lambda i, n: (0, 0)),
                  pl.BlockSpec((1, D_MODEL, MOD_TN), lambda i, n: (i, 0, n)),
                  pl.BlockSpec((1, 1, MOD_TN), lambda i, n: (i, 0, n))],
        out_specs=pl.BlockSpec((1, 8, MOD_TN), lambda i, n: (i, 0, n)),
        out_shape=jax.ShapeDtypeStruct((DEPTH, 8, 6 * D_MODEL), F32),
        compiler_params=_params(("arbitrary", "arbitrary")),
        name="modulation",
    )(cond8, mod_w, mod_b.reshape(DEPTH, 1, 6 * D_MODEL))


def _seg_spec(seg_len):
    return pl.BlockSpec((1, 1, D_MODEL), lambda r: (r * TM // seg_len, 0, 0))


def _row_spec(width):
    return pl.BlockSpec((TM, width), lambda r: (r, 0))


def _full_spec(shape):
    nd = len(shape)
    return pl.BlockSpec(shape, lambda r: (0,) * nd)


def _mla_proj_kernel(rope, x_ref, sc_ref, sh_ref, win_ref, qg_ref, kvg_ref, wq_ref, wkv_ref, *rest):
    if rope:
        cos_ref, sin_ref, q_ref, kv_ref, kpe_ref, ckv_ref, kraw_ref = rest
    else:
        q_ref, kv_ref, kpe_ref, ckv_ref, kraw_ref = rest
    h = (x_ref[...] * (1.0 + sc_ref[0]) + sh_ref[0]).astype(BF16)
    z = _dot(h, win_ref[...])
    cq = _rms_norm(z[:, :MLA_Q_LORA], qg_ref[...])
    ckv = _rms_norm(z[:, MLA_Q_LORA:MLA_Q_LORA + MLA_KV_LORA], kvg_ref[...])
    kp = z[:, MLA_Q_LORA + MLA_KV_LORA:]
    ckv_ref[...] = ckv
    kraw_ref[...] = kp[:, :MLA_ROPE]
    q = _dot(cq.astype(BF16), wq_ref[...])
    kv_ref[...] = _dot(ckv.astype(BF16), wkv_ref[...]).astype(BF16)
    if rope:
        cos, sin = cos_ref[...], sin_ref[...]
        for hd in range(MLA_HEADS):
            a = hd * MLA_QK_PAD
            q_ref[:, a:a + LANES] = q[:, a:a + LANES].astype(BF16)
            q_ref[:, a + LANES:a + 2 * LANES] = _rope128(q[:, a + LANES:a + 2 * LANES], cos, sin).astype(BF16)
        kp = _rope128(kp, cos, sin)
    else:
        q_ref[...] = q.astype(BF16)
    kpe_ref[...] = kp.astype(BF16)


def _mla_proj(x, sc, sh, w, seg_len, rope_tabs):
    rope = rope_tabs is not None
    in_specs = [_row_spec(D_MODEL), _seg_spec(seg_len), _seg_spec(seg_len),
                _full_spec(w["w_in"].shape), _full_spec((1, MLA_Q_LORA)), _full_spec((1, MLA_KV_LORA)),
                _full_spec(w["w_q"].shape), _full_spec(w["w_kv"].shape)]
    args = [x, sc, sh, w["w_in"], w["q_gain"], w["kv_gain"], w["w_q"], w["w_kv"]]
    if rope:
        nblk = DEC_SEQ // TM
        tab_spec = pl.BlockSpec((TM, LANES), lambda r: (r % nblk, 0))
        in_specs += [tab_spec, tab_spec]
        args += list(rope_tabs)
    wq = MLA_HEADS * MLA_QK_PAD
    return pl.pallas_call(
        functools.partial(_mla_proj_kernel, rope),
        grid=(N_TOK // TM,),
        in_specs=in_specs,
        out_specs=[_row_spec(wq), _row_spec(wq), _row_spec(LANES), _row_spec(MLA_KV_LORA), _row_spec(MLA_ROPE)],
        out_shape=[jax.ShapeDtypeStruct((N_TOK, wq), BF16), jax.ShapeDtypeStruct((N_TOK, wq), BF16),
                   jax.ShapeDtypeStruct((N_TOK, LANES), BF16), jax.ShapeDtypeStruct((N_TOK, MLA_KV_LORA), F32),
                   jax.ShapeDtypeStruct((N_TOK, MLA_ROPE), F32)],
        compiler_params=_params(("arbitrary",)),
        name="mla_proj",
    )(*args)


def _mm_kernel(a_ref, w_ref, o_ref):
    o_ref[...] = _dot(a_ref[...], w_ref[...]).astype(o_ref.dtype)


def _matmul_bf16(a, w):
    m, k = a.shape
    n = w.shape[1]
    return pl.pallas_call(
        _mm_kernel,
        grid=(m // TM,),
        in_specs=[pl.BlockSpec((TM, k), lambda r: (r, 0)), _full_spec(w.shape)],
        out_specs=pl.BlockSpec((TM, n), lambda r: (r, 0)),
        out_shape=jax.ShapeDtypeStruct((m, n), BF16),
        compiler_params=_params(("arbitrary",)),
        name="matmul",
    )(a, w)


MLA_SCALE = (MLA_NOPE + MLA_ROPE) ** -0.5


def _mla_attn_ctx_kernel(q_ref, kv_ref, kpe_ref, o_ref):
    kpe = kpe_ref[...]
    for hd in range(MLA_HEADS):
        a = hd * MLA_QK_PAD
        qh = q_ref[:, a:a + MLA_QK_PAD]
        kh = jnp.concatenate([kv_ref[:, a:a + LANES], kpe], axis=1)
        vh = kv_ref[:, a + LANES:a + 2 * LANES]
        s = _dot_nt(qh, kh) * MLA_SCALE
        p = jnp.exp(s - jnp.max(s, axis=-1, keepdims=True))
        l = jnp.sum(p, axis=-1, keepdims=True)
        o_ref[:, hd * MLA_V:(hd + 1) * MLA_V] = (_dot(p.astype(BF16), vh) / l).astype(BF16)


def _mla_attn_ctx(q, kv, kpe):
    wq = MLA_HEADS * MLA_QK_PAD
    blk = lambda w: pl.BlockSpec((CTX_SEQ, w), lambda b: (b, 0))
    return pl.pallas_call(
        _mla_attn_ctx_kernel,
        grid=(CTX_BATCH,),
        in_specs=[blk(wq), blk(wq), blk(LANES)],
        out_specs=blk(MLA_HEADS * MLA_V),
        out_shape=jax.ShapeDtypeStruct((N_TOK, MLA_HEADS * MLA_V), BF16),
        compiler_params=_params(("arbitrary",)),
        name="mla_attn_ctx",
    )(q, kv, kpe)


def _mla_attn_lat_kernel(q_ref, kvl_ref, kpel_ref, kvc_ref, kpec_ref, o_ref):
    kpel = kpel_ref[...]
    kpec = kpec_ref[...]
    for hd in range(MLA_HEADS):
        a = hd * MLA_QK_PAD
        qh = q_ref[:, a:a + MLA_QK_PAD]
        kc = jnp.concatenate([kvc_ref[:, a:a + LANES], kpec], axis=1)
        kl = jnp.concatenate([kvl_ref[:, a:a + LANES], kpel], axis=1)
        sc = _dot_nt(qh, kc) * MLA_SCALE
        sl = _dot_nt(qh, kl) * MLA_SCALE
        m = jnp.maximum(jnp.max(sc, axis=-1, keepdims=True), jnp.max(sl, axis=-1, keepdims=True))
        pc = jnp.exp(sc - m)
        plat = jnp.exp(sl - m)
        l = jnp.sum(pc, axis=-1, keepdims=True) + jnp.sum(plat, axis=-1, keepdims=True)
        o = _dot(pc.astype(BF16), kvc_ref[:, a + LANES:a + 2 * LANES]) \
            + _dot(plat.astype(BF16), kvl_ref[:, a + LANES:a + 2 * LANES])
        o_ref[:, hd * MLA_V:(hd + 1) * MLA_V] = (o / l).astype(BF16)


def _mla_attn_lat(q, kv, kpe, kv_ctx, kpe_ctx):
    wq = MLA_HEADS * MLA_QK_PAD
    nq = DEC_SEQ // TM
    return pl.pallas_call(
        _mla_attn_lat_kernel,
        grid=(DEC_BATCH, nq),
        in_specs=[pl.BlockSpec((TM, wq), lambda b, i: (b * nq + i, 0)),
                  pl.BlockSpec((DEC_SEQ, wq), lambda b, i: (b, 0)),
                  pl.BlockSpec((DEC_SEQ, LANES), lambda b, i: (b, 0)),
                  pl.BlockSpec((PAST_LEN, wq), lambda b, i: (b, 0)),
                  pl.BlockSpec((PAST_LEN, LANES), lambda b, i: (b, 0))],
        out_specs=pl.BlockSpec((TM, MLA_HEADS * MLA_V), lambda b, i: (b * nq + i, 0)),
        out_shape=jax.ShapeDtypeStruct((N_TOK, MLA_HEADS * MLA_V), BF16),
        compiler_params=_params(("arbitrary", "arbitrary")),
        name="mla_attn_lat",
    )(q, kv, kpe, kv_ctx, kpe_ctx)


def _gmlp_kernel(x_ref, sc_ref, sh_ref, win_ref, vg_ref, ws_ref, bs_ref, p_ref):
    h = (x_ref[...] * (1.0 + sc_ref[0]) + sh_ref[0]).astype(BF16)
    zv = _gelu_tanh(_dot(h, win_ref[:, GM_HALF:]))
    mu = jnp.mean(zv, axis=-1, keepdims=True)
    d = zv - mu
    var = jnp.mean(d * d, axis=-1, keepdims=True)
    vn = (d * lax.rsqrt(var + EPS) * vg_ref[...]).astype(BF16)
    for g in range(GM_GROUPS):
        c0 = g * GM_GW
        u = _gelu_tanh(_dot(h, win_ref[:, c0:c0 + GM_GW]))
        mixed = jnp.concatenate(
            [_dot(ws_ref[g], vn[k * CHUNK:(k + 1) * CHUNK, c0:c0 + GM_GW]) for k in range(TM // CHUNK)], axis=0)
        bias = jnp.concatenate([bs_ref[g]] * (TM // CHUNK), axis=0)
        p_ref[:, c0:c0 + GM_GW] = (u * (mixed + bias)).astype(BF16)


def _gmlp_in(x, sc, sh, w, seg_len):
    return pl.pallas_call(
        _gmlp_kernel,
        grid=(N_TOK // TM,),
        in_specs=[_row_spec(D_MODEL), _seg_spec(seg_len), _seg_spec(seg_len),
                  _full_spec((D_MODEL, GM_FF)), _full_spec((1, GM_HALF)),
                  _full_spec((GM_GROUPS, CHUNK, CHUNK)), _full_spec((GM_GROUPS, CHUNK, 1))],
        out_specs=_row_spec(GM_HALF),
        out_shape=jax.ShapeDtypeStruct((N_TOK, GM_HALF), BF16),
        compiler_params=_params(("arbitrary",)),
        name="gmlp_in",
    )(x, sc, sh, w["w_in"], w["v_gain"], w["w_s"], w["b_s"])


SWA_SCALE = SWA_HEAD_DIM ** -0.5
SWA_QW = SWA_HEADS * SWA_HEAD_DIM
SWA_KW = SWA_KV_HEADS * 2 * LANES
SWA_VW = SWA_KV_HEADS * LANES
SWA_GROUP = SWA_KV_HEADS


def _swa_proj_kernel(rope, x_ref, sc_ref, sh_ref, w_ref, *rest):
    if rope:
        cos_ref, sin_ref, q_ref, kd_ref, vd_ref, kraw_ref, vraw_ref = rest
    else:
        q_ref, kd_ref, vd_ref, kraw_ref, vraw_ref = rest
    h = (x_ref[...] * (1.0 + sc_ref[0]) + sh_ref[0]).astype(BF16)
    z = _dot(h, w_ref[...])
    lane = lax.broadcasted_iota(I32, (TM, LANES), 1)
    low = lane < SWA_HEAD_DIM
    for m in range(SWA_KV_HEADS // 2):
        k0 = z[:, SWA_QW + (2 * m) * 2 * LANES:SWA_QW + (2 * m) * 2 * LANES + LANES]
        k1 = z[:, SWA_QW + (2 * m + 1) * 2 * LANES + LANES:SWA_QW + (2 * m + 2) * 2 * LANES]
        kraw_ref[:, m * LANES:(m + 1) * LANES] = jnp.where(low, k0, k1)
        v0 = z[:, SWA_QW + SWA_KW + (2 * m) * LANES:SWA_QW + SWA_KW + (2 * m + 1) * LANES]
        v1 = z[:, SWA_QW + SWA_KW + (2 * m + 1) * LANES:SWA_QW + SWA_KW + (2 * m + 2) * LANES]
        vraw_ref[:, m * LANES:(m + 1) * LANES] = jnp.where(low, v0, v1)
    vd_ref[...] = z[:, SWA_QW + SWA_KW:].astype(BF16)
    if rope:
        cos, sin = cos_ref[...], sin_ref[...]
        for j in range(SWA_QW // LANES):
            q_ref[:, j * LANES:(j + 1) * LANES] = _rope128(z[:, j * LANES:(j + 1) * LANES], cos, sin).astype(BF16)
        for j in range(SWA_KW // LANES):
            a = SWA_QW + j * LANES
            kd_ref[:, j * LANES:(j + 1) * LANES] = _rope128(z[:, a:a + LANES], cos, sin).astype(BF16)
    else:
        q_ref[...] = z[:, :SWA_QW].astype(BF16)
        kd_ref[...] = z[:, SWA_QW:SWA_QW + SWA_KW].astype(BF16)


def _swa_proj(x, sc, sh, w_ext, seg_len, rope_tabs):
    rope = rope_tabs is not None
    in_specs = [_row_spec(D_MODEL), _seg_spec(seg_len), _seg_spec(seg_len), _full_spec(w_ext.shape)]
    args = [x, sc, sh, w_ext]
    if rope:
        nblk = DEC_SEQ // TM
        tab_spec = pl.BlockSpec((TM, LANES), lambda r: (r % nblk, 0))
        in_specs += [tab_spec, tab_spec]
        args += list(rope_tabs)
    kvw = SWA_KV_HEADS * SWA_HEAD_DIM
    return pl.pallas_call(
        functools.partial(_swa_proj_kernel, rope),
        grid=(N_TOK // TM,),
        in_specs=in_specs,
        out_specs=[_row_spec(SWA_QW), _row_spec(SWA_KW), _row_spec(SWA_VW), _row_spec(kvw), _row_spec(kvw)],
        out_shape=[jax.ShapeDtypeStruct((N_TOK, SWA_QW), BF16), jax.ShapeDtypeStruct((N_TOK, SWA_KW), BF16),
                   jax.ShapeDtypeStruct((N_TOK, SWA_VW), BF16), jax.ShapeDtypeStruct((N_TOK, kvw), F32),
                   jax.ShapeDtypeStruct((N_TOK, kvw), F32)],
        compiler_params=_params(("arbitrary",)),
        name="swa_proj",
    )(*args)


def _swa_heads(q_ref, sink_ref, o_ref, score_fn, value_fn):
    rows = q_ref.shape[0]
    lane = lax.broadcasted_iota(I32, (rows, LANES), 1)
    low = lane < SWA_HEAD_DIM
    for hk0 in range(0, SWA_KV_HEADS, SWA_GROUP):
        stacked, sinks = None, []
        for hk in range(hk0, hk0 + SWA_GROUP):
            c0 = hk * 2 * LANES
            q2 = jnp.concatenate([q_ref[:, c0:c0 + LANES], q_ref[:, c0 + LANES:c0 + 2 * LANES]], axis=0)
            for par in range(2):
                for pr in range(2):
                    hd = hk * 4 + pr * 2 + par
                    sinks.append(jnp.broadcast_to(sink_ref[:, hd:hd + 1], (rows, 1)))
                blocks = score_fn(hk, par, q2)
                if stacked is None:
                    stacked = [[] for _ in blocks]
                for kind, s in zip(stacked, blocks):
                    kind.append(s)
        scores = [jnp.concatenate(kind, axis=0) for kind in stacked]
        sk = jnp.concatenate(sinks, axis=0)
        m = sk
        for s in scores:
            m = jnp.maximum(m, jnp.max(s, axis=-1, keepdims=True))
        probs = [jnp.exp(s - m) for s in scores]
        l = jnp.exp(sk - m)
        for p in probs:
            l = l + jnp.sum(p, axis=-1, keepdims=True)
        inv = 1.0 / l
        n = 0
        for hk in range(hk0, hk0 + SWA_GROUP):
            res = []
            for par in range(2):
                blk = slice(n * 2 * rows, (n + 1) * 2 * rows)
                res.append(value_fn(hk, [p[blk] for p in probs]) * inv[blk])
                n += 1
            for pr in range(2):
                c0 = hk * 2 * LANES + pr * LANES
                part = slice(pr * rows, (pr + 1) * rows)
                o_ref[:, c0:c0 + LANES] = jnp.where(low, res[0][part], res[1][part]).astype(BF16)


def _swa_attn_ctx_kernel(q_ref, kd_ref, vd_ref, sink_ref, o_ref):
    def score_fn(hk, par, qp):
        a = hk * 2 * LANES + par * LANES
        return [_dot_nt(qp, kd_ref[:, a:a + LANES]) * SWA_SCALE]

    def value_fn(hk, probs):
        return _dot(probs[0].astype(BF16), vd_ref[:, hk * LANES:(hk + 1) * LANES])

    _swa_heads(q_ref, sink_ref, o_ref, score_fn, value_fn)


def _swa_attn_ctx(q, kd, vd, sink):
    blk = lambda w: pl.BlockSpec((CTX_SEQ, w), lambda b: (b, 0))
    return pl.pallas_call(
        _swa_attn_ctx_kernel,
        grid=(CTX_BATCH,),
        in_specs=[blk(SWA_QW), blk(SWA_KW), blk(SWA_VW), pl.BlockSpec((1, SWA_HEADS), lambda b: (0, 0))],
        out_specs=blk(SWA_QW),
        out_shape=jax.ShapeDtypeStruct((N_TOK, SWA_QW), BF16),
        compiler_params=_params(("arbitrary",)),
        name="swa_attn_ctx",
    )(q, kd, vd, sink)


SWA_WIN = 3 * BAND


def _swa_attn_lat_kernel(q_ref, kdl_ref, vdl_ref, kdc_ref, vdc_ref, sink_ref, o_ref):
    nb = pl.program_id(1)
    start = pl.multiple_of(jnp.clip((nb - 1) * BAND, 0, DEC_SEQ - SWA_WIN), BAND)
    qpos = nb * BAND + (lax.broadcasted_iota(I32, (2 * BAND, SWA_WIN), 0) & (BAND - 1))
    kpos = start + lax.broadcasted_iota(I32, (2 * BAND, SWA_WIN), 1)
    valid = jnp.abs(qpos - kpos) <= WINDOW

    def score_fn(hk, par, qp):
        a = hk * 2 * LANES + par * LANES
        kl = kdl_ref[pl.ds(start, SWA_WIN), a:a + LANES]
        sl = jnp.where(valid, _dot_nt(qp, kl) * SWA_SCALE, NEG_INF)
        sc = _dot_nt(qp, kdc_ref[:, a:a + LANES]) * SWA_SCALE
        return [sl, sc]

    def value_fn(hk, probs):
        vl = vdl_ref[pl.ds(start, SWA_WIN), hk * LANES:(hk + 1) * LANES]
        return _dot(probs[0].astype(BF16), vl) + _dot(probs[1].astype(BF16), vdc_ref[:, hk * LANES:(hk + 1) * LANES])

    _swa_heads(q_ref, sink_ref, o_ref, score_fn, value_fn)


def _swa_attn_lat(q, kd, vd, kd_ctx, vd_ctx, sink):
    nbands = DEC_SEQ // BAND
    return pl.pallas_call(
        _swa_attn_lat_kernel,
        grid=(DEC_BATCH, nbands),
        in_specs=[pl.BlockSpec((BAND, SWA_QW), lambda b, i: (b * nbands + i, 0)),
                  pl.BlockSpec((DEC_SEQ, SWA_KW), lambda b, i: (b, 0)),
                  pl.BlockSpec((DEC_SEQ, SWA_VW), lambda b, i: (b, 0)),
                  pl.BlockSpec((PAST_LEN, SWA_KW), lambda b, i: (b, 0)),
                  pl.BlockSpec((PAST_LEN, SWA_VW), lambda b, i: (b, 0)),
                  pl.BlockSpec((1, SWA_HEADS), lambda b, i: (0, 0))],
        out_specs=pl.BlockSpec((BAND, SWA_QW), lambda b, i: (b * nbands + i, 0)),
        out_shape=jax.ShapeDtypeStruct((N_TOK, SWA_QW), BF16),
        compiler_params=_params(("arbitrary", "arbitrary")),
        name="swa_attn_lat",
    )(q, kd, vd, kd_ctx, vd_ctx, sink)


def _post_kernel(a_ref, w_ref, x_ref, g1_ref, sc2_ref, sh2_ref, lng_ref, lnb_ref, rt_ref, x1_ref, h2_ref, lg_ref):
    y = _dot(a_ref[...], w_ref[...])
    x1 = _layer_norm(ALPHA * x_ref[...] + g1_ref[0] * y, lng_ref[...], lnb_ref[...])
    x1_ref[...] = x1
    h2 = x1 * (1.0 + sc2_ref[0]) + sh2_ref[0]
    for j in range(ROW_TILES):
        h2_ref[pl.ds(j, TM, stride=ROW_TILES), :] = h2[:, j * LANES:(j + 1) * LANES]
    rt = rt_ref[...]
    h_hi = h2.astype(BF16)
    h_lo = (h2 - h_hi.astype(F32)).astype(BF16)
    r_hi = rt.astype(BF16)
    r_lo = (rt - r_hi.astype(F32)).astype(BF16)
    lg_ref[...] = _dot(h_hi, r_hi) + (_dot(h_lo, r_hi) + _dot(h_hi, r_lo))


def _post(a, w_out, x, g1, sc2, sh2, lng, lnb, router_pad, seg_len):
    k = a.shape[1]
    return pl.pallas_call(
        _post_kernel,
        grid=(N_TOK // TM,),
        in_specs=[_row_spec(k), _full_spec((k, D_MODEL)), _row_spec(D_MODEL),
                  _seg_spec(seg_len), _seg_spec(seg_len), _seg_spec(seg_len),
                  _full_spec((1, D_MODEL)), _full_spec((1, D_MODEL)), _full_spec((D_MODEL, LANES))],
        out_specs=[_row_spec(D_MODEL), pl.BlockSpec((TM * ROW_TILES, LANES), lambda r: (r, 0)), _row_spec(LANES)],
        out_shape=[jax.ShapeDtypeStruct((N_TOK, D_MODEL), F32), jax.ShapeDtypeStruct((N_TOK * ROW_TILES, LANES), F32),
                   jax.ShapeDtypeStruct((N_TOK, LANES), F32)],
        compiler_params=_params(("arbitrary",)),
        name="mixer_out",
    )(a, w_out, x, g1, sc2, sh2, lng, lnb, router_pad)


CUM_BLK = 256
REFINE_STEPS = 16
F32_MIN_NORMAL = 1.1754943508222875e-38


def _excl_cumsum_lanes(a, upper):
    outs = []
    carry = jnp.zeros((a.shape[0], 1), F32)
    for b in range(N_TOK // CUM_BLK):
        blk = a[:, b * CUM_BLK:(b + 1) * CUM_BLK]
        outs.append(_dot(blk.astype(BF16), upper) + carry)
        carry = carry + jnp.sum(blk, axis=1, keepdims=True)
    return jnp.concatenate(outs, axis=1)


def _route_kernel(lg_ref, r_ref, tok_ref, roff_ref, idx_ref, rank_scr, w_scr):
    e_rows = N_EXPERTS
    lt = lg_ref[...].T[:e_rows]
    ex = jnp.exp(lt - jnp.max(lt, axis=0, keepdims=True))
    aff = ex / jnp.sum(ex, axis=0, keepdims=True)
    def count_ge(thr):
        return jnp.sum(jnp.where(aff >= thr, 1.0, 0.0), axis=1, keepdims=True)

    cur = jnp.zeros((e_rows, 1), I32)
    for b in range(30, -1, -1):
        cand = cur | jnp.int32(1 << b)
        cur = jnp.where(count_ge(lax.bitcast_convert_type(cand, F32)) >= float(CAP), cand, cur)
    lo = lax.bitcast_convert_type(cur, F32)
    hi = jnp.maximum(lax.bitcast_convert_type(cur + 1, F32), F32_MIN_NORMAL)
    for _ in range(REFINE_STEPS):
        w = hi - lo
        t1, t2, t3 = lo + 0.25 * w, lo + 0.5 * w, lo + 0.75 * w
        ok1, ok2, ok3 = (count_ge(t) >= float(CAP) for t in (t1, t2, t3))
        lo, hi = (jnp.where(ok3, t3, jnp.where(ok2, t2, jnp.where(ok1, t1, lo))),
                  jnp.where(ok1, jnp.where(ok2, jnp.where(ok3, hi, t3), t2), t1))
    gt = aff >= hi
    eq = (aff >= lo) & (aff < hi)
    need = float(CAP) - jnp.sum(jnp.where(gt, 1.0, 0.0), axis=1, keepdims=True)

    ri = lax.broadcasted_iota(I32, (CUM_BLK, CUM_BLK), 0)
    ci = lax.broadcasted_iota(I32, (CUM_BLK, CUM_BLK), 1)
    upper = jnp.where(ri < ci, 1.0, 0.0).astype(BF16)

    tie_rank = _excl_cumsum_lanes(jnp.where(eq, 1.0, 0.0), upper)
    sel = gt | (eq & (tie_rank < need))
    msk = jnp.where(sel, 1.0, 0.0)

    rank = _excl_cumsum_lanes(msk, upper)
    rank_scr[...] = jnp.where(sel, rank, -1.0)

    inv = 1.0 / LANES
    slot_no = jnp.where(sel, rank, -float(LANES))
    slot_hi = jnp.floor(slot_no * inv)
    g0 = aff.astype(BF16).astype(F32)
    r1 = aff - g0
    g1 = r1.astype(BF16).astype(F32)
    g2 = (r1 - g1).astype(BF16).astype(F32)
    tt = lax.broadcasted_iota(I32, (1, N_TOK), 1).astype(F32)
    t_hi = jnp.floor(tt * inv)
    misc = jnp.concatenate([t_hi, tt - t_hi * LANES, jnp.zeros((6, N_TOK), F32)], axis=0)
    feat = jnp.concatenate([slot_hi, slot_no - slot_hi * LANES, g0, g1, g2, misc,
                            jnp.zeros((LANES - 5 * e_rows - 8, N_TOK), F32)], axis=0)
    table = feat.T
    tok_ref[...] = table
    w_scr[...] = table.astype(BF16)

    nblk = N_TOK // TOK_BLK
    bt = lax.broadcasted_iota(I32, (N_TOK, LANES), 0) // TOK_BLK
    bb = lax.broadcasted_iota(I32, (N_TOK, LANES), 1)
    before = jnp.where((bt < bb) & (bb <= nblk), 1.0, 0.0).astype(BF16)
    roff_ref[...] = _dot(msk.astype(BF16), before).astype(I32)

    sub = lax.broadcasted_iota(I32, (8, LANES), 0)
    ln8 = lax.broadcasted_iota(I32, (8, LANES), 1)
    rb_rows = 128

    def per_expert(e, carry):
        rk = rank_scr[pl.ds(e, 1), :]
        want = jnp.where(sub < 5, sub * e_rows + e, sub + (F_T_HI - 5))
        pick = jnp.where(ln8 == want, 1.0, 0.0).astype(BF16)
        for rb in range(CAP // rb_rows):
            slot = (rb * rb_rows + lax.broadcasted_iota(I32, (rb_rows, 1), 0)).astype(F32)
            onehot = jnp.where(rk == slot, 1.0, 0.0).astype(BF16)
            rows = _dot(onehot, w_scr[...])
            r_ref[pl.ds(e, 1), rb * rb_rows:(rb + 1) * rb_rows, :] = rows[None]
            got = _dot_nt(pick, rows.astype(BF16))
            idx_ref[pl.ds(e, 1), :, rb * rb_rows:(rb + 1) * rb_rows] = (got[5:6] * LANES + got[6:7]).astype(I32)[None]
        return carry

    lax.fori_loop(0, e_rows, per_expert, 0)


def _route(logits):
    return pl.pallas_call(
        _route_kernel,
        grid=(1,),
        in_specs=[pl.BlockSpec((N_TOK, LANES), lambda i: (0, 0))],
        out_specs=[pl.BlockSpec((N_EXPERTS, CAP, LANES), lambda i: (0, 0, 0)),
                   pl.BlockSpec((N_TOK, LANES), lambda i: (0, 0)),
                   pl.BlockSpec((N_EXPERTS, LANES), lambda i: (0, 0)),
                   pl.BlockSpec((N_EXPERTS, 1, CAP), lambda i: (0, 0, 0))],
        out_shape=[jax.ShapeDtypeStruct((N_EXPERTS, CAP, LANES), F32),
                   jax.ShapeDtypeStruct((N_TOK, LANES), F32),
                   jax.ShapeDtypeStruct((N_EXPERTS, LANES), I32),
                   jax.ShapeDtypeStruct((N_EXPERTS, 1, CAP), I32)],
        scratch_shapes=[pltpu.VMEM((N_EXPERTS, N_TOK), F32), pltpu.VMEM((N_TOK, LANES), BF16)],
        compiler_params=_params(("arbitrary",)),
        name="moe_route",
    )(logits)


ROWS_E = 2 * CAP
N_FT = EXPERT_FF // FF_TILE
GATHER_UNROLL = 8
PIECE = 256
N_PIECE = FF_TILE // PIECE


def _tile(row):
    return pl.ds(pl.multiple_of(row * ROW_TILES, ROW_TILES), ROW_TILES)


def _ffn_kernel(idx_s, hc_hbm, hl_hbm, rc_ref, rl_ref, wg_ref, wu_ref, wd_ref, y_ref, xe, xb, acc, gsem):
    e = pl.program_id(0)
    f = pl.program_id(1)

    def gather_rows(ee):
        base = ee * ROWS_E

        def body(r, carry):
            tc = idx_s[base + r]
            pltpu.make_async_copy(hc_hbm.at[_tile(tc)], xe.at[_tile(r)], gsem.at[0]).start(priority=0)
            tl = idx_s[base + CAP + r]
            pltpu.make_async_copy(hl_hbm.at[_tile(tl)], xe.at[_tile(CAP + r)], gsem.at[0]).start(priority=1)
            return carry

        lax.fori_loop(0, CAP, body, 0, unroll=GATHER_UNROLL)

    @pl.when(f == 0)
    def _():
        @pl.when(e == 0)
        def _():
            gather_rows(0)

        pltpu.make_async_copy(xe, xe, gsem.at[0]).wait()
        for j in range(ROW_TILES):
            xb[:, j * LANES:(j + 1) * LANES] = xe[pl.ds(j, ROWS_E, stride=ROW_TILES), :].astype(BF16)

        @pl.when(e + 1 < N_EXPERTS)
        def _():
            gather_rows(e + 1)

        acc[...] = jnp.zeros_like(acc)

    x = xb[...]
    for p in range(N_PIECE):
        cols = slice(p * PIECE, (p + 1) * PIECE)
        g = _dot(x, wg_ref[0, :, cols].astype(BF16))
        u = _dot(x, wu_ref[0, :, cols].astype(BF16))
        hid = (g * _sigmoid(g) * u).astype(BF16)
        acc[...] += _dot(hid, wd_ref[0, cols, :].astype(BF16))

    @pl.when(f == N_FT - 1)
    def _():
        lane = lax.broadcasted_iota(I32, (CAP, LANES), 1)
        mine = (lane == F_G0 + e) | (lane == F_G1 + e) | (lane == F_G2 + e)
        gate = jnp.concatenate(
            [jnp.sum(jnp.where(mine, rc_ref[0], 0.0), axis=1, keepdims=True),
             jnp.sum(jnp.where(mine, rl_ref[0], 0.0), axis=1, keepdims=True)], axis=0)
        y_ref[0] = acc[...] * gate


def _moe_ffn(layer, idx_all, h_ctx, h_lat, r_ctx, r_lat, w_gate, w_up, w_down):
    any_spec = pl.BlockSpec(memory_space=pl.ANY)
    e0 = layer * N_EXPERTS
    grid_spec = pltpu.PrefetchScalarGridSpec(
        num_scalar_prefetch=1,
        grid=(N_EXPERTS, N_FT),
        in_specs=[any_spec, any_spec,
                  pl.BlockSpec((1, CAP, LANES), lambda e, f, *_: (e, 0, 0)),
                  pl.BlockSpec((1, CAP, LANES), lambda e, f, *_: (e, 0, 0)),
                  pl.BlockSpec((1, D_MODEL, FF_TILE), lambda e, f, *_: (e0 + e, 0, f)),
                  pl.BlockSpec((1, D_MODEL, FF_TILE), lambda e, f, *_: (e0 + e, 0, f)),
                  pl.BlockSpec((1, FF_TILE, D_MODEL), lambda e, f, *_: (e0 + e, f, 0))],
        out_specs=pl.BlockSpec((1, ROWS_E, D_MODEL), lambda e, f, *_: (e, 0, 0)),
        scratch_shapes=[pltpu.VMEM((ROWS_E * ROW_TILES, LANES), F32), pltpu.VMEM((ROWS_E, D_MODEL), BF16),
                        pltpu.VMEM((ROWS_E, D_MODEL), F32), pltpu.SemaphoreType.DMA((1,))],
    )
    return pl.pallas_call(
        _ffn_kernel,
        grid_spec=grid_spec,
        out_shape=jax.ShapeDtypeStruct((N_EXPERTS, ROWS_E, D_MODEL), F32),
        compiler_params=_params(("arbitrary", "arbitrary")),
        name="moe_ffn",
    )(idx_all, h_ctx, h_lat, r_ctx, r_lat, w_gate, w_up, w_down)


WIN_E = 64
EXPERTS_PER_DOT = 4
N_TBLK = N_TOK // TOK_BLK
ROFF_STRIDE = LANES


def _combine_kernel(group, roff_s, y_hbm, tok_ref, x_ref, g2_ref, lng_ref, lnb_ref, o_ref, ywin, yext, acc, wsem, esem):
    b = pl.program_id(0)

    def win_start(e, bb):
        first = roff_s[e * ROFF_STRIDE + bb]
        return jnp.minimum((first // SUBLANES) * SUBLANES, CAP - WIN_E)

    def rows_at(start):
        return pl.ds(pl.multiple_of(group * CAP + start, SUBLANES), WIN_E)

    def start_windows(bb, slot):
        for e in range(N_EXPERTS):
            pltpu.make_async_copy(y_hbm.at[e, rows_at(win_start(e, bb))], ywin.at[slot, e], wsem.at[slot]).start()

    @pl.when(b == 0)
    def _():
        start_windows(0, 0)

    @pl.when(b + 1 < N_TBLK)
    def _():
        start_windows(b + 1, (b + 1) % 2)

    slot = b % 2
    tok = tok_ref[...]
    lane = lax.broadcasted_iota(I32, (TOK_BLK, WIN_E), 1)
    owner = [tok[:, F_SLOT_HI + e:F_SLOT_HI + e + 1] * LANES + tok[:, F_SLOT_LO + e:F_SLOT_LO + e + 1]
             for e in range(N_EXPERTS)]
    starts = [win_start(e, b) for e in range(N_EXPERTS)]

    pltpu.make_async_copy(ywin.at[slot], ywin.at[slot], wsem.at[slot]).wait()
    wide = lax.broadcasted_iota(I32, (TOK_BLK, EXPERTS_PER_DOT * WIN_E), 1)
    total = None
    for e0 in range(0, N_EXPERTS, EXPERTS_PER_DOT):
        own, slots = owner[e0], (starts[e0] + wide).astype(F32)
        for k in range(1, EXPERTS_PER_DOT):
            later = wide >= k * WIN_E
            own = jnp.where(later, owner[e0 + k], own)
            slots = jnp.where(later, (starts[e0 + k] - k * WIN_E + wide).astype(F32), slots)
        seg = jnp.where(own == slots, 1.0, 0.0).astype(BF16)
        rows = ywin[slot, e0:e0 + EXPERTS_PER_DOT].reshape(EXPERTS_PER_DOT * WIN_E, D_MODEL)
        part = _dot(seg, rows.astype(BF16))
        total = part if total is None else total + part
    acc[...] = total

    for e in range(N_EXPERTS):
        past = jnp.maximum(roff_s[e * ROFF_STRIDE + b + 1] - (starts[e] + WIN_E), 0)

        def over(k, carry, e=e):
            nominal = starts[e] + (k + 1) * WIN_E
            first = jnp.minimum(nominal, CAP - WIN_E)
            cp = pltpu.make_async_copy(y_hbm.at[e, rows_at(first)], yext, esem.at[0])
            cp.start()
            cp.wait()
            slots = first + lane
            seg = jnp.where((owner[e] == slots.astype(F32)) & (slots >= nominal), 1.0, 0.0).astype(BF16)
            acc[...] += _dot(seg, yext[...].astype(BF16))
            return carry

        lax.fori_loop(0, (past + WIN_E - 1) // WIN_E, over, 0)

    o_ref[...] = _layer_norm(ALPHA * x_ref[...] + g2_ref[0] * acc[...], lng_ref[...], lnb_ref[...])


def _combine(group, roff, y, tok, x1, g2, lng, lnb, seg_len):
    grid_spec = pltpu.PrefetchScalarGridSpec(
        num_scalar_prefetch=1,
        grid=(N_TBLK,),
        in_specs=[pl.BlockSpec(memory_space=pl.ANY),
                  pl.BlockSpec((TOK_BLK, LANES), lambda b, *_: (b, 0)),
                  pl.BlockSpec((TOK_BLK, D_MODEL), lambda b, *_: (b, 0)),
                  pl.BlockSpec((1, 1, D_MODEL), lambda b, *_: (b * TOK_BLK // seg_len, 0, 0)),
                  pl.BlockSpec((1, D_MODEL), lambda b, *_: (0, 0)),
                  pl.BlockSpec((1, D_MODEL), lambda b, *_: (0, 0))],
        out_specs=pl.BlockSpec((TOK_BLK, D_MODEL), lambda b, *_: (b, 0)),
        scratch_shapes=[pltpu.VMEM((2, N_EXPERTS, WIN_E, D_MODEL), F32), pltpu.VMEM((WIN_E, D_MODEL), F32),
                        pltpu.VMEM((TOK_BLK, D_MODEL), F32),
                        pltpu.SemaphoreType.DMA((2,)), pltpu.SemaphoreType.DMA((1,))],
    )
    return pl.pallas_call(
        functools.partial(_combine_kernel, group),
        grid_spec=grid_spec,
        out_shape=jax.ShapeDtypeStruct((N_TOK, D_MODEL), F32),
        compiler_params=_params(("arbitrary",)),
        name="moe_combine",
    )(roff, y, tok, x1, g2, lng, lnb)


def _rope_tables():
    t = jnp.arange(DEC_SEQ)
    n_freq = MLA_ROPE // 4
    inv_freq = ROPE_BASE ** (-jnp.arange(n_freq, dtype=F32) / n_freq)
    ang_r = (t // GRID_W).astype(F32)[:, None] * inv_freq[None, :]
    ang_c = (t % GRID_W).astype(F32)[:, None] * inv_freq[None, :]
    cos64 = jnp.concatenate([jnp.cos(ang_r)] * 2 + [jnp.cos(ang_c)] * 2, axis=1)
    sin64 = jnp.concatenate([jnp.sin(ang_r)] * 2 + [jnp.sin(ang_c)] * 2, axis=1)
    one, zero = jnp.ones_like(cos64), jnp.zeros_like(sin64)
    mla = (jnp.concatenate([cos64, one], axis=1), jnp.concatenate([sin64, zero], axis=1))
    swa = (jnp.concatenate([cos64, cos64], axis=1), jnp.concatenate([sin64, sin64], axis=1))
    return mla, swa


def _swa_key_layout(k):
    z = jnp.zeros_like(k)
    return jnp.stack([k, z, z, k], axis=2).reshape(k.shape[0], SWA_KW)


def _swa_value_layout(v):
    return jnp.stack([v, v], axis=2).reshape(v.shape[0], SWA_VW)


def kernel(x_prompt, x_sample, cache_mla_ckv, cache_mla_kpe, cache_swa_k, cache_swa_v, c, c_ctx, mod_w, mod_b, ln_gain, ln_bias, mla_w_in, mla_q_gain, mla_kv_gain, mla_w_q_up, mla_w_kv_up, mla_w_out, gm_w_in, gm_v_gain, gm_w_s, gm_b_s, gm_w_out, swa_w_qkv, swa_sink, swa_w_out, moe_router, moe_w_gate, moe_w_up, moe_w_down):
    d = D_MODEL
    xs = [x_prompt.reshape(N_TOK, d), x_sample.reshape(N_TOK, d)]
    seg_lens = [N_TOK, DEC_SEQ]
    cond8 = jnp.concatenate([c_ctx[None, :], c, jnp.zeros((3, d), F32)], axis=0)
    mod = _modulation_all(cond8, mod_w, mod_b)
    rope_mla, rope_swa = _rope_tables()
    w_gate_all = moe_w_gate.reshape(DEPTH * N_EXPERTS, d, EXPERT_FF)
    w_up_all = moe_w_up.reshape(DEPTH * N_EXPERTS, d, EXPERT_FF)
    w_down_all = moe_w_down.reshape(DEPTH * N_EXPERTS, EXPERT_FF, d)

    def mod_rows(i, q):
        m = mod[i, :, q * d:(q + 1) * d]
        return [m[0:1].reshape(1, 1, d), m[1:1 + DEC_BATCH].reshape(DEC_BATCH, 1, d)]

    ckv_out, kpe_out, k_out, v_out = [], [], [], []
    for i in range(DEPTH):
        kind, j = i % 3, i // 3
        sh1, sc1, g1, sh2, sc2, g2 = (mod_rows(i, q) for q in range(6))
        router_pad = jnp.pad(moe_router[i], ((0, 0), (0, LANES - N_EXPERTS)))
        lng = ln_gain[i].reshape(2, 1, d)
        lnb = ln_bias[i].reshape(2, 1, d)

        mixed = []
        if kind == 0:
            w = {
                "w_in": jnp.pad(mla_w_in[j], ((0, 0), (0, MLA_ROPE))).astype(BF16),
                "q_gain": mla_q_gain[j][None, :], "kv_gain": mla_kv_gain[j][None, :],
                "w_q": jnp.pad(mla_w_q_up[j], ((0, 0), (0, 0), (0, MLA_QK_PAD - MLA_NOPE - MLA_ROPE)))
                .reshape(MLA_Q_LORA, MLA_HEADS * MLA_QK_PAD).astype(BF16),
                "w_kv": mla_w_kv_up[j].reshape(MLA_KV_LORA, MLA_HEADS * (MLA_NOPE + MLA_V)).astype(BF16),
            }
            w_out = mla_w_out[j].astype(BF16)
            q, kv, kpe, ckv, kraw = _mla_proj(xs[0], sc1[0], sh1[0], w, seg_lens[0], None)
            ckv_out.append(ckv.reshape(CTX_BATCH, CTX_SEQ, MLA_KV_LORA))
            kpe_out.append(kraw.reshape(CTX_BATCH, CTX_SEQ, MLA_ROPE))
            mixed.append(_mla_attn_ctx(q, kv, kpe))
            q, kv, kpe, _, _ = _mla_proj(xs[1], sc1[1], sh1[1], w, seg_lens[1], rope_mla)
            kv_ctx = _matmul_bf16(cache_mla_ckv[:, j].reshape(DEC_BATCH * PAST_LEN, MLA_KV_LORA).astype(BF16), w["w_kv"])
            kpe_ctx = jnp.pad(cache_mla_kpe[:, j].reshape(DEC_BATCH * PAST_LEN, MLA_ROPE),
                              ((0, 0), (0, LANES - MLA_ROPE))).astype(BF16)
            mixed.append(_mla_attn_lat(q, kv, kpe, kv_ctx, kpe_ctx))
        elif kind == 1:
            w = {"w_in": gm_w_in[j].astype(BF16), "v_gain": gm_v_gain[j][None, :],
                 "w_s": gm_w_s[j].astype(BF16), "b_s": gm_b_s[j][:, :, None]}
            w_out = gm_w_out[j].astype(BF16)
            for g in range(2):
                mixed.append(_gmlp_in(xs[g], sc1[g], sh1[g], w, seg_lens[g]))
        else:
            wq = swa_w_qkv[j]
            nq, nkv = SWA_HEADS * SWA_HEAD_DIM, SWA_KV_HEADS * SWA_HEAD_DIM
            w_ext = jnp.concatenate(
                [wq[:, :nq],
                 _swa_key_layout(wq[:, nq:nq + nkv].reshape(d, SWA_KV_HEADS, SWA_HEAD_DIM)),
                 _swa_value_layout(wq[:, nq + nkv:].reshape(d, SWA_KV_HEADS, SWA_HEAD_DIM))], axis=1).astype(BF16)
            w_out = swa_w_out[j].astype(BF16)
            sink = swa_sink[j][None, :]
            q, kd, vd, kraw, vraw = _swa_proj(xs[0], sc1[0], sh1[0], w_ext, seg_lens[0], None)
            k_out.append(kraw.reshape(CTX_BATCH, CTX_SEQ, SWA_KV_HEADS, SWA_HEAD_DIM))
            v_out.append(vraw.reshape(CTX_BATCH, CTX_SEQ, SWA_KV_HEADS, SWA_HEAD_DIM))
            mixed.append(_swa_attn_ctx(q, kd, vd, sink))
            q, kd, vd, _, _ = _swa_proj(xs[1], sc1[1], sh1[1], w_ext, seg_lens[1], rope_swa)
            rows = DEC_BATCH * PAST_LEN
            kd_ctx = _swa_key_layout(cache_swa_k[:, j].reshape(rows, SWA_KV_HEADS, SWA_HEAD_DIM)).astype(BF16)
            vd_ctx = _swa_value_layout(cache_swa_v[:, j].reshape(rows, SWA_KV_HEADS, SWA_HEAD_DIM)).astype(BF16)
            mixed.append(_swa_attn_lat(q, kd, vd, kd_ctx, vd_ctx, sink))

        x1, h2, routed = [], [], []
        for g in range(2):
            x1_g, h2_g, lg_g = _post(mixed[g], w_out, xs[g], g1[g], sc2[g], sh2[g], lng[0], lnb[0], router_pad, seg_lens[g])
            x1.append(x1_g)
            h2.append(h2_g)
            routed.append(_route(lg_g))
        idx_all = jnp.concatenate([routed[0][3], routed[1][3]], axis=2).reshape(-1)
        y = _moe_ffn(i, idx_all, h2[0], h2[1], routed[0][0], routed[1][0], w_gate_all, w_up_all, w_down_all)
        for g in range(2):
            roff = routed[g][2].reshape(-1)
            xs[g] = _combine(g, roff, y, routed[g][1], x1[g], g2[g], lng[1], lnb[1], seg_lens[g])

    return (xs[0].reshape(CTX_BATCH, CTX_SEQ, d), xs[1].reshape(DEC_BATCH, DEC_SEQ, d),
            jnp.stack(ckv_out, axis=1), jnp.stack(kpe_out, axis=1),
            jnp.stack(k_out, axis=1), jnp.stack(v_out, axis=1))
```

```python
import functools

import jax
import jax.numpy as jnp
from jax import lax
from jax.experimental import pallas as pl
from jax.experimental.pallas import tpu as pltpu

D_MODEL = 1024
N_TOK = 4096
DEPTH = 4
DEC_BATCH = 4
DEC_SEQ = 1024
PAST_LEN = 256
CTX_BATCH = 16
CTX_SEQ = 256
GRID_W = 64
ALPHA = (2.0 * DEPTH) ** 0.25
MLA_HEADS = 8
MLA_Q_LORA = 384
MLA_KV_LORA = 256
MLA_NOPE = 128
MLA_ROPE = 64
MLA_V = 128
MLA_QK_PAD = 256
GM_FF = 6 * D_MODEL
GM_HALF = GM_FF // 2
GM_GROUPS = 8
GM_GW = GM_HALF // GM_GROUPS
CHUNK = 128
SWA_HEADS = 16
SWA_KV_HEADS = 4
SWA_HEAD_DIM = 64
WINDOW = 128
BAND = 128
N_EXPERTS = 16
EXPERT_FF = 2 * D_MODEL
CAP = 2 * N_TOK // N_EXPERTS
ROPE_BASE = 10000.0
NEG_INF = -1e30
EPS = 1e-6
F32 = jnp.float32
BF16 = jnp.bfloat16
I32 = jnp.int32
U32 = jnp.uint32
HIGHEST = lax.Precision.HIGHEST

LANES = 128
SUBLANES = 8
ROW_TILES = D_MODEL // LANES
VMEM_LIMIT = 56 * 1024 * 1024
TM = 256
FF_TILE = 1024
TOK_BLK = 512
MOD_TN = 1536

F_SLOT_HI, F_SLOT_LO, F_G0, F_G1, F_G2 = 0, 16, 32, 48, 64
F_T_HI, F_T_LO = 80, 81


def _params(sem, vmem=VMEM_LIMIT):
    return pltpu.CompilerParams(dimension_semantics=sem, vmem_limit_bytes=vmem)


def _dot(a, b):
    return jnp.dot(a, b, preferred_element_type=F32)


def _dot_nt(a, b):
    return lax.dot_general(a, b, (((1,), (1,)), ((), ())), preferred_element_type=F32)


def _sigmoid(x):
    return 1.0 / (1.0 + jnp.exp(-x))


def _gelu_tanh(x):
    return 0.5 * x * (1.0 + jnp.tanh(0.7978845608028654 * (x + 0.044715 * (x * x * x))))


def _layer_norm(t, gain, bias):
    mu = jnp.mean(t, axis=-1, keepdims=True)
    d = t - mu
    var = jnp.mean(d * d, axis=-1, keepdims=True)
    return d * lax.rsqrt(var + EPS) * gain + bias


def _rms_norm(t, gain):
    return t * lax.rsqrt(jnp.mean(t * t, axis=-1, keepdims=True) + EPS) * gain


def _rope128(x, cos, sin):
    lane = lax.broadcasted_iota(I32, x.shape, 1)
    first = (lane % 32) < 16
    swapped = jnp.where(first, -pltpu.roll(x, 112, 1), pltpu.roll(x, 16, 1))
    return x * cos + swapped * sin


def _mod_kernel(cond_ref, w_ref, b_ref, o_ref):
    c = cond_ref[...]
    s = c * _sigmoid(c)
    o_ref[0] = jnp.dot(s, w_ref[0], precision=HIGHEST, preferred_element_type=F32) + b_ref[0]


def _modulation_all(cond8, mod_w, mod_b):
    return pl.pallas_call(
        _mod_kernel,
        grid=(DEPTH, 6 * D_MODEL // MOD_TN),
        in_specs=[pl.BlockSpec((8, D_MODEL), lambda i, n: (0, 0)),
                  pl.BlockSpec((1, D_MODEL, MOD_TN), lambda i, n: (i, 0, n)),
                  pl.BlockSpec((1, 1, MOD_TN), lambda i, n: (i, 0, n))],
        out_specs=pl.BlockSpec((1, 8, MOD_TN), lambda i, n: (i, 0, n)),
        out_shape=jax.ShapeDtypeStruct((DEPTH, 8, 6 * D_MODEL), F32),
        compiler_params=_params(("arbitrary", "arbitrary")),
        name="modulation",
    )(cond8, mod_w, mod_b.reshape(DEPTH, 1, 6 * D_MODEL))


def _seg_spec(seg_len):
    return pl.BlockSpec((1, 1, D_MODEL), lambda r: (r * TM // seg_len, 0, 0))


def _row_spec(width):
    return pl.BlockSpec((TM, width), lambda r: (r, 0))


def _full_spec(shape):
    nd = len(shape)
    return pl.BlockSpec(shape, lambda r: (0,) * nd)


def _mla_proj_kernel(rope, x_ref, sc_ref, sh_ref, win_ref, qg_ref, kvg_ref, wq_ref, wkv_ref, *rest):
    if rope:
        cos_ref, sin_ref, q_ref, kv_ref, kpe_ref, ckv_ref, kraw_ref = rest
    else:
        q_ref, kv_ref, kpe_ref, ckv_ref, kraw_ref = rest
    h = (x_ref[...] * (1.0 + sc_ref[0]) + sh_ref[0]).astype(BF16)
    z = _dot(h, win_ref[...])
    cq = _rms_norm(z[:, :MLA_Q_LORA], qg_ref[...])
    ckv = _rms_norm(z[:, MLA_Q_LORA:MLA_Q_LORA + MLA_KV_LORA], kvg_ref[...])
    kp = z[:, MLA_Q_LORA + MLA_KV_LORA:]
    ckv_ref[...] = ckv
    kraw_ref[...] = kp[:, :MLA_ROPE]
    q = _dot(cq.astype(BF16), wq_ref[...])
    kv_ref[...] = _dot(ckv.astype(BF16), wkv_ref[...]).astype(BF16)
    if rope:
        cos, sin = cos_ref[...], sin_ref[...]
        for hd in range(MLA_HEADS):
            a = hd * MLA_QK_PAD
            q_ref[:, a:a + LANES] = q[:, a:a + LANES].astype(BF16)
            q_ref[:, a + LANES:a + 2 * LANES] = _rope128(q[:, a + LANES:a + 2 * LANES], cos, sin).astype(BF16)
        kp = _rope128(kp, cos, sin)
    else:
        q_ref[...] = q.astype(BF16)
    kpe_ref[...] = kp.astype(BF16)


def _mla_proj(x, sc, sh, w, seg_len, rope_tabs):
    rope = rope_tabs is not None
    in_specs = [_row_spec(D_MODEL), _seg_spec(seg_len), _seg_spec(seg_len),
                _full_spec(w["w_in"].shape), _full_spec((1, MLA_Q_LORA)), _full_spec((1, MLA_KV_LORA)),
                _full_spec(w["w_q"].shape), _full_spec(w["w_kv"].shape)]
    args = [x, sc, sh, w["w_in"], w["q_gain"], w["kv_gain"], w["w_q"], w["w_kv"]]
    if rope:
        nblk = DEC_SEQ // TM
        tab_spec = pl.BlockSpec((TM, LANES), lambda r: (r % nblk, 0))
        in_specs += [tab_spec, tab_spec]
        args += list(rope_tabs)
    wq = MLA_HEADS * MLA_QK_PAD
    return pl.pallas_call(
        functools.partial(_mla_proj_kernel, rope),
        grid=(N_TOK // TM,),
        in_specs=in_specs,
        out_specs=[_row_spec(wq), _row_spec(wq), _row_spec(LANES), _row_spec(MLA_KV_LORA), _row_spec(MLA_ROPE)],
        out_shape=[jax.ShapeDtypeStruct((N_TOK, wq), BF16), jax.ShapeDtypeStruct((N_TOK, wq), BF16),
                   jax.ShapeDtypeStruct((N_TOK, LANES), BF16), jax.ShapeDtypeStruct((N_TOK, MLA_KV_LORA), F32),
                   jax.ShapeDtypeStruct((N_TOK, MLA_ROPE), F32)],
        compiler_params=_params(("arbitrary",)),
        name="mla_proj",
    )(*args)


def _mm_kernel(a_ref, w_ref, o_ref):
    o_ref[...] = _dot(a_ref[...], w_ref[...]).astype(o_ref.dtype)


def _matmul_bf16(a, w):
    m, k = a.shape
    n = w.shape[1]
    return pl.pallas_call(
        _mm_kernel,
        grid=(m // TM,),
        in_specs=[pl.BlockSpec((TM, k), lambda r: (r, 0)), _full_spec(w.shape)],
        out_specs=pl.BlockSpec((TM, n), lambda r: (r, 0)),
        out_shape=jax.ShapeDtypeStruct((m, n), BF16),
        compiler_params=_params(("arbitrary",)),
        name="matmul",
    )(a, w)


MLA_SCALE = (MLA_NOPE + MLA_ROPE) ** -0.5


def _mla_attn_ctx_kernel(q_ref, kv_ref, kpe_ref, o_ref):
    kpe = kpe_ref[...]
    for hd in range(MLA_HEADS):
        a = hd * MLA_QK_PAD
        qh = q_ref[:, a:a + MLA_QK_PAD]
        kh = jnp.concatenate([kv_ref[:, a:a + LANES], kpe], axis=1)
        vh = kv_ref[:, a + LANES:a + 2 * LANES]
        s = _dot_nt(qh, kh) * MLA_SCALE
        p = jnp.exp(s - jnp.max(s, axis=-1, keepdims=True))
        l = jnp.sum(p, axis=-1, keepdims=True)
        o_ref[:, hd * MLA_V:(hd + 1) * MLA_V] = (_dot(p.astype(BF16), vh) / l).astype(BF16)


def _mla_attn_ctx(q, kv, kpe):
    wq = MLA_HEADS * MLA_QK_PAD
    blk = lambda w: pl.BlockSpec((CTX_SEQ, w), lambda b: (b, 0))
    return pl.pallas_call(
        _mla_attn_ctx_kernel,
        grid=(CTX_BATCH,),
        in_specs=[blk(wq), blk(wq), blk(LANES)],
        out_specs=blk(MLA_HEADS * MLA_V),
        out_shape=jax.ShapeDtypeStruct((N_TOK, MLA_HEADS * MLA_V), BF16),
        compiler_params=_params(("arbitrary",)),
        name="mla_attn_ctx",
    )(q, kv, kpe)


def _mla_attn_lat_kernel(q_ref, kvl_ref, kpel_ref, kvc_ref, kpec_ref, o_ref):
    kpel = kpel_ref[...]
    kpec = kpec_ref[...]
    for hd in range(MLA_HEADS):
        a = hd * MLA_QK_PAD
        qh = q_ref[:, a:a + MLA_QK_PAD]
        kc = jnp.concatenate([kvc_ref[:, a:a + LANES], kpec], axis=1)
        kl = jnp.concatenate([kvl_ref[:, a:a + LANES], kpel], axis=1)
        sc = _dot_nt(qh, kc) * MLA_SCALE
        sl = _dot_nt(qh, kl) * MLA_SCALE
        m = jnp.maximum(jnp.max(sc, axis=-1, keepdims=True), jnp.max(sl, axis=-1, keepdims=True))
        pc = jnp.exp(sc - m)
        plat = jnp.exp(sl - m)
        l = jnp.sum(pc, axis=-1, keepdims=True) + jnp.sum(plat, axis=-1, keepdims=True)
        o = _dot(pc.astype(BF16), kvc_ref[:, a + LANES:a + 2 * LANES]) \
            + _dot(plat.astype(BF16), kvl_ref[:, a + LANES:a + 2 * LANES])
        o_ref[:, hd * MLA_V:(hd + 1) * MLA_V] = (o / l).astype(BF16)


def _mla_attn_lat(q, kv, kpe, kv_ctx, kpe_ctx):
    wq = MLA_HEADS * MLA_QK_PAD
    nq = DEC_SEQ // TM
    return pl.pallas_call(
        _mla_attn_lat_kernel,
        grid=(DEC_BATCH, nq),
        in_specs=[pl.BlockSpec((TM, wq), lambda b, i: (b * nq + i, 0)),
                  pl.BlockSpec((DEC_SEQ, wq), lambda b, i: (b, 0)),
                  pl.BlockSpec((DEC_SEQ, LANES), lambda b, i: (b, 0)),
                  pl.BlockSpec((PAST_LEN, wq), lambda b, i: (b, 0)),
                  pl.BlockSpec((PAST_LEN, LANES), lambda b, i: (b, 0))],
        out_specs=pl.BlockSpec((TM, MLA_HEADS * MLA_V), lambda b, i: (b * nq + i, 0)),
        out_shape=jax.ShapeDtypeStruct((N_TOK, MLA_HEADS * MLA_V), BF16),
        compiler_params=_params(("arbitrary", "arbitrary")),
        name="mla_attn_lat",
    )(q, kv, kpe, kv_ctx, kpe_ctx)


def _gmlp_kernel(x_ref, sc_ref, sh_ref, win_ref, vg_ref, ws_ref, bs_ref, p_ref):
    h = (x_ref[...] * (1.0 + sc_ref[0]) + sh_ref[0]).astype(BF16)
    zv = _gelu_tanh(_dot(h, win_ref[:, GM_HALF:]))
    mu = jnp.mean(zv, axis=-1, keepdims=True)
    d = zv - mu
    var = jnp.mean(d * d, axis=-1, keepdims=True)
    vn = (d * lax.rsqrt(var + EPS) * vg_ref[...]).astype(BF16)
    for g in range(GM_GROUPS):
        c0 = g * GM_GW
        u = _gelu_tanh(_dot(h, win_ref[:, c0:c0 + GM_GW]))
        mixed = jnp.concatenate(
            [_dot(ws_ref[g], vn[k * CHUNK:(k + 1) * CHUNK, c0:c0 + GM_GW]) for k in range(TM // CHUNK)], axis=0)
        bias = jnp.concatenate([bs_ref[g]] * (TM // CHUNK), axis=0)
        p_ref[:, c0:c0 + GM_GW] = (u * (mixed + bias)).astype(BF16)


def _gmlp_in(x, sc, sh, w, seg_len):
    return pl.pallas_call(
        _gmlp_kernel,
        grid=(N_TOK // TM,),
        in_specs=[_row_spec(D_MODEL), _seg_spec(seg_len), _seg_spec(seg_len),
                  _full_spec((D_MODEL, GM_FF)), _full_spec((1, GM_HALF)),
                  _full_spec((GM_GROUPS, CHUNK, CHUNK)), _full_spec((GM_GROUPS, CHUNK, 1))],
        out_specs=_row_spec(GM_HALF),
        out_shape=jax.ShapeDtypeStruct((N_TOK, GM_HALF), BF16),
        compiler_params=_params(("arbitrary",)),
        name="gmlp_in",
    )(x, sc, sh, w["w_in"], w["v_gain"], w["w_s"], w["b_s"])


SWA_SCALE = SWA_HEAD_DIM ** -0.5
SWA_QW = SWA_HEADS * SWA_HEAD_DIM
SWA_KW = SWA_KV_HEADS * 2 * LANES
SWA_VW = SWA_KV_HEADS * LANES
SWA_GROUP = SWA_KV_HEADS


def _swa_proj_kernel(rope, x_ref, sc_ref, sh_ref, w_ref, *rest):
    if rope:
        cos_ref, sin_ref, q_ref, kd_ref, vd_ref, kraw_ref, vraw_ref = rest
    else:
        q_ref, kd_ref, vd_ref, kraw_ref, vraw_ref = rest
    h = (x_ref[...] * (1.0 + sc_ref[0]) + sh_ref[0]).astype(BF16)
    z = _dot(h, w_ref[...])
    lane = lax.broadcasted_iota(I32, (TM, LANES), 1)
    low = lane < SWA_HEAD_DIM
    for m in range(SWA_KV_HEADS // 2):
        k0 = z[:, SWA_QW + (2 * m) * 2 * LANES:SWA_QW + (2 * m) * 2 * LANES + LANES]
        k1 = z[:, SWA_QW + (2 * m + 1) * 2 * LANES + LANES:SWA_QW + (2 * m + 2) * 2 * LANES]
        kraw_ref[:, m * LANES:(m + 1) * LANES] = jnp.where(low, k0, k1)
        v0 = z[:, SWA_QW + SWA_KW + (2 * m) * LANES:SWA_QW + SWA_KW + (2 * m + 1) * LANES]
        v1 = z[:, SWA_QW + SWA_KW + (2 * m + 1) * LANES:SWA_QW + SWA_KW + (2 * m + 2) * LANES]
        vraw_ref[:, m * LANES:(m + 1) * LANES] = jnp.where(low, v0, v1)
    vd_ref[...] = z[:, SWA_QW + SWA_KW:].astype(BF16)
    if rope:
        cos, sin = cos_ref[...], sin_ref[...]
        for j in range(SWA_QW // LANES):
            q_ref[:, j * LANES:(j + 1) * LANES] = _rope128(z[:, j * LANES:(j + 1) * LANES], cos, sin).astype(BF16)
        for j in range(SWA_KW // LANES):
            a = SWA_QW + j * LANES
            kd_ref[:, j * LANES:(j + 1) * LANES] = _rope128(z[:, a:a + LANES], cos, sin).astype(BF16)
    else:
        q_ref[...] = z[:, :SWA_QW].astype(BF16)
        kd_ref[...] = z[:, SWA_QW:SWA_QW + SWA_KW].astype(BF16)


def _swa_proj(x, sc, sh, w_ext, seg_len, rope_tabs):
    rope = rope_tabs is not None
    in_specs = [_row_spec(D_MODEL), _seg_spec(seg_len), _seg_spec(seg_len), _full_spec(w_ext.shape)]
    args = [x, sc, sh, w_ext]
    if rope:
        nblk = DEC_SEQ // TM
        tab_spec = pl.BlockSpec((TM, LANES), lambda r: (r % nblk, 0))
        in_specs += [tab_spec, tab_spec]
        args += list(rope_tabs)
    kvw = SWA_KV_HEADS * SWA_HEAD_DIM
    return pl.pallas_call(
        functools.partial(_swa_proj_kernel, rope),
        grid=(N_TOK // TM,),
        in_specs=in_specs,
        out_specs=[_row_spec(SWA_QW), _row_spec(SWA_KW), _row_spec(SWA_VW), _row_spec(kvw), _row_spec(kvw)],
        out_shape=[jax.ShapeDtypeStruct((N_TOK, SWA_QW), BF16), jax.ShapeDtypeStruct((N_TOK, SWA_KW), BF16),
                   jax.ShapeDtypeStruct((N_TOK, SWA_VW), BF16), jax.ShapeDtypeStruct((N_TOK, kvw), F32),
                   jax.ShapeDtypeStruct((N_TOK, kvw), F32)],
        compiler_params=_params(("arbitrary",)),
        name="swa_proj",
    )(*args)


def _swa_heads(q_ref, sink_ref, o_ref, score_fn, value_fn):
    rows = q_ref.shape[0]
    lane = lax.broadcasted_iota(I32, (rows, LANES), 1)
    low = lane < SWA_HEAD_DIM
    for hk0 in range(0, SWA_KV_HEADS, SWA_GROUP):
        stacked, sinks = None, []
        for hk in range(hk0, hk0 + SWA_GROUP):
            c0 = hk * 2 * LANES
            q2 = jnp.concatenate([q_ref[:, c0:c0 + LANES], q_ref[:, c0 + LANES:c0 + 2 * LANES]], axis=0)
            for par in range(2):
                for pr in range(2):
                    hd = hk * 4 + pr * 2 + par
                    sinks.append(jnp.broadcast_to(sink_ref[:, hd:hd + 1], (rows, 1)))
                blocks = score_fn(hk, par, q2)
                if stacked is None:
                    stacked = [[] for _ in blocks]
                for kind, s in zip(stacked, blocks):
                    kind.append(s)
        scores = [jnp.concatenate(kind, axis=0) for kind in stacked]
        sk = jnp.concatenate(sinks, axis=0)
        m = sk
        for s in scores:
            m = jnp.maximum(m, jnp.max(s, axis=-1, keepdims=True))
        probs = [jnp.exp(s - m) for s in scores]
        l = jnp.exp(sk - m)
        for p in probs:
            l = l + jnp.sum(p, axis=-1, keepdims=True)
        inv = 1.0 / l
        n = 0
        for hk in range(hk0, hk0 + SWA_GROUP):
            res = []
            for par in range(2):
                blk = slice(n * 2 * rows, (n + 1) * 2 * rows)
                res.append(value_fn(hk, [p[blk] for p in probs]) * inv[blk])
                n += 1
            for pr in range(2):
                c0 = hk * 2 * LANES + pr * LANES
                part = slice(pr * rows, (pr + 1) * rows)
                o_ref[:, c0:c0 + LANES] = jnp.where(low, res[0][part], res[1][part]).astype(BF16)


def _swa_attn_ctx_kernel(q_ref, kd_ref, vd_ref, sink_ref, o_ref):
    def score_fn(hk, par, qp):
        a = hk * 2 * LANES + par * LANES
        return [_dot_nt(qp, kd_ref[:, a:a + LANES]) * SWA_SCALE]

    def value_fn(hk, probs):
        return _dot(probs[0].astype(BF16), vd_ref[:, hk * LANES:(hk + 1) * LANES])

    _swa_heads(q_ref, sink_ref, o_ref, score_fn, value_fn)


def _swa_attn_ctx(q, kd, vd, sink):
    blk = lambda w: pl.BlockSpec((CTX_SEQ, w), lambda b: (b, 0))
    return pl.pallas_call(
        _swa_attn_ctx_kernel,
        grid=(CTX_BATCH,),
        in_specs=[blk(SWA_QW), blk(SWA_KW), blk(SWA_VW), pl.BlockSpec((1, SWA_HEADS), lambda b: (0, 0))],
        out_specs=blk(SWA_QW),
        out_shape=jax.ShapeDtypeStruct((N_TOK, SWA_QW), BF16),
        compiler_params=_params(("arbitrary",)),
        name="swa_attn_ctx",
    )(q, kd, vd, sink)


SWA_WIN = 3 * BAND


def _swa_attn_lat_kernel(q_ref, kdl_ref, vdl_ref, kdc_ref, vdc_ref, sink_ref, o_ref):
    nb = pl.program_id(1)
    start = pl.multiple_of(jnp.clip((nb - 1) * BAND, 0, DEC_SEQ - SWA_WIN), BAND)
    qpos = nb * BAND + (lax.broadcasted_iota(I32, (2 * BAND, SWA_WIN), 0) & (BAND - 1))
    kpos = start + lax.broadcasted_iota(I32, (2 * BAND, SWA_WIN), 1)
    valid = jnp.abs(qpos - kpos) <= WINDOW

    def score_fn(hk, par, qp):
        a = hk * 2 * LANES + par * LANES
        kl = kdl_ref[pl.ds(start, SWA_WIN), a:a + LANES]
        sl = jnp.where(valid, _dot_nt(qp, kl) * SWA_SCALE, NEG_INF)
        sc = _dot_nt(qp, kdc_ref[:, a:a + LANES]) * SWA_SCALE
        return [sl, sc]

    def value_fn(hk, probs):
        vl = vdl_ref[pl.ds(start, SWA_WIN), hk * LANES:(hk + 1) * LANES]
        return _dot(probs[0].astype(BF16), vl) + _dot(probs[1].astype(BF16), vdc_ref[:, hk * LANES:(hk + 1) * LANES])

    _swa_heads(q_ref, sink_ref, o_ref, score_fn, value_fn)


def _swa_attn_lat(q, kd, vd, kd_ctx, vd_ctx, sink):
    nbands = DEC_SEQ // BAND
    return pl.pallas_call(
        _swa_attn_lat_kernel,
        grid=(DEC_BATCH, nbands),
        in_specs=[pl.BlockSpec((BAND, SWA_QW), lambda b, i: (b * nbands + i, 0)),
                  pl.BlockSpec((DEC_SEQ, SWA_KW), lambda b, i: (b, 0)),
                  pl.BlockSpec((DEC_SEQ, SWA_VW), lambda b, i: (b, 0)),
                  pl.BlockSpec((PAST_LEN, SWA_KW), lambda b, i: (b, 0)),
                  pl.BlockSpec((PAST_LEN, SWA_VW), lambda b, i: (b, 0)),
                  pl.BlockSpec((1, SWA_HEADS), lambda b, i: (0, 0))],
        out_specs=pl.BlockSpec((BAND, SWA_QW), lambda b, i: (b * nbands + i, 0)),
        out_shape=jax.ShapeDtypeStruct((N_TOK, SWA_QW), BF16),
        compiler_params=_params(("arbitrary", "arbitrary")),
        name="swa_attn_lat",
    )(q, kd, vd, kd_ctx, vd_ctx, sink)


def _post_kernel(a_ref, w_ref, x_ref, g1_ref, sc2_ref, sh2_ref, lng_ref, lnb_ref, rt_ref, x1_ref, h2_ref, lg_ref):
    y = _dot(a_ref[...], w_ref[...])
    x1 = _layer_norm(ALPHA * x_ref[...] + g1_ref[0] * y, lng_ref[...], lnb_ref[...])
    x1_ref[...] = x1
    h2 = x1 * (1.0 + sc2_ref[0]) + sh2_ref[0]
    for j in range(ROW_TILES):
        h2_ref[pl.ds(j, TM, stride=ROW_TILES), :] = h2[:, j * LANES:(j + 1) * LANES]
    rt = rt_ref[...]
    h_hi = h2.astype(BF16)
    h_lo = (h2 - h_hi.astype(F32)).astype(BF16)
    r_hi = rt.astype(BF16)
    r_lo = (rt - r_hi.astype(F32)).astype(BF16)
    lg_ref[...] = _dot(h_hi, r_hi) + (_dot(h_lo, r_hi) + _dot(h_hi, r_lo))


def _post(a, w_out, x, g1, sc2, sh2, lng, lnb, router_pad, seg_len):
    k = a.shape[1]
    return pl.pallas_call(
        _post_kernel,
        grid=(N_TOK // TM,),
        in_specs=[_row_spec(k), _full_spec((k, D_MODEL)), _row_spec(D_MODEL),
                  _seg_spec(seg_len), _seg_spec(seg_len), _seg_spec(seg_len),
                  _full_spec((1, D_MODEL)), _full_spec((1, D_MODEL)), _full_spec((D_MODEL, LANES))],
        out_specs=[_row_spec(D_MODEL), pl.BlockSpec((TM * ROW_TILES, LANES), lambda r: (r, 0)), _row_spec(LANES)],
        out_shape=[jax.ShapeDtypeStruct((N_TOK, D_MODEL), F32), jax.ShapeDtypeStruct((N_TOK * ROW_TILES, LANES), F32),
                   jax.ShapeDtypeStruct((N_TOK, LANES), F32)],
        compiler_params=_params(("arbitrary",)),
        name="mixer_out",
    )(a, w_out, x, g1, sc2, sh2, lng, lnb, router_pad)


CUM_BLK = 256
REFINE_STEPS = 16
F32_MIN_NORMAL = 1.1754943508222875e-38


def _excl_cumsum_lanes(a, upper):
    outs = []
    carry = jnp.zeros((a.shape[0], 1), F32)
    for b in range(N_TOK // CUM_BLK):
        blk = a[:, b * CUM_BLK:(b + 1) * CUM_BLK]
        outs.append(_dot(blk.astype(BF16), upper) + carry)
        carry = carry + jnp.sum(blk, axis=1, keepdims=True)
    return jnp.concatenate(outs, axis=1)


def _route_kernel(lg_ref, r_ref, tok_ref, roff_ref, idx_ref, rank_scr, w_scr):
    e_rows = N_EXPERTS
    lt = lg_ref[...].T[:e_rows]
    ex = jnp.exp(lt - jnp.max(lt, axis=0, keepdims=True))
    aff = ex / jnp.sum(ex, axis=0, keepdims=True)
    def count_ge(thr):
        return jnp.sum(jnp.where(aff >= thr, 1.0, 0.0), axis=1, keepdims=True)

    cur = jnp.zeros((e_rows, 1), I32)
    for b in range(30, -1, -1):
        cand = cur | jnp.int32(1 << b)
        cur = jnp.where(count_ge(lax.bitcast_convert_type(cand, F32)) >= float(CAP), cand, cur)
    lo = lax.bitcast_convert_type(cur, F32)
    hi = jnp.maximum(lax.bitcast_convert_type(cur + 1, F32), F32_MIN_NORMAL)
    for _ in range(REFINE_STEPS):
        w = hi - lo
        t1, t2, t3 = lo + 0.25 * w, lo + 0.5 * w, lo + 0.75 * w
        ok1, ok2, ok3 = (count_ge(t) >= float(CAP) for t in (t1, t2, t3))
        lo, hi = (jnp.where(ok3, t3, jnp.where(ok2, t2, jnp.where(ok1, t1, lo))),
                  jnp.where(ok1, jnp.where(ok2, jnp.where(ok3, hi, t3), t2), t1))
    gt = aff >= hi
    eq = (aff >= lo) & (aff < hi)
    need = float(CAP) - jnp.sum(jnp.where(gt, 1.0, 0.0), axis=1, keepdims=True)

    ri = lax.broadcasted_iota(I32, (CUM_BLK, CUM_BLK), 0)
    ci = lax.broadcasted_iota(I32, (CUM_BLK, CUM_BLK), 1)
    upper = jnp.where(ri < ci, 1.0, 0.0).astype(BF16)

    tie_rank = _excl_cumsum_lanes(jnp.where(eq, 1.0, 0.0), upper)
    sel = gt | (eq & (tie_rank < need))
    msk = jnp.where(sel, 1.0, 0.0)

    rank = _excl_cumsum_lanes(msk, upper)
    rank_scr[...] = jnp.where(sel, rank, -1.0)

    inv = 1.0 / LANES
    slot_no = jnp.where(sel, rank, -float(LANES))
    slot_hi = jnp.floor(slot_no * inv)
    g0 = aff.astype(BF16).astype(F32)
    r1 = aff - g0
    g1 = r1.astype(BF16).astype(F32)
    g2 = (r1 - g1).astype(BF16).astype(F32)
    tt = lax.broadcasted_iota(I32, (1, N_TOK), 1).astype(F32)
    t_hi = jnp.floor(tt * inv)
    misc = jnp.concatenate([t_hi, tt - t_hi * LANES, jnp.zeros((6, N_TOK), F32)], axis=0)
    feat = jnp.concatenate([slot_hi, slot_no - slot_hi * LANES, g0, g1, g2, misc,
                            jnp.zeros((LANES - 5 * e_rows - 8, N_TOK), F32)], axis=0)
    table = feat.T
    tok_ref[...] = table
    w_scr[...] = table.astype(BF16)

    nblk = N_TOK // TOK_BLK
    bt = lax.broadcasted_iota(I32, (N_TOK, LANES), 0) // TOK_BLK
    bb = lax.broadcasted_iota(I32, (N_TOK, LANES), 1)
    before = jnp.where((bt < bb) & (bb <= nblk), 1.0, 0.0).astype(BF16)
    roff_ref[...] = _dot(msk.astype(BF16), before).astype(I32)

    sub = lax.broadcasted_iota(I32, (8, LANES), 0)
    ln8 = lax.broadcasted_iota(I32, (8, LANES), 1)
    rb_rows = 128

    def per_expert(e, carry):
        rk = rank_scr[pl.ds(e, 1), :]
        want = jnp.where(sub < 5, sub * e_rows + e, sub + (F_T_HI - 5))
        pick = jnp.where(ln8 == want, 1.0, 0.0).astype(BF16)
        for rb in range(CAP // rb_rows):
            slot = (rb * rb_rows + lax.broadcasted_iota(I32, (rb_rows, 1), 0)).astype(F32)
            onehot = jnp.where(rk == slot, 1.0, 0.0).astype(BF16)
            rows = _dot(onehot, w_scr[...])
            r_ref[pl.ds(e, 1), rb * rb_rows:(rb + 1) * rb_rows, :] = rows[None]
            got = _dot_nt(pick, rows.astype(BF16))
            idx_ref[pl.ds(e, 1), :, rb * rb_rows:(rb + 1) * rb_rows] = (got[5:6] * LANES + got[6:7]).astype(I32)[None]
        return carry

    lax.fori_loop(0, e_rows, per_expert, 0)


def _route(logits):
    return pl.pallas_call(
        _route_kernel,
        grid=(1,),
        in_specs=[pl.BlockSpec((N_TOK, LANES), lambda i: (0, 0))],
        out_specs=[pl.BlockSpec((N_EXPERTS, CAP, LANES), lambda i: (0, 0, 0)),
                   pl.BlockSpec((N_TOK, LANES), lambda i: (0, 0)),
                   pl.BlockSpec((N_EXPERTS, LANES), lambda i: (0, 0)),
                   pl.BlockSpec((N_EXPERTS, 1, CAP), lambda i: (0, 0, 0))],
        out_shape=[jax.ShapeDtypeStruct((N_EXPERTS, CAP, LANES), F32),
                   jax.ShapeDtypeStruct((N_TOK, LANES), F32),
                   jax.ShapeDtypeStruct((N_EXPERTS, LANES), I32),
                   jax.ShapeDtypeStruct((N_EXPERTS, 1, CAP), I32)],
        scratch_shapes=[pltpu.VMEM((N_EXPERTS, N_TOK), F32), pltpu.VMEM((N_TOK, LANES), BF16)],
        compiler_params=_params(("arbitrary",)),
        name="moe_route",
    )(logits)


ROWS_E = 2 * CAP
N_FT = EXPERT_FF // FF_TILE
GATHER_UNROLL = 8
PIECE = 256
N_PIECE = FF_TILE // PIECE


def _tile(row):
    return pl.ds(pl.multiple_of(row * ROW_TILES, ROW_TILES), ROW_TILES)


def _ffn_kernel(idx_s, hc_hbm, hl_hbm, rc_ref, rl_ref, wg_ref, wu_ref, wd_ref, y_ref, xe, xb, acc, gsem):
    e = pl.program_id(0)
    f = pl.program_id(1)

    def gather_rows(ee):
        base = ee * ROWS_E

        def body(r, carry):
            tc = idx_s[base + r]
            pltpu.make_async_copy(hc_hbm.at[_tile(tc)], xe.at[_tile(r)], gsem.at[0]).start(priority=0)
            tl = idx_s[base + CAP + r]
            pltpu.make_async_copy(hl_hbm.at[_tile(tl)], xe.at[_tile(CAP + r)], gsem.at[0]).start(priority=1)
            return carry

        lax.fori_loop(0, CAP, body, 0, unroll=GATHER_UNROLL)

    @pl.when(f == 0)
    def _():
        @pl.when(e == 0)
        def _():
            gather_rows(0)

        pltpu.make_async_copy(xe, xe, gsem.at[0]).wait()
        for j in range(ROW_TILES):
            xb[:, j * LANES:(j + 1) * LANES] = xe[pl.ds(j, ROWS_E, stride=ROW_TILES), :].astype(BF16)

        @pl.when(e + 1 < N_EXPERTS)
        def _():
            gather_rows(e + 1)

        acc[...] = jnp.zeros_like(acc)

    x = xb[...]
    for p in range(N_PIECE):
        cols = slice(p * PIECE, (p + 1) * PIECE)
        g = _dot(x, wg_ref[0, :, cols].astype(BF16))
        u = _dot(x, wu_ref[0, :, cols].astype(BF16))
        hid = (g * _sigmoid(g) * u).astype(BF16)
        acc[...] += _dot(hid, wd_ref[0, cols, :].astype(BF16))

    @pl.when(f == N_FT - 1)
    def _():
        lane = lax.broadcasted_iota(I32, (CAP, LANES), 1)
        mine = (lane == F_G0 + e) | (lane == F_G1 + e) | (lane == F_G2 + e)
        gate = jnp.concatenate(
            [jnp.sum(jnp.where(mine, rc_ref[0], 0.0), axis=1, keepdims=True),
             jnp.sum(jnp.where(mine, rl_ref[0], 0.0), axis=1, keepdims=True)], axis=0)
        y_ref[0] = acc[...] * gate


def _moe_ffn(layer, idx_all, h_ctx, h_lat, r_ctx, r_lat, w_gate, w_up, w_down):
    any_spec = pl.BlockSpec(memory_space=pl.ANY)
    e0 = layer * N_EXPERTS
    grid_spec = pltpu.PrefetchScalarGridSpec(
        num_scalar_prefetch=1,
        grid=(N_EXPERTS, N_FT),
        in_specs=[any_spec, any_spec,
                  pl.BlockSpec((1, CAP, LANES), lambda e, f, *_: (e, 0, 0)),
                  pl.BlockSpec((1, CAP, LANES), lambda e, f, *_: (e, 0, 0)),
                  pl.BlockSpec((1, D_MODEL, FF_TILE), lambda e, f, *_: (e0 + e, 0, f)),
                  pl.BlockSpec((1, D_MODEL, FF_TILE), lambda e, f, *_: (e0 + e, 0, f)),
                  pl.BlockSpec((1, FF_TILE, D_MODEL), lambda e, f, *_: (e0 + e, f, 0))],
        out_specs=pl.BlockSpec((1, ROWS_E, D_MODEL), lambda e, f, *_: (e, 0, 0)),
        scratch_shapes=[pltpu.VMEM((ROWS_E * ROW_TILES, LANES), F32), pltpu.VMEM((ROWS_E, D_MODEL), BF16),
                        pltpu.VMEM((ROWS_E, D_MODEL), F32), pltpu.SemaphoreType.DMA((1,))],
    )
    return pl.pallas_call(
        _ffn_kernel,
        grid_spec=grid_spec,
        out_shape=jax.ShapeDtypeStruct((N_EXPERTS, ROWS_E, D_MODEL), F32),
        compiler_params=_params(("arbitrary", "arbitrary")),
        name="moe_ffn",
    )(idx_all, h_ctx, h_lat, r_ctx, r_lat, w_gate, w_up, w_down)


WIN_E = 96
EXPERTS_PER_DOT = 4
N_TBLK = N_TOK // TOK_BLK
ROFF_STRIDE = LANES


def _combine_kernel(group, roff_s, y_hbm, tok_ref, x_ref, g2_ref, lng_ref, lnb_ref, o_ref, ywin, yext, acc, wsem, esem):
    b = pl.program_id(0)

    def win_start(e, bb):
        first = roff_s[e * ROFF_STRIDE + bb]
        return jnp.minimum((first // SUBLANES) * SUBLANES, CAP - WIN_E)

    def rows_at(start):
        return pl.ds(pl.multiple_of(group * CAP + start, SUBLANES), WIN_E)

    def start_windows(bb, slot):
        for e in range(N_EXPERTS):
            pltpu.make_async_copy(y_hbm.at[e, rows_at(win_start(e, bb))], ywin.at[slot, e], wsem.at[slot]).start()

    @pl.when(b == 0)
    def _():
        start_windows(0, 0)

    @pl.when(b + 1 < N_TBLK)
    def _():
        start_windows(b + 1, (b + 1) % 2)

    slot = b % 2
    tok = tok_ref[...]
    lane = lax.broadcasted_iota(I32, (TOK_BLK, WIN_E), 1)
    owner = [tok[:, F_SLOT_HI + e:F_SLOT_HI + e + 1] * LANES + tok[:, F_SLOT_LO + e:F_SLOT_LO + e + 1]
             for e in range(N_EXPERTS)]
    starts = [win_start(e, b) for e in range(N_EXPERTS)]

    pltpu.make_async_copy(ywin.at[slot], ywin.at[slot], wsem.at[slot]).wait()
    wide = lax.broadcasted_iota(I32, (TOK_BLK, EXPERTS_PER_DOT * WIN_E), 1)
    total = None
    for e0 in range(0, N_EXPERTS, EXPERTS_PER_DOT):
        own, slots = owner[e0], (starts[e0] + wide).astype(F32)
        for k in range(1, EXPERTS_PER_DOT):
            later = wide >= k * WIN_E
            own = jnp.where(later, owner[e0 + k], own)
            slots = jnp.where(later, (starts[e0 + k] - k * WIN_E + wide).astype(F32), slots)
        seg = jnp.where(own == slots, 1.0, 0.0).astype(BF16)
        rows = ywin[slot, e0:e0 + EXPERTS_PER_DOT].reshape(EXPERTS_PER_DOT * WIN_E, D_MODEL)
        part = _dot(seg, rows.astype(BF16))
        total = part if total is None else total + part
    acc[...] = total

    for e in range(N_EXPERTS):
        past = jnp.maximum(roff_s[e * ROFF_STRIDE + b + 1] - (starts[e] + WIN_E), 0)

        def over(k, carry, e=e):
            nominal = starts[e] + (k + 1) * WIN_E
            first = jnp.minimum(nominal, CAP - WIN_E)
            cp = pltpu.make_async_copy(y_hbm.at[e, rows_at(first)], yext, esem.at[0])
            cp.start()
            cp.wait()
            slots = first + lane
            seg = jnp.where((owner[e] == slots.astype(F32)) & (slots >= nominal), 1.0, 0.0).astype(BF16)
            acc[...] += _dot(seg, yext[...].astype(BF16))
            return carry

        lax.fori_loop(0, (past + WIN_E - 1) // WIN_E, over, 0)

    o_ref[...] = _layer_norm(ALPHA * x_ref[...] + g2_ref[0] * acc[...], lng_ref[...], lnb_ref[...])


def _combine(group, roff, y, tok, x1, g2, lng, lnb, seg_len):
    grid_spec = pltpu.PrefetchScalarGridSpec(
        num_scalar_prefetch=1,
        grid=(N_TBLK,),
        in_specs=[pl.BlockSpec(memory_space=pl.ANY),
                  pl.BlockSpec((TOK_BLK, LANES), lambda b, *_: (b, 0)),
                  pl.BlockSpec((TOK_BLK, D_MODEL), lambda b, *_: (b, 0)),
                  pl.BlockSpec((1, 1, D_MODEL), lambda b, *_: (b * TOK_BLK // seg_len, 0, 0)),
                  pl.BlockSpec((1, D_MODEL), lambda b, *_: (0, 0)),
                  pl.BlockSpec((1, D_MODEL), lambda b, *_: (0, 0))],
        out_specs=pl.BlockSpec((TOK_BLK, D_MODEL), lambda b, *_: (b, 0)),
        scratch_shapes=[pltpu.VMEM((2, N_EXPERTS, WIN_E, D_MODEL), F32), pltpu.VMEM((WIN_E, D_MODEL), F32),
                        pltpu.VMEM((TOK_BLK, D_MODEL), F32),
                        pltpu.SemaphoreType.DMA((2,)), pltpu.SemaphoreType.DMA((1,))],
    )
    return pl.pallas_call(
        functools.partial(_combine_kernel, group),
        grid_spec=grid_spec,
        out_shape=jax.ShapeDtypeStruct((N_TOK, D_MODEL), F32),
        compiler_params=_params(("arbitrary",)),
        name="moe_combine",
    )(roff, y, tok, x1, g2, lng, lnb)


def _rope_tables():
    t = jnp.arange(DEC_SEQ)
    n_freq = MLA_ROPE // 4
    inv_freq = ROPE_BASE ** (-jnp.arange(n_freq, dtype=F32) / n_freq)
    ang_r = (t // GRID_W).astype(F32)[:, None] * inv_freq[None, :]
    ang_c = (t % GRID_W).astype(F32)[:, None] * inv_freq[None, :]
    cos64 = jnp.concatenate([jnp.cos(ang_r)] * 2 + [jnp.cos(ang_c)] * 2, axis=1)
    sin64 = jnp.concatenate([jnp.sin(ang_r)] * 2 + [jnp.sin(ang_c)] * 2, axis=1)
    one, zero = jnp.ones_like(cos64), jnp.zeros_like(sin64)
    mla = (jnp.concatenate([cos64, one], axis=1), jnp.concatenate([sin64, zero], axis=1))
    swa = (jnp.concatenate([cos64, cos64], axis=1), jnp.concatenate([sin64, sin64], axis=1))
    return mla, swa


def _swa_key_layout(k):
    z = jnp.zeros_like(k)
    return jnp.stack([k, z, z, k], axis=2).reshape(k.shape[0], SWA_KW)


def _swa_value_layout(v):
    return jnp.stack([v, v], axis=2).reshape(v.shape[0], SWA_VW)


def kernel(x_prompt, x_sample, cache_mla_ckv, cache_mla_kpe, cache_swa_k, cache_swa_v, c, c_ctx, mod_w, mod_b, ln_gain, ln_bias, mla_w_in, mla_q_gain, mla_kv_gain, mla_w_q_up, mla_w_kv_up, mla_w_out, gm_w_in, gm_v_gain, gm_w_s, gm_b_s, gm_w_out, swa_w_qkv, swa_sink, swa_w_out, moe_router, moe_w_gate, moe_w_up, moe_w_down):
    d = D_MODEL
    xs = [x_prompt.reshape(N_TOK, d), x_sample.reshape(N_TOK, d)]
    seg_lens = [N_TOK, DEC_SEQ]
    cond8 = jnp.concatenate([c_ctx[None, :], c, jnp.zeros((3, d), F32)], axis=0)
    mod = _modulation_all(cond8, mod_w, mod_b)
    rope_mla, rope_swa = _rope_tables()
    w_gate_all = moe_w_gate.reshape(DEPTH * N_EXPERTS, d, EXPERT_FF)
    w_up_all = moe_w_up.reshape(DEPTH * N_EXPERTS, d, EXPERT_FF)
    w_down_all = moe_w_down.reshape(DEPTH * N_EXPERTS, EXPERT_FF, d)

    def mod_rows(i, q):
        m = mod[i, :, q * d:(q + 1) * d]
        return [m[0:1].reshape(1, 1, d), m[1:1 + DEC_BATCH].reshape(DEC_BATCH, 1, d)]

    ckv_out, kpe_out, k_out, v_out = [], [], [], []
    for i in range(DEPTH):
        kind, j = i % 3, i // 3
        sh1, sc1, g1, sh2, sc2, g2 = (mod_rows(i, q) for q in range(6))
        router_pad = jnp.pad(moe_router[i], ((0, 0), (0, LANES - N_EXPERTS)))
        lng = ln_gain[i].reshape(2, 1, d)
        lnb = ln_bias[i].reshape(2, 1, d)

        mixed = []
        if kind == 0:
            w = {
                "w_in": jnp.pad(mla_w_in[j], ((0, 0), (0, MLA_ROPE))).astype(BF16),
                "q_gain": mla_q_gain[j][None, :], "kv_gain": mla_kv_gain[j][None, :],
                "w_q": jnp.pad(mla_w_q_up[j], ((0, 0), (0, 0), (0, MLA_QK_PAD - MLA_NOPE - MLA_ROPE)))
                .reshape(MLA_Q_LORA, MLA_HEADS * MLA_QK_PAD).astype(BF16),
                "w_kv": mla_w_kv_up[j].reshape(MLA_KV_LORA, MLA_HEADS * (MLA_NOPE + MLA_V)).astype(BF16),
            }
            w_out = mla_w_out[j].astype(BF16)
            q, kv, kpe, ckv, kraw = _mla_proj(xs[0], sc1[0], sh1[0], w, seg_lens[0], None)
            ckv_out.append(ckv.reshape(CTX_BATCH, CTX_SEQ, MLA_KV_LORA))
            kpe_out.append(kraw.reshape(CTX_BATCH, CTX_SEQ, MLA_ROPE))
            mixed.append(_mla_attn_ctx(q, kv, kpe))
            q, kv, kpe, _, _ = _mla_proj(xs[1], sc1[1], sh1[1], w, seg_lens[1], rope_mla)
            kv_ctx = _matmul_bf16(cache_mla_ckv[:, j].reshape(DEC_BATCH * PAST_LEN, MLA_KV_LORA).astype(BF16), w["w_kv"])
            kpe_ctx = jnp.pad(cache_mla_kpe[:, j].reshape(DEC_BATCH * PAST_LEN, MLA_ROPE),
                              ((0, 0), (0, LANES - MLA_ROPE))).astype(BF16)
            mixed.append(_mla_attn_lat(q, kv, kpe, kv_ctx, kpe_ctx))
        elif kind == 1:
            w = {"w_in": gm_w_in[j].astype(BF16), "v_gain": gm_v_gain[j][None, :],
                 "w_s": gm_w_s[j].astype(BF16), "b_s": gm_b_s[j][:, :, None]}
            w_out = gm_w_out[j].astype(BF16)
            for g in range(2):
                mixed.append(_gmlp_in(xs[g], sc1[g], sh1[g], w, seg_lens[g]))
        else:
            wq = swa_w_qkv[j]
            nq, nkv = SWA_HEADS * SWA_HEAD_DIM, SWA_KV_HEADS * SWA_HEAD_DIM
            w_ext = jnp.concatenate(
                [wq[:, :nq],
                 _swa_key_layout(wq[:, nq:nq + nkv].reshape(d, SWA_KV_HEADS, SWA_HEAD_DIM)),
                 _swa_value_layout(wq[:, nq + nkv:].reshape(d, SWA_KV_HEADS, SWA_HEAD_DIM))], axis=1).astype(BF16)
            w_out = swa_w_out[j].astype(BF16)
            sink = swa_sink[j][None, :]
            q, kd, vd, kraw, vraw = _swa_proj(xs[0], sc1[0], sh1[0], w_ext, seg_lens[0], None)
            k_out.append(kraw.reshape(CTX_BATCH, CTX_SEQ, SWA_KV_HEADS, SWA_HEAD_DIM))
            v_out.append(vraw.reshape(CTX_BATCH, CTX_SEQ, SWA_KV_HEADS, SWA_HEAD_DIM))
            mixed.append(_swa_attn_ctx(q, kd, vd, sink))
            q, kd, vd, _, _ = _swa_proj(xs[1], sc1[1], sh1[1], w_ext, seg_lens[1], rope_swa)
            rows = DEC_BATCH * PAST_LEN
            kd_ctx = _swa_key_layout(cache_swa_k[:, j].reshape(rows, SWA_KV_HEADS, SWA_HEAD_DIM)).astype(BF16)
            vd_ctx = _swa_value_layout(cache_swa_v[:, j].reshape(rows, SWA_KV_HEADS, SWA_HEAD_DIM)).astype(BF16)
            mixed.append(_swa_attn_lat(q, kd, vd, kd_ctx, vd_ctx, sink))

        x1, h2, routed = [], [], []
        for g in range(2):
            x1_g, h2_g, lg_g = _post(mixed[g], w_out, xs[g], g1[g], sc2[g], sh2[g], lng[0], lnb[0], router_pad, seg_lens[g])
            x1.append(x1_g)
            h2.append(h2_g)
            routed.append(_route(lg_g))
        idx_all = jnp.concatenate([routed[0][3], routed[1][3]], axis=2).reshape(-1)
        y = _moe_ffn(i, idx_all, h2[0], h2[1], routed[0][0], routed[1][0], w_gate_all, w_up_all, w_down_all)
        for g in range(2):
            roff = routed[g][2].reshape(-1)
            xs[g] = _combine(g, roff, y, routed[g][1], x1[g], g2[g], lng[1], lnb[1], seg_lens[g])

    return (xs[0].reshape(CTX_BATCH, CTX_SEQ, d), xs[1].reshape(DEC_BATCH, DEC_SEQ, d),
            jnp.stack(ckv_out, axis=1), jnp.stack(kpe_out, axis=1),
            jnp.stack(k_out, axis=1), jnp.stack(v_out, axis=1))
```

```python
import functools

import jax
import jax.numpy as jnp
from jax import lax
from jax.experimental import pallas as pl
from jax.experimental.pallas import tpu as pltpu

D_MODEL = 1024
N_TOK = 4096
DEPTH = 4
DEC_BATCH = 4
DEC_SEQ = 1024
PAST_LEN = 256
CTX_BATCH = 16
CTX_SEQ = 256
GRID_W = 64
ALPHA = (2.0 * DEPTH) ** 0.25
MLA_HEADS = 8
MLA_Q_LORA = 384
MLA_KV_LORA = 256
MLA_NOPE = 128
MLA_ROPE = 64
MLA_V = 128
MLA_QK_PAD = 256
GM_FF = 6 * D_MODEL
GM_HALF = GM_FF // 2
GM_GROUPS = 8
GM_GW = GM_HALF // GM_GROUPS
CHUNK = 128
SWA_HEADS = 16
SWA_KV_HEADS = 4
SWA_HEAD_DIM = 64
WINDOW = 128
BAND = 128
N_EXPERTS = 16
EXPERT_FF = 2 * D_MODEL
CAP = 2 * N_TOK // N_EXPERTS
ROPE_BASE = 10000.0
NEG_INF = -1e30
EPS = 1e-6
F32 = jnp.float32
BF16 = jnp.bfloat16
I32 = jnp.int32
U32 = jnp.uint32
HIGHEST = lax.Precision.HIGHEST

LANES = 128
SUBLANES = 8
BF16_ROWS = 16
ROW_TILES = D_MODEL // LANES
VMEM_LIMIT = 56 * 1024 * 1024
TM = 256
FF_TILE = 1024
TOK_BLK = 256
MOD_TN = 1536

F_SLOT_HI, F_SLOT_LO, F_G0, F_G1, F_G2 = 0, 16, 32, 48, 64
F_T_HI, F_T_LO = 80, 81


def _params(sem, vmem=VMEM_LIMIT):
    return pltpu.CompilerParams(dimension_semantics=sem, vmem_limit_bytes=vmem)


def _dot(a, b):
    return jnp.dot(a, b, preferred_element_type=F32)


def _dot_nt(a, b):
    return lax.dot_general(a, b, (((1,), (1,)), ((), ())), preferred_element_type=F32)


def _sigmoid(x):
    return 1.0 / (1.0 + jnp.exp(-x))


def _gelu_tanh(x):
    return 0.5 * x * (1.0 + jnp.tanh(0.7978845608028654 * (x + 0.044715 * (x * x * x))))


def _layer_norm(t, gain, bias):
    mu = jnp.mean(t, axis=-1, keepdims=True)
    d = t - mu
    var = jnp.mean(d * d, axis=-1, keepdims=True)
    return d * lax.rsqrt(var + EPS) * gain + bias


def _rms_norm(t, gain):
    return t * lax.rsqrt(jnp.mean(t * t, axis=-1, keepdims=True) + EPS) * gain


def _rope128(x, cos, sin):
    lane = lax.broadcasted_iota(I32, x.shape, 1)
    first = (lane % 32) < 16
    swapped = jnp.where(first, -pltpu.roll(x, 112, 1), pltpu.roll(x, 16, 1))
    return x * cos + swapped * sin


def _mod_kernel(cond_ref, w_ref, b_ref, o_ref):
    c = cond_ref[...]
    s = c * _sigmoid(c)
    o_ref[0] = jnp.dot(s, w_ref[0], precision=HIGHEST, preferred_element_type=F32) + b_ref[0]


def _modulation_all(cond8, mod_w, mod_b):
    return pl.pallas_call(
        _mod_kernel,
        grid=(DEPTH, 6 * D_MODEL // MOD_TN),
        in_specs=[pl.BlockSpec((8, D_MODEL), lambda i, n: (0, 0)),
                  pl.BlockSpec((1, D_MODEL, MOD_TN), lambda i, n: (i, 0, n)),
                  pl.BlockSpec((1, 1, MOD_TN), lambda i, n: (i, 0, n))],
        out_specs=pl.BlockSpec((1, 8, MOD_TN), lambda i, n: (i, 0, n)),
        out_shape=jax.ShapeDtypeStruct((DEPTH, 8, 6 * D_MODEL), F32),
        compiler_params=_params(("arbitrary", "arbitrary")),
        name="modulation",
    )(cond8, mod_w, mod_b.reshape(DEPTH, 1, 6 * D_MODEL))


def _seg_spec(seg_len):
    return pl.BlockSpec((1, 1, D_MODEL), lambda r: (r * TM // seg_len, 0, 0))


def _row_spec(width):
    return pl.BlockSpec((TM, width), lambda r: (r, 0))


def _full_spec(shape):
    nd = len(shape)
    return pl.BlockSpec(shape, lambda r: (0,) * nd)


def _mla_proj_kernel(rope, x_ref, sc_ref, sh_ref, win_ref, qg_ref, kvg_ref, wq_ref, wkv_ref, *rest):
    if rope:
        cos_ref, sin_ref, q_ref, kv_ref, kpe_ref, ckv_ref, kraw_ref = rest
    else:
        q_ref, kv_ref, kpe_ref, ckv_ref, kraw_ref = rest
    h = (x_ref[...] * (1.0 + sc_ref[0]) + sh_ref[0]).astype(BF16)
    z = _dot(h, win_ref[...])
    cq = _rms_norm(z[:, :MLA_Q_LORA], qg_ref[...])
    ckv = _rms_norm(z[:, MLA_Q_LORA:MLA_Q_LORA + MLA_KV_LORA], kvg_ref[...])
    kp = z[:, MLA_Q_LORA + MLA_KV_LORA:]
    ckv_ref[...] = ckv
    kraw_ref[...] = kp[:, :MLA_ROPE]
    q = _dot(cq.astype(BF16), wq_ref[...])
    kv_ref[...] = _dot(ckv.astype(BF16), wkv_ref[...]).astype(BF16)
    if rope:
        cos, sin = cos_ref[...], sin_ref[...]
        for hd in range(MLA_HEADS):
            a = hd * MLA_QK_PAD
            q_ref[:, a:a + LANES] = q[:, a:a + LANES].astype(BF16)
            q_ref[:, a + LANES:a + 2 * LANES] = _rope128(q[:, a + LANES:a + 2 * LANES], cos, sin).astype(BF16)
        kp = _rope128(kp, cos, sin)
    else:
        q_ref[...] = q.astype(BF16)
    kpe_ref[...] = kp.astype(BF16)


def _mla_proj(x, sc, sh, w, seg_len, rope_tabs):
    rope = rope_tabs is not None
    in_specs = [_row_spec(D_MODEL), _seg_spec(seg_len), _seg_spec(seg_len),
                _full_spec(w["w_in"].shape), _full_spec((1, MLA_Q_LORA)), _full_spec((1, MLA_KV_LORA)),
                _full_spec(w["w_q"].shape), _full_spec(w["w_kv"].shape)]
    args = [x, sc, sh, w["w_in"], w["q_gain"], w["kv_gain"], w["w_q"], w["w_kv"]]
    if rope:
        nblk = DEC_SEQ // TM
        tab_spec = pl.BlockSpec((TM, LANES), lambda r: (r % nblk, 0))
        in_specs += [tab_spec, tab_spec]
        args += list(rope_tabs)
    wq = MLA_HEADS * MLA_QK_PAD
    return pl.pallas_call(
        functools.partial(_mla_proj_kernel, rope),
        grid=(N_TOK // TM,),
        in_specs=in_specs,
        out_specs=[_row_spec(wq), _row_spec(wq), _row_spec(LANES), _row_spec(MLA_KV_LORA), _row_spec(MLA_ROPE)],
        out_shape=[jax.ShapeDtypeStruct((N_TOK, wq), BF16), jax.ShapeDtypeStruct((N_TOK, wq), BF16),
                   jax.ShapeDtypeStruct((N_TOK, LANES), BF16), jax.ShapeDtypeStruct((N_TOK, MLA_KV_LORA), F32),
                   jax.ShapeDtypeStruct((N_TOK, MLA_ROPE), F32)],
        compiler_params=_params(("arbitrary",)),
        name="mla_proj",
    )(*args)


def _mm_kernel(a_ref, w_ref, o_ref):
    o_ref[...] = _dot(a_ref[...], w_ref[...]).astype(o_ref.dtype)


def _matmul_bf16(a, w):
    m, k = a.shape
    n = w.shape[1]
    return pl.pallas_call(
        _mm_kernel,
        grid=(m // TM,),
        in_specs=[pl.BlockSpec((TM, k), lambda r: (r, 0)), _full_spec(w.shape)],
        out_specs=pl.BlockSpec((TM, n), lambda r: (r, 0)),
        out_shape=jax.ShapeDtypeStruct((m, n), BF16),
        compiler_params=_params(("arbitrary",)),
        name="matmul",
    )(a, w)


MLA_SCALE = (MLA_NOPE + MLA_ROPE) ** -0.5


def _mla_attn_ctx_kernel(q_ref, kv_ref, kpe_ref, o_ref):
    kpe = kpe_ref[...]
    for hd in range(MLA_HEADS):
        a = hd * MLA_QK_PAD
        qh = q_ref[:, a:a + MLA_QK_PAD]
        kh = jnp.concatenate([kv_ref[:, a:a + LANES], kpe], axis=1)
        vh = kv_ref[:, a + LANES:a + 2 * LANES]
        s = _dot_nt(qh, kh) * MLA_SCALE
        p = jnp.exp(s - jnp.max(s, axis=-1, keepdims=True))
        l = jnp.sum(p, axis=-1, keepdims=True)
        o_ref[:, hd * MLA_V:(hd + 1) * MLA_V] = (_dot(p.astype(BF16), vh) / l).astype(BF16)


def _mla_attn_ctx(q, kv, kpe):
    wq = MLA_HEADS * MLA_QK_PAD
    blk = lambda w: pl.BlockSpec((CTX_SEQ, w), lambda b: (b, 0))
    return pl.pallas_call(
        _mla_attn_ctx_kernel,
        grid=(CTX_BATCH,),
        in_specs=[blk(wq), blk(wq), blk(LANES)],
        out_specs=blk(MLA_HEADS * MLA_V),
        out_shape=jax.ShapeDtypeStruct((N_TOK, MLA_HEADS * MLA_V), BF16),
        compiler_params=_params(("arbitrary",)),
        name="mla_attn_ctx",
    )(q, kv, kpe)


def _mla_attn_lat_kernel(q_ref, kvl_ref, kpel_ref, kvc_ref, kpec_ref, o_ref):
    kpel = kpel_ref[...]
    kpec = kpec_ref[...]
    for hd in range(MLA_HEADS):
        a = hd * MLA_QK_PAD
        qh = q_ref[:, a:a + MLA_QK_PAD]
        kc = jnp.concatenate([kvc_ref[:, a:a + LANES], kpec], axis=1)
        kl = jnp.concatenate([kvl_ref[:, a:a + LANES], kpel], axis=1)
        sc = _dot_nt(qh, kc) * MLA_SCALE
        sl = _dot_nt(qh, kl) * MLA_SCALE
        m = jnp.maximum(jnp.max(sc, axis=-1, keepdims=True), jnp.max(sl, axis=-1, keepdims=True))
        pc = jnp.exp(sc - m)
        plat = jnp.exp(sl - m)
        l = jnp.sum(pc, axis=-1, keepdims=True) + jnp.sum(plat, axis=-1, keepdims=True)
        o = _dot(pc.astype(BF16), kvc_ref[:, a + LANES:a + 2 * LANES]) \
            + _dot(plat.astype(BF16), kvl_ref[:, a + LANES:a + 2 * LANES])
        o_ref[:, hd * MLA_V:(hd + 1) * MLA_V] = (o / l).astype(BF16)


def _mla_attn_lat(q, kv, kpe, kv_ctx, kpe_ctx):
    wq = MLA_HEADS * MLA_QK_PAD
    nq = DEC_SEQ // TM
    return pl.pallas_call(
        _mla_attn_lat_kernel,
        grid=(DEC_BATCH, nq),
        in_specs=[pl.BlockSpec((TM, wq), lambda b, i: (b * nq + i, 0)),
                  pl.BlockSpec((DEC_SEQ, wq), lambda b, i: (b, 0)),
                  pl.BlockSpec((DEC_SEQ, LANES), lambda b, i: (b, 0)),
                  pl.BlockSpec((PAST_LEN, wq), lambda b, i: (b, 0)),
                  pl.BlockSpec((PAST_LEN, LANES), lambda b, i: (b, 0))],
        out_specs=pl.BlockSpec((TM, MLA_HEADS * MLA_V), lambda b, i: (b * nq + i, 0)),
        out_shape=jax.ShapeDtypeStruct((N_TOK, MLA_HEADS * MLA_V), BF16),
        compiler_params=_params(("arbitrary", "arbitrary")),
        name="mla_attn_lat",
    )(q, kv, kpe, kv_ctx, kpe_ctx)


def _gmlp_kernel(x_ref, sc_ref, sh_ref, win_ref, vg_ref, ws_ref, bs_ref, p_ref):
    h = (x_ref[...] * (1.0 + sc_ref[0]) + sh_ref[0]).astype(BF16)
    zv = _gelu_tanh(_dot(h, win_ref[:, GM_HALF:]))
    mu = jnp.mean(zv, axis=-1, keepdims=True)
    d = zv - mu
    var = jnp.mean(d * d, axis=-1, keepdims=True)
    vn = (d * lax.rsqrt(var + EPS) * vg_ref[...]).astype(BF16)
    for g in range(GM_GROUPS):
        c0 = g * GM_GW
        u = _gelu_tanh(_dot(h, win_ref[:, c0:c0 + GM_GW]))
        mixed = jnp.concatenate(
            [_dot(ws_ref[g], vn[k * CHUNK:(k + 1) * CHUNK, c0:c0 + GM_GW]) for k in range(TM // CHUNK)], axis=0)
        bias = jnp.concatenate([bs_ref[g]] * (TM // CHUNK), axis=0)
        p_ref[:, c0:c0 + GM_GW] = (u * (mixed + bias)).astype(BF16)


def _gmlp_in(x, sc, sh, w, seg_len):
    return pl.pallas_call(
        _gmlp_kernel,
        grid=(N_TOK // TM,),
        in_specs=[_row_spec(D_MODEL), _seg_spec(seg_len), _seg_spec(seg_len),
                  _full_spec((D_MODEL, GM_FF)), _full_spec((1, GM_HALF)),
                  _full_spec((GM_GROUPS, CHUNK, CHUNK)), _full_spec((GM_GROUPS, CHUNK, 1))],
        out_specs=_row_spec(GM_HALF),
        out_shape=jax.ShapeDtypeStruct((N_TOK, GM_HALF), BF16),
        compiler_params=_params(("arbitrary",)),
        name="gmlp_in",
    )(x, sc, sh, w["w_in"], w["v_gain"], w["w_s"], w["b_s"])


SWA_SCALE = SWA_HEAD_DIM ** -0.5
SWA_QW = SWA_HEADS * SWA_HEAD_DIM
SWA_KW = SWA_KV_HEADS * 2 * LANES
SWA_VW = SWA_KV_HEADS * LANES
SWA_GROUP = SWA_KV_HEADS


def _swa_proj_kernel(rope, x_ref, sc_ref, sh_ref, w_ref, *rest):
    if rope:
        cos_ref, sin_ref, q_ref, kd_ref, vd_ref, kraw_ref, vraw_ref = rest
    else:
        q_ref, kd_ref, vd_ref, kraw_ref, vraw_ref = rest
    h = (x_ref[...] * (1.0 + sc_ref[0]) + sh_ref[0]).astype(BF16)
    z = _dot(h, w_ref[...])
    lane = lax.broadcasted_iota(I32, (TM, LANES), 1)
    low = lane < SWA_HEAD_DIM
    for m in range(SWA_KV_HEADS // 2):
        k0 = z[:, SWA_QW + (2 * m) * 2 * LANES:SWA_QW + (2 * m) * 2 * LANES + LANES]
        k1 = z[:, SWA_QW + (2 * m + 1) * 2 * LANES + LANES:SWA_QW + (2 * m + 2) * 2 * LANES]
        kraw_ref[:, m * LANES:(m + 1) * LANES] = jnp.where(low, k0, k1)
        v0 = z[:, SWA_QW + SWA_KW + (2 * m) * LANES:SWA_QW + SWA_KW + (2 * m + 1) * LANES]
        v1 = z[:, SWA_QW + SWA_KW + (2 * m + 1) * LANES:SWA_QW + SWA_KW + (2 * m + 2) * LANES]
        vraw_ref[:, m * LANES:(m + 1) * LANES] = jnp.where(low, v0, v1)
    vd_ref[...] = z[:, SWA_QW + SWA_KW:].astype(BF16)
    if rope:
        cos, sin = cos_ref[...], sin_ref[...]
        for j in range(SWA_QW // LANES):
            q_ref[:, j * LANES:(j + 1) * LANES] = _rope128(z[:, j * LANES:(j + 1) * LANES], cos, sin).astype(BF16)
        for j in range(SWA_KW // LANES):
            a = SWA_QW + j * LANES
            kd_ref[:, j * LANES:(j + 1) * LANES] = _rope128(z[:, a:a + LANES], cos, sin).astype(BF16)
    else:
        q_ref[...] = z[:, :SWA_QW].astype(BF16)
        kd_ref[...] = z[:, SWA_QW:SWA_QW + SWA_KW].astype(BF16)


def _swa_proj(x, sc, sh, w_ext, seg_len, rope_tabs):
    rope = rope_tabs is not None
    in_specs = [_row_spec(D_MODEL), _seg_spec(seg_len), _seg_spec(seg_len), _full_spec(w_ext.shape)]
    args = [x, sc, sh, w_ext]
    if rope:
        nblk = DEC_SEQ // TM
        tab_spec = pl.BlockSpec((TM, LANES), lambda r: (r % nblk, 0))
        in_specs += [tab_spec, tab_spec]
        args += list(rope_tabs)
    kvw = SWA_KV_HEADS * SWA_HEAD_DIM
    return pl.pallas_call(
        functools.partial(_swa_proj_kernel, rope),
        grid=(N_TOK // TM,),
        in_specs=in_specs,
        out_specs=[_row_spec(SWA_QW), _row_spec(SWA_KW), _row_spec(SWA_VW), _row_spec(kvw), _row_spec(kvw)],
        out_shape=[jax.ShapeDtypeStruct((N_TOK, SWA_QW), BF16), jax.ShapeDtypeStruct((N_TOK, SWA_KW), BF16),
                   jax.ShapeDtypeStruct((N_TOK, SWA_VW), BF16), jax.ShapeDtypeStruct((N_TOK, kvw), F32),
                   jax.ShapeDtypeStruct((N_TOK, kvw), F32)],
        compiler_params=_params(("arbitrary",)),
        name="swa_proj",
    )(*args)


def _swa_heads(q_ref, sink_ref, o_ref, score_fn, value_fn):
    rows = q_ref.shape[0]
    lane = lax.broadcasted_iota(I32, (rows, LANES), 1)
    low = lane < SWA_HEAD_DIM
    for hk0 in range(0, SWA_KV_HEADS, SWA_GROUP):
        stacked, sinks = None, []
        for hk in range(hk0, hk0 + SWA_GROUP):
            c0 = hk * 2 * LANES
            q2 = jnp.concatenate([q_ref[:, c0:c0 + LANES], q_ref[:, c0 + LANES:c0 + 2 * LANES]], axis=0)
            for par in range(2):
                for pr in range(2):
                    hd = hk * 4 + pr * 2 + par
                    sinks.append(jnp.broadcast_to(sink_ref[:, hd:hd + 1], (rows, 1)))
                blocks = score_fn(hk, par, q2)
                if stacked is None:
                    stacked = [[] for _ in blocks]
                for kind, s in zip(stacked, blocks):
                    kind.append(s)
        scores = [jnp.concatenate(kind, axis=0) for kind in stacked]
        sk = jnp.concatenate(sinks, axis=0)
        m = sk
        for s in scores:
            m = jnp.maximum(m, jnp.max(s, axis=-1, keepdims=True))
        probs = [jnp.exp(s - m) for s in scores]
        l = jnp.exp(sk - m)
        for p in probs:
            l = l + jnp.sum(p, axis=-1, keepdims=True)
        inv = 1.0 / l
        n = 0
        for hk in range(hk0, hk0 + SWA_GROUP):
            res = []
            for par in range(2):
                blk = slice(n * 2 * rows, (n + 1) * 2 * rows)
                res.append(value_fn(hk, [p[blk] for p in probs]) * inv[blk])
                n += 1
            for pr in range(2):
                c0 = hk * 2 * LANES + pr * LANES
                part = slice(pr * rows, (pr + 1) * rows)
                o_ref[:, c0:c0 + LANES] = jnp.where(low, res[0][part], res[1][part]).astype(BF16)


def _swa_attn_ctx_kernel(q_ref, kd_ref, vd_ref, sink_ref, o_ref):
    def score_fn(hk, par, qp):
        a = hk * 2 * LANES + par * LANES
        return [_dot_nt(qp, kd_ref[:, a:a + LANES]) * SWA_SCALE]

    def value_fn(hk, probs):
        return _dot(probs[0].astype(BF16), vd_ref[:, hk * LANES:(hk + 1) * LANES])

    _swa_heads(q_ref, sink_ref, o_ref, score_fn, value_fn)


def _swa_attn_ctx(q, kd, vd, sink):
    blk = lambda w: pl.BlockSpec((CTX_SEQ, w), lambda b: (b, 0))
    return pl.pallas_call(
        _swa_attn_ctx_kernel,
        grid=(CTX_BATCH,),
        in_specs=[blk(SWA_QW), blk(SWA_KW), blk(SWA_VW), pl.BlockSpec((1, SWA_HEADS), lambda b: (0, 0))],
        out_specs=blk(SWA_QW),
        out_shape=jax.ShapeDtypeStruct((N_TOK, SWA_QW), BF16),
        compiler_params=_params(("arbitrary",)),
        name="swa_attn_ctx",
    )(q, kd, vd, sink)


SWA_WIN = 3 * BAND


def _swa_attn_lat_kernel(q_ref, kdl_ref, vdl_ref, kdc_ref, vdc_ref, sink_ref, o_ref):
    nb = pl.program_id(1)
    start = pl.multiple_of(jnp.clip((nb - 1) * BAND, 0, DEC_SEQ - SWA_WIN), BAND)
    qpos = nb * BAND + (lax.broadcasted_iota(I32, (2 * BAND, SWA_WIN), 0) & (BAND - 1))
    kpos = start + lax.broadcasted_iota(I32, (2 * BAND, SWA_WIN), 1)
    valid = jnp.abs(qpos - kpos) <= WINDOW

    def score_fn(hk, par, qp):
        a = hk * 2 * LANES + par * LANES
        kl = kdl_ref[pl.ds(start, SWA_WIN), a:a + LANES]
        sl = jnp.where(valid, _dot_nt(qp, kl) * SWA_SCALE, NEG_INF)
        sc = _dot_nt(qp, kdc_ref[:, a:a + LANES]) * SWA_SCALE
        return [sl, sc]

    def value_fn(hk, probs):
        vl = vdl_ref[pl.ds(start, SWA_WIN), hk * LANES:(hk + 1) * LANES]
        return _dot(probs[0].astype(BF16), vl) + _dot(probs[1].astype(BF16), vdc_ref[:, hk * LANES:(hk + 1) * LANES])

    _swa_heads(q_ref, sink_ref, o_ref, score_fn, value_fn)


def _swa_attn_lat(q, kd, vd, kd_ctx, vd_ctx, sink):
    nbands = DEC_SEQ // BAND
    return pl.pallas_call(
        _swa_attn_lat_kernel,
        grid=(DEC_BATCH, nbands),
        in_specs=[pl.BlockSpec((BAND, SWA_QW), lambda b, i: (b * nbands + i, 0)),
                  pl.BlockSpec((DEC_SEQ, SWA_KW), lambda b, i: (b, 0)),
                  pl.BlockSpec((DEC_SEQ, SWA_VW), lambda b, i: (b, 0)),
                  pl.BlockSpec((PAST_LEN, SWA_KW), lambda b, i: (b, 0)),
                  pl.BlockSpec((PAST_LEN, SWA_VW), lambda b, i: (b, 0)),
                  pl.BlockSpec((1, SWA_HEADS), lambda b, i: (0, 0))],
        out_specs=pl.BlockSpec((BAND, SWA_QW), lambda b, i: (b * nbands + i, 0)),
        out_shape=jax.ShapeDtypeStruct((N_TOK, SWA_QW), BF16),
        compiler_params=_params(("arbitrary", "arbitrary")),
        name="swa_attn_lat",
    )(q, kd, vd, kd_ctx, vd_ctx, sink)


def _post_kernel(a_ref, w_ref, x_ref, g1_ref, sc2_ref, sh2_ref, lng_ref, lnb_ref, rt_ref, x1_ref, h2_ref, lg_ref):
    y = _dot(a_ref[...], w_ref[...])
    x1 = _layer_norm(ALPHA * x_ref[...] + g1_ref[0] * y, lng_ref[...], lnb_ref[...])
    x1_ref[...] = x1
    h2 = x1 * (1.0 + sc2_ref[0]) + sh2_ref[0]
    for j in range(ROW_TILES):
        h2_ref[pl.ds(j, TM, stride=ROW_TILES), :] = h2[:, j * LANES:(j + 1) * LANES]
    rt = rt_ref[...]
    h_hi = h2.astype(BF16)
    h_lo = (h2 - h_hi.astype(F32)).astype(BF16)
    r_hi = rt.astype(BF16)
    r_lo = (rt - r_hi.astype(F32)).astype(BF16)
    lg_ref[...] = _dot(h_hi, r_hi) + (_dot(h_lo, r_hi) + _dot(h_hi, r_lo))


def _post(a, w_out, x, g1, sc2, sh2, lng, lnb, router_pad, seg_len):
    k = a.shape[1]
    return pl.pallas_call(
        _post_kernel,
        grid=(N_TOK // TM,),
        in_specs=[_row_spec(k), _full_spec((k, D_MODEL)), _row_spec(D_MODEL),
                  _seg_spec(seg_len), _seg_spec(seg_len), _seg_spec(seg_len),
                  _full_spec((1, D_MODEL)), _full_spec((1, D_MODEL)), _full_spec((D_MODEL, LANES))],
        out_specs=[_row_spec(D_MODEL), pl.BlockSpec((TM * ROW_TILES, LANES), lambda r: (r, 0)), _row_spec(LANES)],
        out_shape=[jax.ShapeDtypeStruct((N_TOK, D_MODEL), F32), jax.ShapeDtypeStruct((N_TOK * ROW_TILES, LANES), F32),
                   jax.ShapeDtypeStruct((N_TOK, LANES), F32)],
        compiler_params=_params(("arbitrary",)),
        name="mixer_out",
    )(a, w_out, x, g1, sc2, sh2, lng, lnb, router_pad)


CUM_BLK = 256
REFINE_STEPS = 16
F32_MIN_NORMAL = 1.1754943508222875e-38


def _excl_cumsum_lanes(a, upper):
    outs = []
    carry = jnp.zeros((a.shape[0], 1), F32)
    for b in range(N_TOK // CUM_BLK):
        blk = a[:, b * CUM_BLK:(b + 1) * CUM_BLK]
        outs.append(_dot(blk.astype(BF16), upper) + carry)
        carry = carry + jnp.sum(blk, axis=1, keepdims=True)
    return jnp.concatenate(outs, axis=1)


def _route_kernel(lg_ref, r_ref, tok_ref, roff_ref, idx_ref, rank_scr, w_scr):
    e_rows = N_EXPERTS
    lt = lg_ref[...].T[:e_rows]
    ex = jnp.exp(lt - jnp.max(lt, axis=0, keepdims=True))
    aff = ex / jnp.sum(ex, axis=0, keepdims=True)
    def count_ge(thr):
        return jnp.sum(jnp.where(aff >= thr, 1.0, 0.0), axis=1, keepdims=True)

    cur = jnp.zeros((e_rows, 1), I32)
    for b in range(30, -1, -1):
        cand = cur | jnp.int32(1 << b)
        cur = jnp.where(count_ge(lax.bitcast_convert_type(cand, F32)) >= float(CAP), cand, cur)
    lo = lax.bitcast_convert_type(cur, F32)
    hi = jnp.maximum(lax.bitcast_convert_type(cur + 1, F32), F32_MIN_NORMAL)
    for _ in range(REFINE_STEPS):
        w = hi - lo
        t1, t2, t3 = lo + 0.25 * w, lo + 0.5 * w, lo + 0.75 * w
        ok1, ok2, ok3 = (count_ge(t) >= float(CAP) for t in (t1, t2, t3))
        lo, hi = (jnp.where(ok3, t3, jnp.where(ok2, t2, jnp.where(ok1, t1, lo))),
                  jnp.where(ok1, jnp.where(ok2, jnp.where(ok3, hi, t3), t2), t1))
    gt = aff >= hi
    eq = (aff >= lo) & (aff < hi)
    need = float(CAP) - jnp.sum(jnp.where(gt, 1.0, 0.0), axis=1, keepdims=True)

    ri = lax.broadcasted_iota(I32, (CUM_BLK, CUM_BLK), 0)
    ci = lax.broadcasted_iota(I32, (CUM_BLK, CUM_BLK), 1)
    upper = jnp.where(ri < ci, 1.0, 0.0).astype(BF16)

    tie_rank = _excl_cumsum_lanes(jnp.where(eq, 1.0, 0.0), upper)
    sel = gt | (eq & (tie_rank < need))
    msk = jnp.where(sel, 1.0, 0.0)

    rank = _excl_cumsum_lanes(msk, upper)
    rank_scr[...] = jnp.where(sel, rank, -1.0)

    inv = 1.0 / LANES
    slot_no = jnp.where(sel, rank, -float(LANES))
    slot_hi = jnp.floor(slot_no * inv)
    g0 = aff.astype(BF16).astype(F32)
    r1 = aff - g0
    g1 = r1.astype(BF16).astype(F32)
    g2 = (r1 - g1).astype(BF16).astype(F32)
    tt = lax.broadcasted_iota(I32, (1, N_TOK), 1).astype(F32)
    t_hi = jnp.floor(tt * inv)
    misc = jnp.concatenate([t_hi, tt - t_hi * LANES, jnp.zeros((6, N_TOK), F32)], axis=0)
    feat = jnp.concatenate([slot_hi, slot_no - slot_hi * LANES, g0, g1, g2, misc,
                            jnp.zeros((LANES - 5 * e_rows - 8, N_TOK), F32)], axis=0)
    table = feat.T
    tok_ref[...] = table
    w_scr[...] = table.astype(BF16)

    nblk = N_TOK // TOK_BLK
    bt = lax.broadcasted_iota(I32, (N_TOK, LANES), 0) // TOK_BLK
    bb = lax.broadcasted_iota(I32, (N_TOK, LANES), 1)
    before = jnp.where((bt < bb) & (bb <= nblk), 1.0, 0.0).astype(BF16)
    roff_ref[...] = _dot(msk.astype(BF16), before).astype(I32)

    sub = lax.broadcasted_iota(I32, (8, LANES), 0)
    ln8 = lax.broadcasted_iota(I32, (8, LANES), 1)
    rb_rows = 128

    def per_expert(e, carry):
        rk = rank_scr[pl.ds(e, 1), :]
        want = jnp.where(sub < 5, sub * e_rows + e, sub + (F_T_HI - 5))
        pick = jnp.where(ln8 == want, 1.0, 0.0).astype(BF16)
        for rb in range(CAP // rb_rows):
            slot = (rb * rb_rows + lax.broadcasted_iota(I32, (rb_rows, 1), 0)).astype(F32)
            onehot = jnp.where(rk == slot, 1.0, 0.0).astype(BF16)
            rows = _dot(onehot, w_scr[...])
            r_ref[pl.ds(e, 1), rb * rb_rows:(rb + 1) * rb_rows, :] = rows[None]
            got = _dot_nt(pick, rows.astype(BF16))
            idx_ref[pl.ds(e, 1), :, rb * rb_rows:(rb + 1) * rb_rows] = (got[5:6] * LANES + got[6:7]).astype(I32)[None]
        return carry

    lax.fori_loop(0, e_rows, per_expert, 0)


def _route(logits):
    return pl.pallas_call(
        _route_kernel,
        grid=(1,),
        in_specs=[pl.BlockSpec((N_TOK, LANES), lambda i: (0, 0))],
        out_specs=[pl.BlockSpec((N_EXPERTS, CAP, LANES), lambda i: (0, 0, 0)),
                   pl.BlockSpec((N_TOK, LANES), lambda i: (0, 0)),
                   pl.BlockSpec((N_EXPERTS, LANES), lambda i: (0, 0)),
                   pl.BlockSpec((N_EXPERTS, 1, CAP), lambda i: (0, 0, 0))],
        out_shape=[jax.ShapeDtypeStruct((N_EXPERTS, CAP, LANES), F32),
                   jax.ShapeDtypeStruct((N_TOK, LANES), F32),
                   jax.ShapeDtypeStruct((N_EXPERTS, LANES), I32),
                   jax.ShapeDtypeStruct((N_EXPERTS, 1, CAP), I32)],
        scratch_shapes=[pltpu.VMEM((N_EXPERTS, N_TOK), F32), pltpu.VMEM((N_TOK, LANES), BF16)],
        compiler_params=_params(("arbitrary",)),
        name="moe_route",
    )(logits)


ROWS_E = 2 * CAP
N_FT = EXPERT_FF // FF_TILE
GATHER_UNROLL = 8
PIECE = 256
N_PIECE = FF_TILE // PIECE


def _tile(row):
    return pl.ds(pl.multiple_of(row * ROW_TILES, ROW_TILES), ROW_TILES)


def _ffn_kernel(idx_s, hc_hbm, hl_hbm, rc_ref, rl_ref, wg_ref, wu_ref, wd_ref, y_ref, xe, xb, acc, gsem):
    e = pl.program_id(0)
    f = pl.program_id(1)

    def gather_rows(ee):
        base = ee * ROWS_E

        def body(r, carry):
            tc = idx_s[base + r]
            pltpu.make_async_copy(hc_hbm.at[_tile(tc)], xe.at[_tile(r)], gsem.at[0]).start(priority=0)
            tl = idx_s[base + CAP + r]
            pltpu.make_async_copy(hl_hbm.at[_tile(tl)], xe.at[_tile(CAP + r)], gsem.at[0]).start(priority=1)
            return carry

        lax.fori_loop(0, CAP, body, 0, unroll=GATHER_UNROLL)

    @pl.when(f == 0)
    def _():
        @pl.when(e == 0)
        def _():
            gather_rows(0)

        pltpu.make_async_copy(xe, xe, gsem.at[0]).wait()
        for j in range(ROW_TILES):
            xb[:, j * LANES:(j + 1) * LANES] = xe[pl.ds(j, ROWS_E, stride=ROW_TILES), :].astype(BF16)

        @pl.when(e + 1 < N_EXPERTS)
        def _():
            gather_rows(e + 1)

        acc[...] = jnp.zeros_like(acc)

    x = xb[...]
    for p in range(N_PIECE):
        cols = slice(p * PIECE, (p + 1) * PIECE)
        g = _dot(x, wg_ref[0, :, cols].astype(BF16))
        u = _dot(x, wu_ref[0, :, cols].astype(BF16))
        hid = (g * _sigmoid(g) * u).astype(BF16)
        acc[...] += _dot(hid, wd_ref[0, cols, :].astype(BF16))

    @pl.when(f == N_FT - 1)
    def _():
        lane = lax.broadcasted_iota(I32, (CAP, LANES), 1)
        mine = (lane == F_G0 + e) | (lane == F_G1 + e) | (lane == F_G2 + e)
        gate = jnp.concatenate(
            [jnp.sum(jnp.where(mine, rc_ref[0], 0.0), axis=1, keepdims=True),
             jnp.sum(jnp.where(mine, rl_ref[0], 0.0), axis=1, keepdims=True)], axis=0)
        y_ref[0] = (acc[...] * gate).astype(BF16)


def _moe_ffn(layer, idx_all, h_ctx, h_lat, r_ctx, r_lat, w_gate, w_up, w_down):
    any_spec = pl.BlockSpec(memory_space=pl.ANY)
    e0 = layer * N_EXPERTS
    grid_spec = pltpu.PrefetchScalarGridSpec(
        num_scalar_prefetch=1,
        grid=(N_EXPERTS, N_FT),
        in_specs=[any_spec, any_spec,
                  pl.BlockSpec((1, CAP, LANES), lambda e, f, *_: (e, 0, 0)),
                  pl.BlockSpec((1, CAP, LANES), lambda e, f, *_: (e, 0, 0)),
                  pl.BlockSpec((1, D_MODEL, FF_TILE), lambda e, f, *_: (e0 + e, 0, f)),
                  pl.BlockSpec((1, D_MODEL, FF_TILE), lambda e, f, *_: (e0 + e, 0, f)),
                  pl.BlockSpec((1, FF_TILE, D_MODEL), lambda e, f, *_: (e0 + e, f, 0))],
        out_specs=pl.BlockSpec((1, ROWS_E, D_MODEL), lambda e, f, *_: (e, 0, 0)),
        scratch_shapes=[pltpu.VMEM((ROWS_E * ROW_TILES, LANES), F32), pltpu.VMEM((ROWS_E, D_MODEL), BF16),
                        pltpu.VMEM((ROWS_E, D_MODEL), F32), pltpu.SemaphoreType.DMA((1,))],
    )
    return pl.pallas_call(
        _ffn_kernel,
        grid_spec=grid_spec,
        out_shape=jax.ShapeDtypeStruct((N_EXPERTS, ROWS_E, D_MODEL), BF16),
        compiler_params=_params(("arbitrary", "arbitrary")),
        name="moe_ffn",
    )(idx_all, h_ctx, h_lat, r_ctx, r_lat, w_gate, w_up, w_down)


WIN_E = 64
EXPERTS_PER_DOT = 4
N_TBLK = N_TOK // TOK_BLK
ROFF_STRIDE = LANES


def _combine_kernel(group, roff_s, y_hbm, tok_ref, x_ref, g2_ref, lng_ref, lnb_ref, o_ref, ywin, yext, acc, wsem, esem):
    b = pl.program_id(0)

    def win_start(e, bb):
        first = roff_s[e * ROFF_STRIDE + bb]
        return jnp.minimum((first // BF16_ROWS) * BF16_ROWS, CAP - WIN_E)

    def rows_at(start):
        return pl.ds(pl.multiple_of(group * CAP + start, BF16_ROWS), WIN_E)

    def start_windows(bb, slot):
        for e in range(N_EXPERTS):
            pltpu.make_async_copy(y_hbm.at[e, rows_at(win_start(e, bb))], ywin.at[slot, e], wsem.at[slot]).start()

    @pl.when(b == 0)
    def _():
        start_windows(0, 0)

    @pl.when(b + 1 < N_TBLK)
    def _():
        start_windows(b + 1, (b + 1) % 2)

    slot = b % 2
    tok = tok_ref[...]
    lane = lax.broadcasted_iota(I32, (TOK_BLK, WIN_E), 1)
    owner = [tok[:, F_SLOT_HI + e:F_SLOT_HI + e + 1] * LANES + tok[:, F_SLOT_LO + e:F_SLOT_LO + e + 1]
             for e in range(N_EXPERTS)]
    starts = [win_start(e, b) for e in range(N_EXPERTS)]

    pltpu.make_async_copy(ywin.at[slot], ywin.at[slot], wsem.at[slot]).wait()
    wide = lax.broadcasted_iota(I32, (TOK_BLK, EXPERTS_PER_DOT * WIN_E), 1)
    total = None
    for e0 in range(0, N_EXPERTS, EXPERTS_PER_DOT):
        own, slots = owner[e0], (starts[e0] + wide).astype(F32)
        for k in range(1, EXPERTS_PER_DOT):
            later = wide >= k * WIN_E
            own = jnp.where(later, owner[e0 + k], own)
            slots = jnp.where(later, (starts[e0 + k] - k * WIN_E + wide).astype(F32), slots)
        seg = jnp.where(own == slots, 1.0, 0.0).astype(BF16)
        rows = ywin[slot, e0:e0 + EXPERTS_PER_DOT].reshape(EXPERTS_PER_DOT * WIN_E, D_MODEL)
        part = _dot(seg, rows)
        total = part if total is None else total + part
    acc[...] = total

    for e in range(N_EXPERTS):
        past = jnp.maximum(roff_s[e * ROFF_STRIDE + b + 1] - (starts[e] + WIN_E), 0)

        def over(k, carry, e=e):
            nominal = starts[e] + (k + 1) * WIN_E
            first = jnp.minimum(nominal, CAP - WIN_E)
            cp = pltpu.make_async_copy(y_hbm.at[e, rows_at(first)], yext, esem.at[0])
            cp.start()
            cp.wait()
            slots = first + lane
            seg = jnp.where((owner[e] == slots.astype(F32)) & (slots >= nominal), 1.0, 0.0).astype(BF16)
            acc[...] += _dot(seg, yext[...])
            return carry

        lax.fori_loop(0, (past + WIN_E - 1) // WIN_E, over, 0)

    o_ref[...] = _layer_norm(ALPHA * x_ref[...] + g2_ref[0] * acc[...], lng_ref[...], lnb_ref[...])


def _combine(group, roff, y, tok, x1, g2, lng, lnb, seg_len):
    grid_spec = pltpu.PrefetchScalarGridSpec(
        num_scalar_prefetch=1,
        grid=(N_TBLK,),
        in_specs=[pl.BlockSpec(memory_space=pl.ANY),
                  pl.BlockSpec((TOK_BLK, LANES), lambda b, *_: (b, 0)),
                  pl.BlockSpec((TOK_BLK, D_MODEL), lambda b, *_: (b, 0)),
                  pl.BlockSpec((1, 1, D_MODEL), lambda b, *_: (b * TOK_BLK // seg_len, 0, 0)),
                  pl.BlockSpec((1, D_MODEL), lambda b, *_: (0, 0)),
                  pl.BlockSpec((1, D_MODEL), lambda b, *_: (0, 0))],
        out_specs=pl.BlockSpec((TOK_BLK, D_MODEL), lambda b, *_: (b, 0)),
        scratch_shapes=[pltpu.VMEM((2, N_EXPERTS, WIN_E, D_MODEL), BF16), pltpu.VMEM((WIN_E, D_MODEL), BF16),
                        pltpu.VMEM((TOK_BLK, D_MODEL), F32),
                        pltpu.SemaphoreType.DMA((2,)), pltpu.SemaphoreType.DMA((1,))],
    )
    return pl.pallas_call(
        functools.partial(_combine_kernel, group),
        grid_spec=grid_spec,
        out_shape=jax.ShapeDtypeStruct((N_TOK, D_MODEL), F32),
        compiler_params=_params(("arbitrary",)),
        name="moe_combine",
    )(roff, y, tok, x1, g2, lng, lnb)


def _rope_tables():
    t = jnp.arange(DEC_SEQ)
    n_freq = MLA_ROPE // 4
    inv_freq = ROPE_BASE ** (-jnp.arange(n_freq, dtype=F32) / n_freq)
    ang_r = (t // GRID_W).astype(F32)[:, None] * inv_freq[None, :]
    ang_c = (t % GRID_W).astype(F32)[:, None] * inv_freq[None, :]
    cos64 = jnp.concatenate([jnp.cos(ang_r)] * 2 + [jnp.cos(ang_c)] * 2, axis=1)
    sin64 = jnp.concatenate([jnp.sin(ang_r)] * 2 + [jnp.sin(ang_c)] * 2, axis=1)
    one, zero = jnp.ones_like(cos64), jnp.zeros_like(sin64)
    mla = (jnp.concatenate([cos64, one], axis=1), jnp.concatenate([sin64, zero], axis=1))
    swa = (jnp.concatenate([cos64, cos64], axis=1), jnp.concatenate([sin64, sin64], axis=1))
    return mla, swa


def _swa_key_layout(k):
    z = jnp.zeros_like(k)
    return jnp.stack([k, z, z, k], axis=2).reshape(k.shape[0], SWA_KW)


def _swa_value_layout(v):
    return jnp.stack([v, v], axis=2).reshape(v.shape[0], SWA_VW)


def kernel(x_prompt, x_sample, cache_mla_ckv, cache_mla_kpe, cache_swa_k, cache_swa_v, c, c_ctx, mod_w, mod_b, ln_gain, ln_bias, mla_w_in, mla_q_gain, mla_kv_gain, mla_w_q_up, mla_w_kv_up, mla_w_out, gm_w_in, gm_v_gain, gm_w_s, gm_b_s, gm_w_out, swa_w_qkv, swa_sink, swa_w_out, moe_router, moe_w_gate, moe_w_up, moe_w_down):
    d = D_MODEL
    xs = [x_prompt.reshape(N_TOK, d), x_sample.reshape(N_TOK, d)]
    seg_lens = [N_TOK, DEC_SEQ]
    cond8 = jnp.concatenate([c_ctx[None, :], c, jnp.zeros((3, d), F32)], axis=0)
    mod = _modulation_all(cond8, mod_w, mod_b)
    rope_mla, rope_swa = _rope_tables()
    w_gate_all = moe_w_gate.reshape(DEPTH * N_EXPERTS, d, EXPERT_FF)
    w_up_all = moe_w_up.reshape(DEPTH * N_EXPERTS, d, EXPERT_FF)
    w_down_all = moe_w_down.reshape(DEPTH * N_EXPERTS, EXPERT_FF, d)

    def mod_rows(i, q):
        m = mod[i, :, q * d:(q + 1) * d]
        return [m[0:1].reshape(1, 1, d), m[1:1 + DEC_BATCH].reshape(DEC_BATCH, 1, d)]

    ckv_out, kpe_out, k_out, v_out = [], [], [], []
    for i in range(DEPTH):
        kind, j = i % 3, i // 3
        sh1, sc1, g1, sh2, sc2, g2 = (mod_rows(i, q) for q in range(6))
        router_pad = jnp.pad(moe_router[i], ((0, 0), (0, LANES - N_EXPERTS)))
        lng = ln_gain[i].reshape(2, 1, d)
        lnb = ln_bias[i].reshape(2, 1, d)

        mixed = []
        if kind == 0:
            w = {
                "w_in": jnp.pad(mla_w_in[j], ((0, 0), (0, MLA_ROPE))).astype(BF16),
                "q_gain": mla_q_gain[j][None, :], "kv_gain": mla_kv_gain[j][None, :],
                "w_q": jnp.pad(mla_w_q_up[j], ((0, 0), (0, 0), (0, MLA_QK_PAD - MLA_NOPE - MLA_ROPE)))
                .reshape(MLA_Q_LORA, MLA_HEADS * MLA_QK_PAD).astype(BF16),
                "w_kv": mla_w_kv_up[j].reshape(MLA_KV_LORA, MLA_HEADS * (MLA_NOPE + MLA_V)).astype(BF16),
            }
            w_out = mla_w_out[j].astype(BF16)
            q, kv, kpe, ckv, kraw = _mla_proj(xs[0], sc1[0], sh1[0], w, seg_lens[0], None)
            ckv_out.append(ckv.reshape(CTX_BATCH, CTX_SEQ, MLA_KV_LORA))
            kpe_out.append(kraw.reshape(CTX_BATCH, CTX_SEQ, MLA_ROPE))
            mixed.append(_mla_attn_ctx(q, kv, kpe))
            q, kv, kpe, _, _ = _mla_proj(xs[1], sc1[1], sh1[1], w, seg_lens[1], rope_mla)
            kv_ctx = _matmul_bf16(cache_mla_ckv[:, j].reshape(DEC_BATCH * PAST_LEN, MLA_KV_LORA).astype(BF16), w["w_kv"])
            kpe_ctx = jnp.pad(cache_mla_kpe[:, j].reshape(DEC_BATCH * PAST_LEN, MLA_ROPE),
                              ((0, 0), (0, LANES - MLA_ROPE))).astype(BF16)
            mixed.append(_mla_attn_lat(q, kv, kpe, kv_ctx, kpe_ctx))
        elif kind == 1:
            w = {"w_in": gm_w_in[j].astype(BF16), "v_gain": gm_v_gain[j][None, :],
                 "w_s": gm_w_s[j].astype(BF16), "b_s": gm_b_s[j][:, :, None]}
            w_out = gm_w_out[j].astype(BF16)
            for g in range(2):
                mixed.append(_gmlp_in(xs[g], sc1[g], sh1[g], w, seg_lens[g]))
        else:
            wq = swa_w_qkv[j]
            nq, nkv = SWA_HEADS * SWA_HEAD_DIM, SWA_KV_HEADS * SWA_HEAD_DIM
            w_ext = jnp.concatenate(
                [wq[:, :nq],
                 _swa_key_layout(wq[:, nq:nq + nkv].reshape(d, SWA_KV_HEADS, SWA_HEAD_DIM)),
                 _swa_value_layout(wq[:, nq + nkv:].reshape(d, SWA_KV_HEADS, SWA_HEAD_DIM))], axis=1).astype(BF16)
            w_out = swa_w_out[j].astype(BF16)
            sink = swa_sink[j][None, :]
            q, kd, vd, kraw, vraw = _swa_proj(xs[0], sc1[0], sh1[0], w_ext, seg_lens[0], None)
            k_out.append(kraw.reshape(CTX_BATCH, CTX_SEQ, SWA_KV_HEADS, SWA_HEAD_DIM))
            v_out.append(vraw.reshape(CTX_BATCH, CTX_SEQ, SWA_KV_HEADS, SWA_HEAD_DIM))
            mixed.append(_swa_attn_ctx(q, kd, vd, sink))
            q, kd, vd, _, _ = _swa_proj(xs[1], sc1[1], sh1[1], w_ext, seg_lens[1], rope_swa)
            rows = DEC_BATCH * PAST_LEN
            kd_ctx = _swa_key_layout(cache_swa_k[:, j].reshape(rows, SWA_KV_HEADS, SWA_HEAD_DIM)).astype(BF16)
            vd_ctx = _swa_value_layout(cache_swa_v[:, j].reshape(rows, SWA_KV_HEADS, SWA_HEAD_DIM)).astype(BF16)
            mixed.append(_swa_attn_lat(q, kd, vd, kd_ctx, vd_ctx, sink))

        x1, h2, routed = [], [], []
        for g in range(2):
            x1_g, h2_g, lg_g = _post(mixed[g], w_out, xs[g], g1[g], sc2[g], sh2[g], lng[0], lnb[0], router_pad, seg_lens[g])
            x1.append(x1_g)
            h2.append(h2_g)
            routed.append(_route(lg_g))
        idx_all = jnp.concatenate([routed[0][3], routed[1][3]], axis=2).reshape(-1)
        y = _moe_ffn(i, idx_all, h2[0], h2[1], routed[0][0], routed[1][0], w_gate_all, w_up_all, w_down_all)
        for g in range(2):
            roff = routed[g][2].reshape(-1)
            xs[g] = _combine(g, roff, y, routed[g][1], x1[g], g2[g], lng[1], lnb[1], seg_lens[g])

    return (xs[0].reshape(CTX_BATCH, CTX_SEQ, d), xs[1].reshape(DEC_BATCH, DEC_SEQ, d),
            jnp.stack(ckv_out, axis=1), jnp.stack(kpe_out, axis=1),
            jnp.stack(k_out, axis=1), jnp.stack(v_out, axis=1))
```

```python
import functools

import jax
import jax.numpy as jnp
from jax import lax
from jax.experimental import pallas as pl
from jax.experimental.pallas import tpu as pltpu

D_MODEL = 1024
N_TOK = 4096
DEPTH = 4
DEC_BATCH = 4
DEC_SEQ = 1024
PAST_LEN = 256
CTX_BATCH = 16
CTX_SEQ = 256
GRID_W = 64
ALPHA = (2.0 * DEPTH) ** 0.25
MLA_HEADS = 8
MLA_Q_LORA = 384
MLA_KV_LORA = 256
MLA_NOPE = 128
MLA_ROPE = 64
MLA_V = 128
MLA_QK_PAD = 256
GM_FF = 6 * D_MODEL
GM_HALF = GM_FF // 2
GM_GROUPS = 8
GM_GW = GM_HALF // GM_GROUPS
CHUNK = 128
SWA_HEADS = 16
SWA_KV_HEADS = 4
SWA_HEAD_DIM = 64
WINDOW = 128
BAND = 128
N_EXPERTS = 16
EXPERT_FF = 2 * D_MODEL
CAP = 2 * N_TOK // N_EXPERTS
ROPE_BASE = 10000.0
NEG_INF = -1e30
EPS = 1e-6
F32 = jnp.float32
BF16 = jnp.bfloat16
I32 = jnp.int32
U32 = jnp.uint32
HIGHEST = lax.Precision.HIGHEST

LANES = 128
SUBLANES = 8
BF16_ROWS = 16
ROW_TILES = D_MODEL // LANES
VMEM_LIMIT = 56 * 1024 * 1024
TM = 256
TQ_LAT = 512
TM_PROJ = 512
FF_TILE = 1024
TOK_BLK = 256
MOD_TN = 1536

F_SLOT_HI, F_SLOT_LO, F_G0, F_G1, F_G2 = 0, 16, 32, 48, 64
F_T_HI, F_T_LO = 80, 81


def _params(sem, vmem=VMEM_LIMIT):
    return pltpu.CompilerParams(dimension_semantics=sem, vmem_limit_bytes=vmem)


def _dot(a, b):
    return jnp.dot(a, b, preferred_element_type=F32)


def _dot_nt(a, b):
    return lax.dot_general(a, b, (((1,), (1,)), ((), ())), preferred_element_type=F32)


def _sigmoid(x):
    return 1.0 / (1.0 + jnp.exp(-x))


def _gelu_tanh(x):
    return 0.5 * x * (1.0 + jnp.tanh(0.7978845608028654 * (x + 0.044715 * (x * x * x))))


def _layer_norm(t, gain, bias):
    mu = jnp.mean(t, axis=-1, keepdims=True)
    d = t - mu
    var = jnp.mean(d * d, axis=-1, keepdims=True)
    return d * lax.rsqrt(var + EPS) * gain + bias


def _rms_norm(t, gain):
    return t * lax.rsqrt(jnp.mean(t * t, axis=-1, keepdims=True) + EPS) * gain


def _rope128(x, cos, sin):
    lane = lax.broadcasted_iota(I32, x.shape, 1)
    first = (lane % 32) < 16
    swapped = jnp.where(first, -pltpu.roll(x, 112, 1), pltpu.roll(x, 16, 1))
    return x * cos + swapped * sin


def _mod_kernel(cond_ref, w_ref, b_ref, o_ref):
    c = cond_ref[...]
    s = c * _sigmoid(c)
    o_ref[0] = jnp.dot(s, w_ref[0], precision=HIGHEST, preferred_element_type=F32) + b_ref[0]


def _modulation_all(cond8, mod_w, mod_b):
    return pl.pallas_call(
        _mod_kernel,
        grid=(DEPTH, 6 * D_MODEL // MOD_TN),
        in_specs=[pl.BlockSpec((8, D_MODEL), lambda i, n: (0, 0)),
                  pl.BlockSpec((1, D_MODEL, MOD_TN), lambda i, n: (i, 0, n)),
                  pl.BlockSpec((1, 1, MOD_TN), lambda i, n: (i, 0, n))],
        out_specs=pl.BlockSpec((1, 8, MOD_TN), lambda i, n: (i, 0, n)),
        out_shape=jax.ShapeDtypeStruct((DEPTH, 8, 6 * D_MODEL), F32),
        compiler_params=_params(("arbitrary", "arbitrary")),
        name="modulation",
    )(cond8, mod_w, mod_b.reshape(DEPTH, 1, 6 * D_MODEL))


def _seg_spec(seg_len, tm=TM):
    return pl.BlockSpec((1, 1, D_MODEL), lambda r: (r * tm // seg_len, 0, 0))


def _row_spec(width, tm=TM):
    return pl.BlockSpec((tm, width), lambda r: (r, 0))


def _full_spec(shape):
    nd = len(shape)
    return pl.BlockSpec(shape, lambda r: (0,) * nd)


def _mla_proj_kernel(rope, x_ref, sc_ref, sh_ref, win_ref, qg_ref, kvg_ref, wq_ref, wkv_ref, *rest):
    if rope:
        cos_ref, sin_ref, q_ref, kv_ref, kpe_ref, ckv_ref, kraw_ref = rest
    else:
        q_ref, kv_ref, kpe_ref, ckv_ref, kraw_ref = rest
    h = (x_ref[...] * (1.0 + sc_ref[0]) + sh_ref[0]).astype(BF16)
    z = _dot(h, win_ref[...])
    cq = _rms_norm(z[:, :MLA_Q_LORA], qg_ref[...])
    ckv = _rms_norm(z[:, MLA_Q_LORA:MLA_Q_LORA + MLA_KV_LORA], kvg_ref[...])
    kp = z[:, MLA_Q_LORA + MLA_KV_LORA:]
    ckv_ref[...] = ckv
    kraw_ref[...] = kp[:, :MLA_ROPE]
    q = _dot(cq.astype(BF16), wq_ref[...])
    kv_ref[...] = _dot(ckv.astype(BF16), wkv_ref[...]).astype(BF16)
    if rope:
        cos, sin = cos_ref[...], sin_ref[...]
        for hd in range(MLA_HEADS):
            a = hd * MLA_QK_PAD
            q_ref[:, a:a + LANES] = q[:, a:a + LANES].astype(BF16)
            q_ref[:, a + LANES:a + 2 * LANES] = _rope128(q[:, a + LANES:a + 2 * LANES], cos, sin).astype(BF16)
        kp = _rope128(kp, cos, sin)
    else:
        q_ref[...] = q.astype(BF16)
    kpe_ref[...] = kp.astype(BF16)


def _mla_proj(x, sc, sh, w, seg_len, rope_tabs):
    rope = rope_tabs is not None
    seg, row = _seg_spec(seg_len, TM_PROJ), functools.partial(_row_spec, tm=TM_PROJ)
    in_specs = [row(D_MODEL), seg, seg,
                _full_spec(w["w_in"].shape), _full_spec((1, MLA_Q_LORA)), _full_spec((1, MLA_KV_LORA)),
                _full_spec(w["w_q"].shape), _full_spec(w["w_kv"].shape)]
    args = [x, sc, sh, w["w_in"], w["q_gain"], w["kv_gain"], w["w_q"], w["w_kv"]]
    if rope:
        nblk = DEC_SEQ // TM_PROJ
        tab_spec = pl.BlockSpec((TM_PROJ, LANES), lambda r: (r % nblk, 0))
        in_specs += [tab_spec, tab_spec]
        args += list(rope_tabs)
    wq = MLA_HEADS * MLA_QK_PAD
    return pl.pallas_call(
        functools.partial(_mla_proj_kernel, rope),
        grid=(N_TOK // TM_PROJ,),
        in_specs=in_specs,
        out_specs=[row(wq), row(wq), row(LANES), row(MLA_KV_LORA), row(MLA_ROPE)],
        out_shape=[jax.ShapeDtypeStruct((N_TOK, wq), BF16), jax.ShapeDtypeStruct((N_TOK, wq), BF16),
                   jax.ShapeDtypeStruct((N_TOK, LANES), BF16), jax.ShapeDtypeStruct((N_TOK, MLA_KV_LORA), F32),
                   jax.ShapeDtypeStruct((N_TOK, MLA_ROPE), F32)],
        compiler_params=_params(("arbitrary",)),
        name="mla_proj",
    )(*args)


def _mm_kernel(a_ref, w_ref, o_ref):
    o_ref[...] = _dot(a_ref[...], w_ref[...]).astype(o_ref.dtype)


def _matmul_bf16(a, w):
    m, k = a.shape
    n = w.shape[1]
    return pl.pallas_call(
        _mm_kernel,
        grid=(m // TM,),
        in_specs=[pl.BlockSpec((TM, k), lambda r: (r, 0)), _full_spec(w.shape)],
        out_specs=pl.BlockSpec((TM, n), lambda r: (r, 0)),
        out_shape=jax.ShapeDtypeStruct((m, n), BF16),
        compiler_params=_params(("arbitrary",)),
        name="matmul",
    )(a, w)


MLA_SCALE = (MLA_NOPE + MLA_ROPE) ** -0.5


def _mla_attn_ctx_kernel(q_ref, kv_ref, kpe_ref, o_ref):
    kpe = kpe_ref[...]
    for hd in range(MLA_HEADS):
        a = hd * MLA_QK_PAD
        qh = q_ref[:, a:a + MLA_QK_PAD]
        kh = jnp.concatenate([kv_ref[:, a:a + LANES], kpe], axis=1)
        vh = kv_ref[:, a + LANES:a + 2 * LANES]
        s = _dot_nt(qh, kh) * MLA_SCALE
        p = jnp.exp(s - jnp.max(s, axis=-1, keepdims=True))
        l = jnp.sum(p, axis=-1, keepdims=True)
        o_ref[:, hd * MLA_V:(hd + 1) * MLA_V] = (_dot(p.astype(BF16), vh) / l).astype(BF16)


def _mla_attn_ctx(q, kv, kpe):
    wq = MLA_HEADS * MLA_QK_PAD
    blk = lambda w: pl.BlockSpec((CTX_SEQ, w), lambda b: (b, 0))
    return pl.pallas_call(
        _mla_attn_ctx_kernel,
        grid=(CTX_BATCH,),
        in_specs=[blk(wq), blk(wq), blk(LANES)],
        out_specs=blk(MLA_HEADS * MLA_V),
        out_shape=jax.ShapeDtypeStruct((N_TOK, MLA_HEADS * MLA_V), BF16),
        compiler_params=_params(("arbitrary",)),
        name="mla_attn_ctx",
    )(q, kv, kpe)


def _mla_attn_lat_kernel(q_ref, kvl_ref, kpel_ref, kvc_ref, kpec_ref, o_ref):
    kpel = kpel_ref[...]
    kpec = kpec_ref[...]
    for hd in range(MLA_HEADS):
        a = hd * MLA_QK_PAD
        qh = q_ref[:, a:a + MLA_QK_PAD]
        kc = jnp.concatenate([kvc_ref[:, a:a + LANES], kpec], axis=1)
        kl = jnp.concatenate([kvl_ref[:, a:a + LANES], kpel], axis=1)
        sc = _dot_nt(qh, kc) * MLA_SCALE
        sl = _dot_nt(qh, kl) * MLA_SCALE
        m = jnp.maximum(jnp.max(sc, axis=-1, keepdims=True), jnp.max(sl, axis=-1, keepdims=True))
        pc = jnp.exp(sc - m)
        plat = jnp.exp(sl - m)
        l = jnp.sum(pc, axis=-1, keepdims=True) + jnp.sum(plat, axis=-1, keepdims=True)
        o = _dot(pc.astype(BF16), kvc_ref[:, a + LANES:a + 2 * LANES]) \
            + _dot(plat.astype(BF16), kvl_ref[:, a + LANES:a + 2 * LANES])
        o_ref[:, hd * MLA_V:(hd + 1) * MLA_V] = (o / l).astype(BF16)


def _mla_attn_lat(q, kv, kpe, kv_ctx, kpe_ctx):
    wq = MLA_HEADS * MLA_QK_PAD
    nq = DEC_SEQ // TQ_LAT
    return pl.pallas_call(
        _mla_attn_lat_kernel,
        grid=(DEC_BATCH, nq),
        in_specs=[pl.BlockSpec((TQ_LAT, wq), lambda b, i: (b * nq + i, 0)),
                  pl.BlockSpec((DEC_SEQ, wq), lambda b, i: (b, 0)),
                  pl.BlockSpec((DEC_SEQ, LANES), lambda b, i: (b, 0)),
                  pl.BlockSpec((PAST_LEN, wq), lambda b, i: (b, 0)),
                  pl.BlockSpec((PAST_LEN, LANES), lambda b, i: (b, 0))],
        out_specs=pl.BlockSpec((TQ_LAT, MLA_HEADS * MLA_V), lambda b, i: (b * nq + i, 0)),
        out_shape=jax.ShapeDtypeStruct((N_TOK, MLA_HEADS * MLA_V), BF16),
        compiler_params=_params(("arbitrary", "arbitrary")),
        name="mla_attn_lat",
    )(q, kv, kpe, kv_ctx, kpe_ctx)


def _gmlp_kernel(x_ref, sc_ref, sh_ref, win_ref, vg_ref, ws_ref, bs_ref, p_ref):
    h = (x_ref[...] * (1.0 + sc_ref[0]) + sh_ref[0]).astype(BF16)
    zv = _gelu_tanh(_dot(h, win_ref[:, GM_HALF:]))
    mu = jnp.mean(zv, axis=-1, keepdims=True)
    d = zv - mu
    var = jnp.mean(d * d, axis=-1, keepdims=True)
    vn = (d * lax.rsqrt(var + EPS) * vg_ref[...]).astype(BF16)
    for g in range(GM_GROUPS):
        c0 = g * GM_GW
        u = _gelu_tanh(_dot(h, win_ref[:, c0:c0 + GM_GW]))
        mixed = jnp.concatenate(
            [_dot(ws_ref[g], vn[k * CHUNK:(k + 1) * CHUNK, c0:c0 + GM_GW]) for k in range(TM // CHUNK)], axis=0)
        bias = jnp.concatenate([bs_ref[g]] * (TM // CHUNK), axis=0)
        p_ref[:, c0:c0 + GM_GW] = (u * (mixed + bias)).astype(BF16)


def _gmlp_in(x, sc, sh, w, seg_len):
    return pl.pallas_call(
        _gmlp_kernel,
        grid=(N_TOK // TM,),
        in_specs=[_row_spec(D_MODEL), _seg_spec(seg_len), _seg_spec(seg_len),
                  _full_spec((D_MODEL, GM_FF)), _full_spec((1, GM_HALF)),
                  _full_spec((GM_GROUPS, CHUNK, CHUNK)), _full_spec((GM_GROUPS, CHUNK, 1))],
        out_specs=_row_spec(GM_HALF),
        out_shape=jax.ShapeDtypeStruct((N_TOK, GM_HALF), BF16),
        compiler_params=_params(("arbitrary",)),
        name="gmlp_in",
    )(x, sc, sh, w["w_in"], w["v_gain"], w["w_s"], w["b_s"])


SWA_SCALE = SWA_HEAD_DIM ** -0.5
SWA_QW = SWA_HEADS * SWA_HEAD_DIM
SWA_KW = SWA_KV_HEADS * 2 * LANES
SWA_VW = SWA_KV_HEADS * LANES
SWA_GROUP = SWA_KV_HEADS


def _swa_proj_kernel(rope, x_ref, sc_ref, sh_ref, w_ref, *rest):
    if rope:
        cos_ref, sin_ref, q_ref, kd_ref, vd_ref, kraw_ref, vraw_ref = rest
    else:
        q_ref, kd_ref, vd_ref, kraw_ref, vraw_ref = rest
    h = (x_ref[...] * (1.0 + sc_ref[0]) + sh_ref[0]).astype(BF16)
    z = _dot(h, w_ref[...])
    lane = lax.broadcasted_iota(I32, (x_ref.shape[0], LANES), 1)
    low = lane < SWA_HEAD_DIM
    for m in range(SWA_KV_HEADS // 2):
        k0 = z[:, SWA_QW + (2 * m) * 2 * LANES:SWA_QW + (2 * m) * 2 * LANES + LANES]
        k1 = z[:, SWA_QW + (2 * m + 1) * 2 * LANES + LANES:SWA_QW + (2 * m + 2) * 2 * LANES]
        kraw_ref[:, m * LANES:(m + 1) * LANES] = jnp.where(low, k0, k1)
        v0 = z[:, SWA_QW + SWA_KW + (2 * m) * LANES:SWA_QW + SWA_KW + (2 * m + 1) * LANES]
        v1 = z[:, SWA_QW + SWA_KW + (2 * m + 1) * LANES:SWA_QW + SWA_KW + (2 * m + 2) * LANES]
        vraw_ref[:, m * LANES:(m + 1) * LANES] = jnp.where(low, v0, v1)
    vd_ref[...] = z[:, SWA_QW + SWA_KW:].astype(BF16)
    if rope:
        cos, sin = cos_ref[...], sin_ref[...]
        for j in range(SWA_QW // LANES):
            q_ref[:, j * LANES:(j + 1) * LANES] = _rope128(z[:, j * LANES:(j + 1) * LANES], cos, sin).astype(BF16)
        for j in range(SWA_KW // LANES):
            a = SWA_QW + j * LANES
            kd_ref[:, j * LANES:(j + 1) * LANES] = _rope128(z[:, a:a + LANES], cos, sin).astype(BF16)
    else:
        q_ref[...] = z[:, :SWA_QW].astype(BF16)
        kd_ref[...] = z[:, SWA_QW:SWA_QW + SWA_KW].astype(BF16)


def _swa_proj(x, sc, sh, w_ext, seg_len, rope_tabs):
    rope = rope_tabs is not None
    seg, row = _seg_spec(seg_len, TM_PROJ), functools.partial(_row_spec, tm=TM_PROJ)
    in_specs = [row(D_MODEL), seg, seg, _full_spec(w_ext.shape)]
    args = [x, sc, sh, w_ext]
    if rope:
        nblk = DEC_SEQ // TM_PROJ
        tab_spec = pl.BlockSpec((TM_PROJ, LANES), lambda r: (r % nblk, 0))
        in_specs += [tab_spec, tab_spec]
        args += list(rope_tabs)
    kvw = SWA_KV_HEADS * SWA_HEAD_DIM
    return pl.pallas_call(
        functools.partial(_swa_proj_kernel, rope),
        grid=(N_TOK // TM_PROJ,),
        in_specs=in_specs,
        out_specs=[row(SWA_QW), row(SWA_KW), row(SWA_VW), row(kvw), row(kvw)],
        out_shape=[jax.ShapeDtypeStruct((N_TOK, SWA_QW), BF16), jax.ShapeDtypeStruct((N_TOK, SWA_KW), BF16),
                   jax.ShapeDtypeStruct((N_TOK, SWA_VW), BF16), jax.ShapeDtypeStruct((N_TOK, kvw), F32),
                   jax.ShapeDtypeStruct((N_TOK, kvw), F32)],
        compiler_params=_params(("arbitrary",)),
        name="swa_proj",
    )(*args)


def _swa_heads(q_ref, sink_ref, o_ref, score_fn, value_fn):
    rows = q_ref.shape[0]
    lane = lax.broadcasted_iota(I32, (rows, LANES), 1)
    low = lane < SWA_HEAD_DIM
    for hk0 in range(0, SWA_KV_HEADS, SWA_GROUP):
        stacked, sinks = None, []
        for hk in range(hk0, hk0 + SWA_GROUP):
            c0 = hk * 2 * LANES
            q2 = jnp.concatenate([q_ref[:, c0:c0 + LANES], q_ref[:, c0 + LANES:c0 + 2 * LANES]], axis=0)
            for par in range(2):
                for pr in range(2):
                    hd = hk * 4 + pr * 2 + par
                    sinks.append(jnp.broadcast_to(sink_ref[:, hd:hd + 1], (rows, 1)))
                blocks = score_fn(hk, par, q2)
                if stacked is None:
                    stacked = [[] for _ in blocks]
                for kind, s in zip(stacked, blocks):
                    kind.append(s)
        scores = [jnp.concatenate(kind, axis=0) for kind in stacked]
        sk = jnp.concatenate(sinks, axis=0)
        m = sk
        for s in scores:
            m = jnp.maximum(m, jnp.max(s, axis=-1, keepdims=True))
        probs = [jnp.exp(s - m) for s in scores]
        l = jnp.exp(sk - m)
        for p in probs:
            l = l + jnp.sum(p, axis=-1, keepdims=True)
        inv = 1.0 / l
        n = 0
        for hk in range(hk0, hk0 + SWA_GROUP):
            res = []
            for par in range(2):
                blk = slice(n * 2 * rows, (n + 1) * 2 * rows)
                res.append(value_fn(hk, [p[blk] for p in probs]) * inv[blk])
                n += 1
            for pr in range(2):
                c0 = hk * 2 * LANES + pr * LANES
                part = slice(pr * rows, (pr + 1) * rows)
                o_ref[:, c0:c0 + LANES] = jnp.where(low, res[0][part], res[1][part]).astype(BF16)


def _swa_attn_ctx_kernel(q_ref, kd_ref, vd_ref, sink_ref, o_ref):
    def score_fn(hk, par, qp):
        a = hk * 2 * LANES + par * LANES
        return [_dot_nt(qp, kd_ref[:, a:a + LANES]) * SWA_SCALE]

    def value_fn(hk, probs):
        return _dot(probs[0].astype(BF16), vd_ref[:, hk * LANES:(hk + 1) * LANES])

    _swa_heads(q_ref, sink_ref, o_ref, score_fn, value_fn)


def _swa_attn_ctx(q, kd, vd, sink):
    blk = lambda w: pl.BlockSpec((CTX_SEQ, w), lambda b: (b, 0))
    return pl.pallas_call(
        _swa_attn_ctx_kernel,
        grid=(CTX_BATCH,),
        in_specs=[blk(SWA_QW), blk(SWA_KW), blk(SWA_VW), pl.BlockSpec((1, SWA_HEADS), lambda b: (0, 0))],
        out_specs=blk(SWA_QW),
        out_shape=jax.ShapeDtypeStruct((N_TOK, SWA_QW), BF16),
        compiler_params=_params(("arbitrary",)),
        name="swa_attn_ctx",
    )(q, kd, vd, sink)


SWA_WIN = 3 * BAND


def _swa_attn_lat_kernel(q_ref, kdl_ref, vdl_ref, kdc_ref, vdc_ref, sink_ref, o_ref):
    nb = pl.program_id(1)
    start = pl.multiple_of(jnp.clip((nb - 1) * BAND, 0, DEC_SEQ - SWA_WIN), BAND)
    qpos = nb * BAND + (lax.broadcasted_iota(I32, (2 * BAND, SWA_WIN), 0) & (BAND - 1))
    kpos = start + lax.broadcasted_iota(I32, (2 * BAND, SWA_WIN), 1)
    valid = jnp.abs(qpos - kpos) <= WINDOW

    def score_fn(hk, par, qp):
        a = hk * 2 * LANES + par * LANES
        kl = kdl_ref[pl.ds(start, SWA_WIN), a:a + LANES]
        sl = jnp.where(valid, _dot_nt(qp, kl) * SWA_SCALE, NEG_INF)
        sc = _dot_nt(qp, kdc_ref[:, a:a + LANES]) * SWA_SCALE
        return [sl, sc]

    def value_fn(hk, probs):
        vl = vdl_ref[pl.ds(start, SWA_WIN), hk * LANES:(hk + 1) * LANES]
        return _dot(probs[0].astype(BF16), vl) + _dot(probs[1].astype(BF16), vdc_ref[:, hk * LANES:(hk + 1) * LANES])

    _swa_heads(q_ref, sink_ref, o_ref, score_fn, value_fn)


def _swa_attn_lat(q, kd, vd, kd_ctx, vd_ctx, sink):
    nbands = DEC_SEQ // BAND
    return pl.pallas_call(
        _swa_attn_lat_kernel,
        grid=(DEC_BATCH, nbands),
        in_specs=[pl.BlockSpec((BAND, SWA_QW), lambda b, i: (b * nbands + i, 0)),
                  pl.BlockSpec((DEC_SEQ, SWA_KW), lambda b, i: (b, 0)),
                  pl.BlockSpec((DEC_SEQ, SWA_VW), lambda b, i: (b, 0)),
                  pl.BlockSpec((PAST_LEN, SWA_KW), lambda b, i: (b, 0)),
                  pl.BlockSpec((PAST_LEN, SWA_VW), lambda b, i: (b, 0)),
                  pl.BlockSpec((1, SWA_HEADS), lambda b, i: (0, 0))],
        out_specs=pl.BlockSpec((BAND, SWA_QW), lambda b, i: (b * nbands + i, 0)),
        out_shape=jax.ShapeDtypeStruct((N_TOK, SWA_QW), BF16),
        compiler_params=_params(("arbitrary", "arbitrary")),
        name="swa_attn_lat",
    )(q, kd, vd, kd_ctx, vd_ctx, sink)


def _post_kernel(a_ref, w_ref, x_ref, g1_ref, sc2_ref, sh2_ref, lng_ref, lnb_ref, rt_ref, x1_ref, h2_ref, lg_ref):
    y = _dot(a_ref[...], w_ref[...])
    x1 = _layer_norm(ALPHA * x_ref[...] + g1_ref[0] * y, lng_ref[...], lnb_ref[...])
    x1_ref[...] = x1
    h2 = x1 * (1.0 + sc2_ref[0]) + sh2_ref[0]
    for j in range(ROW_TILES):
        h2_ref[pl.ds(j, TM, stride=ROW_TILES), :] = h2[:, j * LANES:(j + 1) * LANES]
    rt = rt_ref[...]
    h_hi = h2.astype(BF16)
    h_lo = (h2 - h_hi.astype(F32)).astype(BF16)
    r_hi = rt.astype(BF16)
    r_lo = (rt - r_hi.astype(F32)).astype(BF16)
    lg_ref[...] = _dot(h_hi, r_hi) + (_dot(h_lo, r_hi) + _dot(h_hi, r_lo))


def _post(a, w_out, x, g1, sc2, sh2, lng, lnb, router_pad, seg_len):
    k = a.shape[1]
    return pl.pallas_call(
        _post_kernel,
        grid=(N_TOK // TM,),
        in_specs=[_row_spec(k), _full_spec((k, D_MODEL)), _row_spec(D_MODEL),
                  _seg_spec(seg_len), _seg_spec(seg_len), _seg_spec(seg_len),
                  _full_spec((1, D_MODEL)), _full_spec((1, D_MODEL)), _full_spec((D_MODEL, LANES))],
        out_specs=[_row_spec(D_MODEL), pl.BlockSpec((TM * ROW_TILES, LANES), lambda r: (r, 0)), _row_spec(LANES)],
        out_shape=[jax.ShapeDtypeStruct((N_TOK, D_MODEL), F32), jax.ShapeDtypeStruct((N_TOK * ROW_TILES, LANES), F32),
                   jax.ShapeDtypeStruct((N_TOK, LANES), F32)],
        compiler_params=_params(("arbitrary",)),
        name="mixer_out",
    )(a, w_out, x, g1, sc2, sh2, lng, lnb, router_pad)


CUM_BLK = 256
REFINE_STEPS = 16
F32_MIN_NORMAL = 1.1754943508222875e-38


def _excl_cumsum_lanes(a, upper):
    outs = []
    carry = jnp.zeros((a.shape[0], 1), F32)
    for b in range(N_TOK // CUM_BLK):
        blk = a[:, b * CUM_BLK:(b + 1) * CUM_BLK]
        outs.append(_dot(blk.astype(BF16), upper) + carry)
        carry = carry + jnp.sum(blk, axis=1, keepdims=True)
    return jnp.concatenate(outs, axis=1)


def _route_kernel(lg_ref, r_ref, tok_ref, roff_ref, idx_ref, rank_scr, w_scr):
    e_rows = N_EXPERTS
    lt = lg_ref[...].T[:e_rows]
    ex = jnp.exp(lt - jnp.max(lt, axis=0, keepdims=True))
    aff = ex / jnp.sum(ex, axis=0, keepdims=True)
    def count_ge(thr):
        return jnp.sum(jnp.where(aff >= thr, 1.0, 0.0), axis=1, keepdims=True)

    cur = jnp.zeros((e_rows, 1), I32)
    for b in range(30, -1, -1):
        cand = cur | jnp.int32(1 << b)
        cur = jnp.where(count_ge(lax.bitcast_convert_type(cand, F32)) >= float(CAP), cand, cur)
    lo = lax.bitcast_convert_type(cur, F32)
    hi = jnp.maximum(lax.bitcast_convert_type(cur + 1, F32), F32_MIN_NORMAL)
    for _ in range(REFINE_STEPS):
        w = hi - lo
        t1, t2, t3 = lo + 0.25 * w, lo + 0.5 * w, lo + 0.75 * w
        ok1, ok2, ok3 = (count_ge(t) >= float(CAP) for t in (t1, t2, t3))
        lo, hi = (jnp.where(ok3, t3, jnp.where(ok2, t2, jnp.where(ok1, t1, lo))),
                  jnp.where(ok1, jnp.where(ok2, jnp.where(ok3, hi, t3), t2), t1))
    gt = aff >= hi
    eq = (aff >= lo) & (aff < hi)
    need = float(CAP) - jnp.sum(jnp.where(gt, 1.0, 0.0), axis=1, keepdims=True)

    ri = lax.broadcasted_iota(I32, (CUM_BLK, CUM_BLK), 0)
    ci = lax.broadcasted_iota(I32, (CUM_BLK, CUM_BLK), 1)
    upper = jnp.where(ri < ci, 1.0, 0.0).astype(BF16)

    tie_rank = _excl_cumsum_lanes(jnp.where(eq, 1.0, 0.0), upper)
    sel = gt | (eq & (tie_rank < need))
    msk = jnp.where(sel, 1.0, 0.0)

    rank = _excl_cumsum_lanes(msk, upper)
    rank_scr[...] = jnp.where(sel, rank, -1.0)

    inv = 1.0 / LANES
    slot_no = jnp.where(sel, rank, -float(LANES))
    slot_hi = jnp.floor(slot_no * inv)
    g0 = aff.astype(BF16).astype(F32)
    r1 = aff - g0
    g1 = r1.astype(BF16).astype(F32)
    g2 = (r1 - g1).astype(BF16).astype(F32)
    tt = lax.broadcasted_iota(I32, (1, N_TOK), 1).astype(F32)
    t_hi = jnp.floor(tt * inv)
    misc = jnp.concatenate([t_hi, tt - t_hi * LANES, jnp.zeros((6, N_TOK), F32)], axis=0)
    feat = jnp.concatenate([slot_hi, slot_no - slot_hi * LANES, g0, g1, g2, misc,
                            jnp.zeros((LANES - 5 * e_rows - 8, N_TOK), F32)], axis=0)
    table = feat.T
    tok_ref[...] = table
    w_scr[...] = table.astype(BF16)

    nblk = N_TOK // TOK_BLK
    bt = lax.broadcasted_iota(I32, (N_TOK, LANES), 0) // TOK_BLK
    bb = lax.broadcasted_iota(I32, (N_TOK, LANES), 1)
    before = jnp.where((bt < bb) & (bb <= nblk), 1.0, 0.0).astype(BF16)
    roff_ref[...] = _dot(msk.astype(BF16), before).astype(I32)

    sub = lax.broadcasted_iota(I32, (8, LANES), 0)
    ln8 = lax.broadcasted_iota(I32, (8, LANES), 1)
    rb_rows = 128

    def per_expert(e, carry):
        rk = rank_scr[pl.ds(e, 1), :]
        want = jnp.where(sub < 5, sub * e_rows + e, sub + (F_T_HI - 5))
        pick = jnp.where(ln8 == want, 1.0, 0.0).astype(BF16)
        for rb in range(CAP // rb_rows):
            slot = (rb * rb_rows + lax.broadcasted_iota(I32, (rb_rows, 1), 0)).astype(F32)
            onehot = jnp.where(rk == slot, 1.0, 0.0).astype(BF16)
            rows = _dot(onehot, w_scr[...])
            r_ref[pl.ds(e, 1), rb * rb_rows:(rb + 1) * rb_rows, :] = rows[None]
            got = _dot_nt(pick, rows.astype(BF16))
            idx_ref[pl.ds(e, 1), :, rb * rb_rows:(rb + 1) * rb_rows] = (got[5:6] * LANES + got[6:7]).astype(I32)[None]
        return carry

    lax.fori_loop(0, e_rows, per_expert, 0)


def _route(logits):
    return pl.pallas_call(
        _route_kernel,
        grid=(1,),
        in_specs=[pl.BlockSpec((N_TOK, LANES), lambda i: (0, 0))],
        out_specs=[pl.BlockSpec((N_EXPERTS, CAP, LANES), lambda i: (0, 0, 0)),
                   pl.BlockSpec((N_TOK, LANES), lambda i: (0, 0)),
                   pl.BlockSpec((N_EXPERTS, LANES), lambda i: (0, 0)),
                   pl.BlockSpec((N_EXPERTS, 1, CAP), lambda i: (0, 0, 0))],
        out_shape=[jax.ShapeDtypeStruct((N_EXPERTS, CAP, LANES), F32),
                   jax.ShapeDtypeStruct((N_TOK, LANES), F32),
                   jax.ShapeDtypeStruct((N_EXPERTS, LANES), I32),
                   jax.ShapeDtypeStruct((N_EXPERTS, 1, CAP), I32)],
        scratch_shapes=[pltpu.VMEM((N_EXPERTS, N_TOK), F32), pltpu.VMEM((N_TOK, LANES), BF16)],
        compiler_params=_params(("arbitrary",)),
        name="moe_route",
    )(logits)


ROWS_E = 2 * CAP
N_FT = EXPERT_FF // FF_TILE
GATHER_UNROLL = 8
PIECE = 256
N_PIECE = FF_TILE // PIECE


def _tile(row):
    return pl.ds(pl.multiple_of(row * ROW_TILES, ROW_TILES), ROW_TILES)


def _ffn_kernel(idx_s, hc_hbm, hl_hbm, rc_ref, rl_ref, wg_ref, wu_ref, wd_ref, y_ref, xe, xb, acc, gsem):
    e = pl.program_id(0)
    f = pl.program_id(1)

    def gather_rows(ee):
        base = ee * ROWS_E

        def body(r, carry):
            tc = idx_s[base + r]
            pltpu.make_async_copy(hc_hbm.at[_tile(tc)], xe.at[_tile(r)], gsem.at[0]).start(priority=0)
            tl = idx_s[base + CAP + r]
            pltpu.make_async_copy(hl_hbm.at[_tile(tl)], xe.at[_tile(CAP + r)], gsem.at[0]).start(priority=1)
            return carry

        lax.fori_loop(0, CAP, body, 0, unroll=GATHER_UNROLL)

    @pl.when(f == 0)
    def _():
        @pl.when(e == 0)
        def _():
            gather_rows(0)

        pltpu.make_async_copy(xe, xe, gsem.at[0]).wait()
        for j in range(ROW_TILES):
            xb[:, j * LANES:(j + 1) * LANES] = xe[pl.ds(j, ROWS_E, stride=ROW_TILES), :].astype(BF16)

        @pl.when(e + 1 < N_EXPERTS)
        def _():
            gather_rows(e + 1)

        acc[...] = jnp.zeros_like(acc)

    x = xb[...]
    for p in range(N_PIECE):
        cols = slice(p * PIECE, (p + 1) * PIECE)
        g = _dot(x, wg_ref[0, :, cols].astype(BF16))
        u = _dot(x, wu_ref[0, :, cols].astype(BF16))
        hid = (g * _sigmoid(g) * u).astype(BF16)
        acc[...] += _dot(hid, wd_ref[0, cols, :].astype(BF16))

    @pl.when(f == N_FT - 1)
    def _():
        lane = lax.broadcasted_iota(I32, (CAP, LANES), 1)
        mine = (lane == F_G0 + e) | (lane == F_G1 + e) | (lane == F_G2 + e)
        gate = jnp.concatenate(
            [jnp.sum(jnp.where(mine, rc_ref[0], 0.0), axis=1, keepdims=True),
             jnp.sum(jnp.where(mine, rl_ref[0], 0.0), axis=1, keepdims=True)], axis=0)
        y_ref[0] = (acc[...] * gate).astype(BF16)


def _moe_ffn(layer, idx_all, h_ctx, h_lat, r_ctx, r_lat, w_gate, w_up, w_down):
    any_spec = pl.BlockSpec(memory_space=pl.ANY)
    e0 = layer * N_EXPERTS
    grid_spec = pltpu.PrefetchScalarGridSpec(
        num_scalar_prefetch=1,
        grid=(N_EXPERTS, N_FT),
        in_specs=[any_spec, any_spec,
                  pl.BlockSpec((1, CAP, LANES), lambda e, f, *_: (e, 0, 0)),
                  pl.BlockSpec((1, CAP, LANES), lambda e, f, *_: (e, 0, 0)),
                  pl.BlockSpec((1, D_MODEL, FF_TILE), lambda e, f, *_: (e0 + e, 0, f)),
                  pl.BlockSpec((1, D_MODEL, FF_TILE), lambda e, f, *_: (e0 + e, 0, f)),
                  pl.BlockSpec((1, FF_TILE, D_MODEL), lambda e, f, *_: (e0 + e, f, 0))],
        out_specs=pl.BlockSpec((1, ROWS_E, D_MODEL), lambda e, f, *_: (e, 0, 0)),
        scratch_shapes=[pltpu.VMEM((ROWS_E * ROW_TILES, LANES), F32), pltpu.VMEM((ROWS_E, D_MODEL), BF16),
                        pltpu.VMEM((ROWS_E, D_MODEL), F32), pltpu.SemaphoreType.DMA((1,))],
    )
    return pl.pallas_call(
        _ffn_kernel,
        grid_spec=grid_spec,
        out_shape=jax.ShapeDtypeStruct((N_EXPERTS, ROWS_E, D_MODEL), BF16),
        compiler_params=_params(("arbitrary", "arbitrary")),
        name="moe_ffn",
    )(idx_all, h_ctx, h_lat, r_ctx, r_lat, w_gate, w_up, w_down)


WIN_E = 64
EXPERTS_PER_DOT = 4
N_TBLK = N_TOK // TOK_BLK
ROFF_STRIDE = LANES


def _combine_kernel(group, roff_s, y_hbm, tok_ref, x_ref, g2_ref, lng_ref, lnb_ref, o_ref, ywin, yext, acc, wsem, esem):
    b = pl.program_id(0)

    def win_start(e, bb):
        first = roff_s[e * ROFF_STRIDE + bb]
        return jnp.minimum((first // BF16_ROWS) * BF16_ROWS, CAP - WIN_E)

    def rows_at(start):
        return pl.ds(pl.multiple_of(group * CAP + start, BF16_ROWS), WIN_E)

    def start_windows(bb, slot):
        for e in range(N_EXPERTS):
            pltpu.make_async_copy(y_hbm.at[e, rows_at(win_start(e, bb))], ywin.at[slot, e], wsem.at[slot]).start()

    @pl.when(b == 0)
    def _():
        start_windows(0, 0)

    @pl.when(b + 1 < N_TBLK)
    def _():
        start_windows(b + 1, (b + 1) % 2)

    slot = b % 2
    tok = tok_ref[...]
    lane = lax.broadcasted_iota(I32, (TOK_BLK, WIN_E), 1)
    owner = [tok[:, F_SLOT_HI + e:F_SLOT_HI + e + 1] * LANES + tok[:, F_SLOT_LO + e:F_SLOT_LO + e + 1]
             for e in range(N_EXPERTS)]
    starts = [win_start(e, b) for e in range(N_EXPERTS)]

    pltpu.make_async_copy(ywin.at[slot], ywin.at[slot], wsem.at[slot]).wait()
    wide = lax.broadcasted_iota(I32, (TOK_BLK, EXPERTS_PER_DOT * WIN_E), 1)
    total = None
    for e0 in range(0, N_EXPERTS, EXPERTS_PER_DOT):
        own, slots = owner[e0], (starts[e0] + wide).astype(F32)
        for k in range(1, EXPERTS_PER_DOT):
            later = wide >= k * WIN_E
            own = jnp.where(later, owner[e0 + k], own)
            slots = jnp.where(later, (starts[e0 + k] - k * WIN_E + wide).astype(F32), slots)
        seg = jnp.where(own == slots, 1.0, 0.0).astype(BF16)
        rows = ywin[slot, e0:e0 + EXPERTS_PER_DOT].reshape(EXPERTS_PER_DOT * WIN_E, D_MODEL)
        part = _dot(seg, rows)
        total = part if total is None else total + part
    acc[...] = total

    for e in range(N_EXPERTS):
        past = jnp.maximum(roff_s[e * ROFF_STRIDE + b + 1] - (starts[e] + WIN_E), 0)

        def over(k, carry, e=e):
            nominal = starts[e] + (k + 1) * WIN_E
            first = jnp.minimum(nominal, CAP - WIN_E)
            cp = pltpu.make_async_copy(y_hbm.at[e, rows_at(first)], yext, esem.at[0])
            cp.start()
            cp.wait()
            slots = first + lane
            seg = jnp.where((owner[e] == slots.astype(F32)) & (slots >= nominal), 1.0, 0.0).astype(BF16)
            acc[...] += _dot(seg, yext[...])
            return carry

        lax.fori_loop(0, (past + WIN_E - 1) // WIN_E, over, 0)

    o_ref[...] = _layer_norm(ALPHA * x_ref[...] + g2_ref[0] * acc[...], lng_ref[...], lnb_ref[...])


def _combine(group, roff, y, tok, x1, g2, lng, lnb, seg_len):
    grid_spec = pltpu.PrefetchScalarGridSpec(
        num_scalar_prefetch=1,
        grid=(N_TBLK,),
        in_specs=[pl.BlockSpec(memory_space=pl.ANY),
                  pl.BlockSpec((TOK_BLK, LANES), lambda b, *_: (b, 0)),
                  pl.BlockSpec((TOK_BLK, D_MODEL), lambda b, *_: (b, 0)),
                  pl.BlockSpec((1, 1, D_MODEL), lambda b, *_: (b * TOK_BLK // seg_len, 0, 0)),
                  pl.BlockSpec((1, D_MODEL), lambda b, *_: (0, 0)),
                  pl.BlockSpec((1, D_MODEL), lambda b, *_: (0, 0))],
        out_specs=pl.BlockSpec((TOK_BLK, D_MODEL), lambda b, *_: (b, 0)),
        scratch_shapes=[pltpu.VMEM((2, N_EXPERTS, WIN_E, D_MODEL), BF16), pltpu.VMEM((WIN_E, D_MODEL), BF16),
                        pltpu.VMEM((TOK_BLK, D_MODEL), F32),
                        pltpu.SemaphoreType.DMA((2,)), pltpu.SemaphoreType.DMA((1,))],
    )
    return pl.pallas_call(
        functools.partial(_combine_kernel, group),
        grid_spec=grid_spec,
        out_shape=jax.ShapeDtypeStruct((N_TOK, D_MODEL), F32),
        compiler_params=_params(("arbitrary",)),
        name="moe_combine",
    )(roff, y, tok, x1, g2, lng, lnb)


def _rope_tables():
    t = jnp.arange(DEC_SEQ)
    n_freq = MLA_ROPE // 4
    inv_freq = ROPE_BASE ** (-jnp.arange(n_freq, dtype=F32) / n_freq)
    ang_r = (t // GRID_W).astype(F32)[:, None] * inv_freq[None, :]
    ang_c = (t % GRID_W).astype(F32)[:, None] * inv_freq[None, :]
    cos64 = jnp.concatenate([jnp.cos(ang_r)] * 2 + [jnp.cos(ang_c)] * 2, axis=1)
    sin64 = jnp.concatenate([jnp.sin(ang_r)] * 2 + [jnp.sin(ang_c)] * 2, axis=1)
    one, zero = jnp.ones_like(cos64), jnp.zeros_like(sin64)
    mla = (jnp.concatenate([cos64, one], axis=1), jnp.concatenate([sin64, zero], axis=1))
    swa = (jnp.concatenate([cos64, cos64], axis=1), jnp.concatenate([sin64, sin64], axis=1))
    return mla, swa


def _swa_key_layout(k):
    z = jnp.zeros_like(k)
    return jnp.stack([k, z, z, k], axis=2).reshape(k.shape[0], SWA_KW)


def _swa_value_layout(v):
    return jnp.stack([v, v], axis=2).reshape(v.shape[0], SWA_VW)


def kernel(x_prompt, x_sample, cache_mla_ckv, cache_mla_kpe, cache_swa_k, cache_swa_v, c, c_ctx, mod_w, mod_b, ln_gain, ln_bias, mla_w_in, mla_q_gain, mla_kv_gain, mla_w_q_up, mla_w_kv_up, mla_w_out, gm_w_in, gm_v_gain, gm_w_s, gm_b_s, gm_w_out, swa_w_qkv, swa_sink, swa_w_out, moe_router, moe_w_gate, moe_w_up, moe_w_down):
    d = D_MODEL
    xs = [x_prompt.reshape(N_TOK, d), x_sample.reshape(N_TOK, d)]
    seg_lens = [N_TOK, DEC_SEQ]
    cond8 = jnp.concatenate([c_ctx[None, :], c, jnp.zeros((3, d), F32)], axis=0)
    mod = _modulation_all(cond8, mod_w, mod_b)
    rope_mla, rope_swa = _rope_tables()
    w_gate_all = moe_w_gate.reshape(DEPTH * N_EXPERTS, d, EXPERT_FF)
    w_up_all = moe_w_up.reshape(DEPTH * N_EXPERTS, d, EXPERT_FF)
    w_down_all = moe_w_down.reshape(DEPTH * N_EXPERTS, EXPERT_FF, d)

    def mod_rows(i, q):
        m = mod[i, :, q * d:(q + 1) * d]
        return [m[0:1].reshape(1, 1, d), m[1:1 + DEC_BATCH].reshape(DEC_BATCH, 1, d)]

    ckv_out, kpe_out, k_out, v_out = [], [], [], []
    for i in range(DEPTH):
        kind, j = i % 3, i // 3
        sh1, sc1, g1, sh2, sc2, g2 = (mod_rows(i, q) for q in range(6))
        router_pad = jnp.pad(moe_router[i], ((0, 0), (0, LANES - N_EXPERTS)))
        lng = ln_gain[i].reshape(2, 1, d)
        lnb = ln_bias[i].reshape(2, 1, d)

        mixed = []
        if kind == 0:
            w = {
                "w_in": jnp.pad(mla_w_in[j], ((0, 0), (0, MLA_ROPE))).astype(BF16),
                "q_gain": mla_q_gain[j][None, :], "kv_gain": mla_kv_gain[j][None, :],
                "w_q": jnp.pad(mla_w_q_up[j], ((0, 0), (0, 0), (0, MLA_QK_PAD - MLA_NOPE - MLA_ROPE)))
                .reshape(MLA_Q_LORA, MLA_HEADS * MLA_QK_PAD).astype(BF16),
                "w_kv": mla_w_kv_up[j].reshape(MLA_KV_LORA, MLA_HEADS * (MLA_NOPE + MLA_V)).astype(BF16),
            }
            w_out = mla_w_out[j].astype(BF16)
            q, kv, kpe, ckv, kraw = _mla_proj(xs[0], sc1[0], sh1[0], w, seg_lens[0], None)
            ckv_out.append(ckv.reshape(CTX_BATCH, CTX_SEQ, MLA_KV_LORA))
            kpe_out.append(kraw.reshape(CTX_BATCH, CTX_SEQ, MLA_ROPE))
            mixed.append(_mla_attn_ctx(q, kv, kpe))
            q, kv, kpe, _, _ = _mla_proj(xs[1], sc1[1], sh1[1], w, seg_lens[1], rope_mla)
            kv_ctx = _matmul_bf16(cache_mla_ckv[:, j].reshape(DEC_BATCH * PAST_LEN, MLA_KV_LORA).astype(BF16), w["w_kv"])
            kpe_ctx = jnp.pad(cache_mla_kpe[:, j].reshape(DEC_BATCH * PAST_LEN, MLA_ROPE),
                              ((0, 0), (0, LANES - MLA_ROPE))).astype(BF16)
            mixed.append(_mla_attn_lat(q, kv, kpe, kv_ctx, kpe_ctx))
        elif kind == 1:
            w = {"w_in": gm_w_in[j].astype(BF16), "v_gain": gm_v_gain[j][None, :],
                 "w_s": gm_w_s[j].astype(BF16), "b_s": gm_b_s[j][:, :, None]}
            w_out = gm_w_out[j].astype(BF16)
            for g in range(2):
                mixed.append(_gmlp_in(xs[g], sc1[g], sh1[g], w, seg_lens[g]))
        else:
            wq = swa_w_qkv[j]
            nq, nkv = SWA_HEADS * SWA_HEAD_DIM, SWA_KV_HEADS * SWA_HEAD_DIM
            w_ext = jnp.concatenate(
                [wq[:, :nq],
                 _swa_key_layout(wq[:, nq:nq + nkv].reshape(d, SWA_KV_HEADS, SWA_HEAD_DIM)),
                 _swa_value_layout(wq[:, nq + nkv:].reshape(d, SWA_KV_HEADS, SWA_HEAD_DIM))], axis=1).astype(BF16)
            w_out = swa_w_out[j].astype(BF16)
            sink = swa_sink[j][None, :]
            q, kd, vd, kraw, vraw = _swa_proj(xs[0], sc1[0], sh1[0], w_ext, seg_lens[0], None)
            k_out.append(kraw.reshape(CTX_BATCH, CTX_SEQ, SWA_KV_HEADS, SWA_HEAD_DIM))
            v_out.append(vraw.reshape(CTX_BATCH, CTX_SEQ, SWA_KV_HEADS, SWA_HEAD_DIM))
            mixed.append(_swa_attn_ctx(q, kd, vd, sink))
            q, kd, vd, _, _ = _swa_proj(xs[1], sc1[1], sh1[1], w_ext, seg_lens[1], rope_swa)
            rows = DEC_BATCH * PAST_LEN
            kd_ctx = _swa_key_layout(cache_swa_k[:, j].reshape(rows, SWA_KV_HEADS, SWA_HEAD_DIM)).astype(BF16)
            vd_ctx = _swa_value_layout(cache_swa_v[:, j].reshape(rows, SWA_KV_HEADS, SWA_HEAD_DIM)).astype(BF16)
            mixed.append(_swa_attn_lat(q, kd, vd, kd_ctx, vd_ctx, sink))

        x1, h2, routed = [], [], []
        for g in range(2):
            x1_g, h2_g, lg_g = _post(mixed[g], w_out, xs[g], g1[g], sc2[g], sh2[g], lng[0], lnb[0], router_pad, seg_lens[g])
            x1.append(x1_g)
            h2.append(h2_g)
            routed.append(_route(lg_g))
        idx_all = jnp.concatenate([routed[0][3], routed[1][3]], axis=2).reshape(-1)
        y = _moe_ffn(i, idx_all, h2[0], h2[1], routed[0][0], routed[1][0], w_gate_all, w_up_all, w_down_all)
        for g in range(2):
            roff = routed[g][2].reshape(-1)
            xs[g] = _combine(g, roff, y, routed[g][1], x1[g], g2[g], lng[1], lnb[1], seg_lens[g])

    return (xs[0].reshape(CTX_BATCH, CTX_SEQ, d), xs[1].reshape(DEC_BATCH, DEC_SEQ, d),
            jnp.stack(ckv_out, axis=1), jnp.stack(kpe_out, axis=1),
            jnp.stack(k_out, axis=1), jnp.stack(v_out, axis=1))
```

```python
import functools

import jax
import jax.numpy as jnp
from jax import lax
from jax.experimental import pallas as pl
from jax.experimental.pallas import tpu as pltpu

D_MODEL = 1024
N_TOK = 4096
DEPTH = 4
DEC_BATCH = 4
DEC_SEQ = 1024
PAST_LEN = 256
CTX_BATCH = 16
CTX_SEQ = 256
GRID_W = 64
ALPHA = (2.0 * DEPTH) ** 0.25
MLA_HEADS = 8
MLA_Q_LORA = 384
MLA_KV_LORA = 256
MLA_NOPE = 128
MLA_ROPE = 64
MLA_V = 128
MLA_QK_PAD = 256
GM_FF = 6 * D_MODEL
GM_HALF = GM_FF // 2
GM_GROUPS = 8
GM_GW = GM_HALF // GM_GROUPS
CHUNK = 128
SWA_HEADS = 16
SWA_KV_HEADS = 4
SWA_HEAD_DIM = 64
WINDOW = 128
BAND = 128
N_EXPERTS = 16
EXPERT_FF = 2 * D_MODEL
CAP = 2 * N_TOK // N_EXPERTS
ROPE_BASE = 10000.0
NEG_INF = -1e30
EPS = 1e-6
F32 = jnp.float32
BF16 = jnp.bfloat16
I32 = jnp.int32
U32 = jnp.uint32
HIGHEST = lax.Precision.HIGHEST

LANES = 128
SUBLANES = 8
BF16_ROWS = 16
ROW_TILES = D_MODEL // LANES
VMEM_LIMIT = 56 * 1024 * 1024
TM = 256
TQ_LAT = 512
TM_PROJ = 512
FF_TILE = 1024
TOK_BLK = 256
MOD_TN = 1536

F_SLOT_HI, F_SLOT_LO, F_G0, F_G1, F_G2 = 0, 16, 32, 48, 64
F_T_HI, F_T_LO = 80, 81


def _params(sem, vmem=VMEM_LIMIT):
    return pltpu.CompilerParams(dimension_semantics=sem, vmem_limit_bytes=vmem)


def _dot(a, b):
    return jnp.dot(a, b, preferred_element_type=F32)


def _dot_nt(a, b):
    return lax.dot_general(a, b, (((1,), (1,)), ((), ())), preferred_element_type=F32)


def _sigmoid(x):
    return 1.0 / (1.0 + jnp.exp(-x))


def _gelu_tanh(x):
    return 0.5 * x * (1.0 + jnp.tanh(0.7978845608028654 * (x + 0.044715 * (x * x * x))))


def _layer_norm(t, gain, bias):
    mu = jnp.mean(t, axis=-1, keepdims=True)
    d = t - mu
    var = jnp.mean(d * d, axis=-1, keepdims=True)
    return d * lax.rsqrt(var + EPS) * gain + bias


def _rms_norm(t, gain):
    return t * lax.rsqrt(jnp.mean(t * t, axis=-1, keepdims=True) + EPS) * gain


def _rope128(x, cos, sin):
    lane = lax.broadcasted_iota(I32, x.shape, 1)
    first = (lane % 32) < 16
    swapped = jnp.where(first, -pltpu.roll(x, 112, 1), pltpu.roll(x, 16, 1))
    return x * cos + swapped * sin


def _mod_kernel(cond_ref, w_ref, b_ref, o_ref):
    c = cond_ref[...]
    s = c * _sigmoid(c)
    o_ref[0] = jnp.dot(s, w_ref[0], precision=HIGHEST, preferred_element_type=F32) + b_ref[0]


def _modulation_all(cond8, mod_w, mod_b):
    return pl.pallas_call(
        _mod_kernel,
        grid=(DEPTH, 6 * D_MODEL // MOD_TN),
        in_specs=[pl.BlockSpec((8, D_MODEL), lambda i, n: (0, 0)),
                  pl.BlockSpec((1, D_MODEL, MOD_TN), lambda i, n: (i, 0, n)),
                  pl.BlockSpec((1, 1, MOD_TN), lambda i, n: (i, 0, n))],
        out_specs=pl.BlockSpec((1, 8, MOD_TN), lambda i, n: (i, 0, n)),
        out_shape=jax.ShapeDtypeStruct((DEPTH, 8, 6 * D_MODEL), F32),
        compiler_params=_params(("arbitrary", "arbitrary")),
        name="modulation",
    )(cond8, mod_w, mod_b.reshape(DEPTH, 1, 6 * D_MODEL))


def _seg_spec(seg_len, tm=TM):
    return pl.BlockSpec((1, 1, D_MODEL), lambda r: (r * tm // seg_len, 0, 0))


def _row_spec(width, tm=TM):
    return pl.BlockSpec((tm, width), lambda r: (r, 0))


def _full_spec(shape):
    nd = len(shape)
    return pl.BlockSpec(shape, lambda r: (0,) * nd)


def _mla_proj_kernel(rope, x_ref, sc_ref, sh_ref, win_ref, qg_ref, kvg_ref, wq_ref, wkv_ref, *rest):
    if rope:
        cos_ref, sin_ref, q_ref, kv_ref, kpe_ref, ckv_ref, kraw_ref = rest
    else:
        q_ref, kv_ref, kpe_ref, ckv_ref, kraw_ref = rest
    h = (x_ref[...] * (1.0 + sc_ref[0]) + sh_ref[0]).astype(BF16)
    z = _dot(h, win_ref[...])
    cq = _rms_norm(z[:, :MLA_Q_LORA], qg_ref[...])
    ckv = _rms_norm(z[:, MLA_Q_LORA:MLA_Q_LORA + MLA_KV_LORA], kvg_ref[...])
    kp = z[:, MLA_Q_LORA + MLA_KV_LORA:]
    ckv_ref[...] = ckv
    kraw_ref[...] = kp[:, :MLA_ROPE]
    q = _dot(cq.astype(BF16), wq_ref[...])
    kv_ref[...] = _dot(ckv.astype(BF16), wkv_ref[...]).astype(BF16)
    if rope:
        cos, sin = cos_ref[...], sin_ref[...]
        for hd in range(MLA_HEADS):
            a = hd * MLA_QK_PAD
            q_ref[:, a:a + LANES] = q[:, a:a + LANES].astype(BF16)
            q_ref[:, a + LANES:a + 2 * LANES] = _rope128(q[:, a + LANES:a + 2 * LANES], cos, sin).astype(BF16)
        kp = _rope128(kp, cos, sin)
    else:
        q_ref[...] = q.astype(BF16)
    kpe_ref[...] = kp.astype(BF16)


def _mla_proj(x, sc, sh, w, seg_len, rope_tabs):
    rope = rope_tabs is not None
    seg, row = _seg_spec(seg_len, TM_PROJ), functools.partial(_row_spec, tm=TM_PROJ)
    in_specs = [row(D_MODEL), seg, seg,
                _full_spec(w["w_in"].shape), _full_spec((1, MLA_Q_LORA)), _full_spec((1, MLA_KV_LORA)),
                _full_spec(w["w_q"].shape), _full_spec(w["w_kv"].shape)]
    args = [x, sc, sh, w["w_in"], w["q_gain"], w["kv_gain"], w["w_q"], w["w_kv"]]
    if rope:
        nblk = DEC_SEQ // TM_PROJ
        tab_spec = pl.BlockSpec((TM_PROJ, LANES), lambda r: (r % nblk, 0))
        in_specs += [tab_spec, tab_spec]
        args += list(rope_tabs)
    wq = MLA_HEADS * MLA_QK_PAD
    return pl.pallas_call(
        functools.partial(_mla_proj_kernel, rope),
        grid=(N_TOK // TM_PROJ,),
        in_specs=in_specs,
        out_specs=[row(wq), row(wq), row(LANES), row(MLA_KV_LORA), row(MLA_ROPE)],
        out_shape=[jax.ShapeDtypeStruct((N_TOK, wq), BF16), jax.ShapeDtypeStruct((N_TOK, wq), BF16),
                   jax.ShapeDtypeStruct((N_TOK, LANES), BF16), jax.ShapeDtypeStruct((N_TOK, MLA_KV_LORA), F32),
                   jax.ShapeDtypeStruct((N_TOK, MLA_ROPE), F32)],
        compiler_params=_params(("arbitrary",)),
        name="mla_proj",
    )(*args)


def _mm_kernel(a_ref, w_ref, o_ref):
    o_ref[...] = _dot(a_ref[...], w_ref[...]).astype(o_ref.dtype)


def _matmul_bf16(a, w):
    m, k = a.shape
    n = w.shape[1]
    return pl.pallas_call(
        _mm_kernel,
        grid=(m // TM,),
        in_specs=[pl.BlockSpec((TM, k), lambda r: (r, 0)), _full_spec(w.shape)],
        out_specs=pl.BlockSpec((TM, n), lambda r: (r, 0)),
        out_shape=jax.ShapeDtypeStruct((m, n), BF16),
        compiler_params=_params(("arbitrary",)),
        name="matmul",
    )(a, w)


MLA_SCALE = (MLA_NOPE + MLA_ROPE) ** -0.5


def _mla_attn_ctx_kernel(q_ref, kv_ref, kpe_ref, o_ref):
    kpe = kpe_ref[...]
    for hd in range(MLA_HEADS):
        a = hd * MLA_QK_PAD
        qh = q_ref[:, a:a + MLA_QK_PAD]
        kh = jnp.concatenate([kv_ref[:, a:a + LANES], kpe], axis=1)
        vh = kv_ref[:, a + LANES:a + 2 * LANES]
        s = _dot_nt(qh, kh) * MLA_SCALE
        p = jnp.exp(s - jnp.max(s, axis=-1, keepdims=True))
        l = jnp.sum(p, axis=-1, keepdims=True)
        o_ref[:, hd * MLA_V:(hd + 1) * MLA_V] = (_dot(p.astype(BF16), vh) / l).astype(BF16)


def _mla_attn_ctx(q, kv, kpe):
    wq = MLA_HEADS * MLA_QK_PAD
    blk = lambda w: pl.BlockSpec((CTX_SEQ, w), lambda b: (b, 0))
    return pl.pallas_call(
        _mla_attn_ctx_kernel,
        grid=(CTX_BATCH,),
        in_specs=[blk(wq), blk(wq), blk(LANES)],
        out_specs=blk(MLA_HEADS * MLA_V),
        out_shape=jax.ShapeDtypeStruct((N_TOK, MLA_HEADS * MLA_V), BF16),
        compiler_params=_params(("arbitrary",)),
        name="mla_attn_ctx",
    )(q, kv, kpe)


def _mla_attn_lat_kernel(q_ref, kvl_ref, kpel_ref, kvc_ref, kpec_ref, o_ref):
    kpel = kpel_ref[...]
    kpec = kpec_ref[...]
    for hd in range(MLA_HEADS):
        a = hd * MLA_QK_PAD
        qh = q_ref[:, a:a + MLA_QK_PAD]
        kc = jnp.concatenate([kvc_ref[:, a:a + LANES], kpec], axis=1)
        kl = jnp.concatenate([kvl_ref[:, a:a + LANES], kpel], axis=1)
        sc = _dot_nt(qh, kc) * MLA_SCALE
        sl = _dot_nt(qh, kl) * MLA_SCALE
        m = jnp.maximum(jnp.max(sc, axis=-1, keepdims=True), jnp.max(sl, axis=-1, keepdims=True))
        pc = jnp.exp(sc - m)
        plat = jnp.exp(sl - m)
        l = jnp.sum(pc, axis=-1, keepdims=True) + jnp.sum(plat, axis=-1, keepdims=True)
        o = _dot(pc.astype(BF16), kvc_ref[:, a + LANES:a + 2 * LANES]) \
            + _dot(plat.astype(BF16), kvl_ref[:, a + LANES:a + 2 * LANES])
        o_ref[:, hd * MLA_V:(hd + 1) * MLA_V] = (o / l).astype(BF16)


def _mla_attn_lat(q, kv, kpe, kv_ctx, kpe_ctx):
    wq = MLA_HEADS * MLA_QK_PAD
    nq = DEC_SEQ // TQ_LAT
    return pl.pallas_call(
        _mla_attn_lat_kernel,
        grid=(DEC_BATCH, nq),
        in_specs=[pl.BlockSpec((TQ_LAT, wq), lambda b, i: (b * nq + i, 0)),
                  pl.BlockSpec((DEC_SEQ, wq), lambda b, i: (b, 0)),
                  pl.BlockSpec((DEC_SEQ, LANES), lambda b, i: (b, 0)),
                  pl.BlockSpec((PAST_LEN, wq), lambda b, i: (b, 0)),
                  pl.BlockSpec((PAST_LEN, LANES), lambda b, i: (b, 0))],
        out_specs=pl.BlockSpec((TQ_LAT, MLA_HEADS * MLA_V), lambda b, i: (b * nq + i, 0)),
        out_shape=jax.ShapeDtypeStruct((N_TOK, MLA_HEADS * MLA_V), BF16),
        compiler_params=_params(("arbitrary", "arbitrary")),
        name="mla_attn_lat",
    )(q, kv, kpe, kv_ctx, kpe_ctx)


def _gmlp_kernel(x_ref, sc_ref, sh_ref, win_ref, vg_ref, ws_ref, bs_ref, p_ref):
    h = (x_ref[...] * (1.0 + sc_ref[0]) + sh_ref[0]).astype(BF16)
    zv = _gelu_tanh(_dot(h, win_ref[:, GM_HALF:]))
    mu = jnp.mean(zv, axis=-1, keepdims=True)
    d = zv - mu
    var = jnp.mean(d * d, axis=-1, keepdims=True)
    vn = (d * lax.rsqrt(var + EPS) * vg_ref[...]).astype(BF16)
    for g in range(GM_GROUPS):
        c0 = g * GM_GW
        u = _gelu_tanh(_dot(h, win_ref[:, c0:c0 + GM_GW]))
        mixed = jnp.concatenate(
            [_dot(ws_ref[g], vn[k * CHUNK:(k + 1) * CHUNK, c0:c0 + GM_GW]) for k in range(TM // CHUNK)], axis=0)
        bias = jnp.concatenate([bs_ref[g]] * (TM // CHUNK), axis=0)
        p_ref[:, c0:c0 + GM_GW] = (u * (mixed + bias)).astype(BF16)


def _gmlp_in(x, sc, sh, w, seg_len):
    return pl.pallas_call(
        _gmlp_kernel,
        grid=(N_TOK // TM,),
        in_specs=[_row_spec(D_MODEL), _seg_spec(seg_len), _seg_spec(seg_len),
                  _full_spec((D_MODEL, GM_FF)), _full_spec((1, GM_HALF)),
                  _full_spec((GM_GROUPS, CHUNK, CHUNK)), _full_spec((GM_GROUPS, CHUNK, 1))],
        out_specs=_row_spec(GM_HALF),
        out_shape=jax.ShapeDtypeStruct((N_TOK, GM_HALF), BF16),
        compiler_params=_params(("arbitrary",)),
        name="gmlp_in",
    )(x, sc, sh, w["w_in"], w["v_gain"], w["w_s"], w["b_s"])


SWA_SCALE = SWA_HEAD_DIM ** -0.5
SWA_QW = SWA_HEADS * SWA_HEAD_DIM
SWA_KW = SWA_KV_HEADS * 2 * LANES
SWA_VW = SWA_KV_HEADS * LANES
SWA_GROUP = SWA_KV_HEADS


def _swa_proj_kernel(rope, x_ref, sc_ref, sh_ref, w_ref, *rest):
    if rope:
        cos_ref, sin_ref, q_ref, kd_ref, vd_ref, kraw_ref, vraw_ref = rest
    else:
        q_ref, kd_ref, vd_ref, kraw_ref, vraw_ref = rest
    h = (x_ref[...] * (1.0 + sc_ref[0]) + sh_ref[0]).astype(BF16)
    z = _dot(h, w_ref[...])
    lane = lax.broadcasted_iota(I32, (x_ref.shape[0], LANES), 1)
    low = lane < SWA_HEAD_DIM
    for m in range(SWA_KV_HEADS // 2):
        k0 = z[:, SWA_QW + (2 * m) * 2 * LANES:SWA_QW + (2 * m) * 2 * LANES + LANES]
        k1 = z[:, SWA_QW + (2 * m + 1) * 2 * LANES + LANES:SWA_QW + (2 * m + 2) * 2 * LANES]
        kraw_ref[:, m * LANES:(m + 1) * LANES] = jnp.where(low, k0, k1)
        v0 = z[:, SWA_QW + SWA_KW + (2 * m) * LANES:SWA_QW + SWA_KW + (2 * m + 1) * LANES]
        v1 = z[:, SWA_QW + SWA_KW + (2 * m + 1) * LANES:SWA_QW + SWA_KW + (2 * m + 2) * LANES]
        vraw_ref[:, m * LANES:(m + 1) * LANES] = jnp.where(low, v0, v1)
    vd_ref[...] = z[:, SWA_QW + SWA_KW:].astype(BF16)
    if rope:
        cos, sin = cos_ref[...], sin_ref[...]
        for j in range(SWA_QW // LANES):
            q_ref[:, j * LANES:(j + 1) * LANES] = _rope128(z[:, j * LANES:(j + 1) * LANES], cos, sin).astype(BF16)
        for j in range(SWA_KW // LANES):
            a = SWA_QW + j * LANES
            kd_ref[:, j * LANES:(j + 1) * LANES] = _rope128(z[:, a:a + LANES], cos, sin).astype(BF16)
    else:
        q_ref[...] = z[:, :SWA_QW].astype(BF16)
        kd_ref[...] = z[:, SWA_QW:SWA_QW + SWA_KW].astype(BF16)


def _swa_proj(x, sc, sh, w_ext, seg_len, rope_tabs):
    rope = rope_tabs is not None
    seg, row = _seg_spec(seg_len, TM_PROJ), functools.partial(_row_spec, tm=TM_PROJ)
    in_specs = [row(D_MODEL), seg, seg, _full_spec(w_ext.shape)]
    args = [x, sc, sh, w_ext]
    if rope:
        nblk = DEC_SEQ // TM_PROJ
        tab_spec = pl.BlockSpec((TM_PROJ, LANES), lambda r: (r % nblk, 0))
        in_specs += [tab_spec, tab_spec]
        args += list(rope_tabs)
    kvw = SWA_KV_HEADS * SWA_HEAD_DIM
    return pl.pallas_call(
        functools.partial(_swa_proj_kernel, rope),
        grid=(N_TOK // TM_PROJ,),
        in_specs=in_specs,
        out_specs=[row(SWA_QW), row(SWA_KW), row(SWA_VW), row(kvw), row(kvw)],
        out_shape=[jax.ShapeDtypeStruct((N_TOK, SWA_QW), BF16), jax.ShapeDtypeStruct((N_TOK, SWA_KW), BF16),
                   jax.ShapeDtypeStruct((N_TOK, SWA_VW), BF16), jax.ShapeDtypeStruct((N_TOK, kvw), F32),
                   jax.ShapeDtypeStruct((N_TOK, kvw), F32)],
        compiler_params=_params(("arbitrary",)),
        name="swa_proj",
    )(*args)


def _swa_heads(q_ref, sink_ref, o_ref, score_fn, value_fn):
    rows = q_ref.shape[0]
    lane = lax.broadcasted_iota(I32, (rows, LANES), 1)
    low = lane < SWA_HEAD_DIM
    for hk0 in range(0, SWA_KV_HEADS, SWA_GROUP):
        stacked, sinks = None, []
        for hk in range(hk0, hk0 + SWA_GROUP):
            c0 = hk * 2 * LANES
            q2 = jnp.concatenate([q_ref[:, c0:c0 + LANES], q_ref[:, c0 + LANES:c0 + 2 * LANES]], axis=0)
            for par in range(2):
                for pr in range(2):
                    hd = hk * 4 + pr * 2 + par
                    sinks.append(jnp.broadcast_to(sink_ref[:, hd:hd + 1], (rows, 1)))
                blocks = score_fn(hk, par, q2)
                if stacked is None:
                    stacked = [[] for _ in blocks]
                for kind, s in zip(stacked, blocks):
                    kind.append(s)
        scores = [jnp.concatenate(kind, axis=0) for kind in stacked]
        sk = jnp.concatenate(sinks, axis=0)
        m = sk
        for s in scores:
            m = jnp.maximum(m, jnp.max(s, axis=-1, keepdims=True))
        probs = [jnp.exp(s - m) for s in scores]
        l = jnp.exp(sk - m)
        for p in probs:
            l = l + jnp.sum(p, axis=-1, keepdims=True)
        inv = 1.0 / l
        n = 0
        for hk in range(hk0, hk0 + SWA_GROUP):
            res = []
            for par in range(2):
                blk = slice(n * 2 * rows, (n + 1) * 2 * rows)
                res.append(value_fn(hk, [p[blk] for p in probs]) * inv[blk])
                n += 1
            for pr in range(2):
                c0 = hk * 2 * LANES + pr * LANES
                part = slice(pr * rows, (pr + 1) * rows)
                o_ref[:, c0:c0 + LANES] = jnp.where(low, res[0][part], res[1][part]).astype(BF16)


def _swa_attn_ctx_kernel(q_ref, kd_ref, vd_ref, sink_ref, o_ref):
    def score_fn(hk, par, qp):
        a = hk * 2 * LANES + par * LANES
        return [_dot_nt(qp, kd_ref[:, a:a + LANES]) * SWA_SCALE]

    def value_fn(hk, probs):
        return _dot(probs[0].astype(BF16), vd_ref[:, hk * LANES:(hk + 1) * LANES])

    _swa_heads(q_ref, sink_ref, o_ref, score_fn, value_fn)


def _swa_attn_ctx(q, kd, vd, sink):
    blk = lambda w: pl.BlockSpec((CTX_SEQ, w), lambda b: (b, 0))
    return pl.pallas_call(
        _swa_attn_ctx_kernel,
        grid=(CTX_BATCH,),
        in_specs=[blk(SWA_QW), blk(SWA_KW), blk(SWA_VW), pl.BlockSpec((1, SWA_HEADS), lambda b: (0, 0))],
        out_specs=blk(SWA_QW),
        out_shape=jax.ShapeDtypeStruct((N_TOK, SWA_QW), BF16),
        compiler_params=_params(("arbitrary",)),
        name="swa_attn_ctx",
    )(q, kd, vd, sink)


SWA_WIN = 3 * BAND


def _swa_attn_lat_kernel(q_ref, kdl_ref, vdl_ref, kdc_ref, vdc_ref, sink_ref, o_ref):
    nb = pl.program_id(1)
    start = pl.multiple_of(jnp.clip((nb - 1) * BAND, 0, DEC_SEQ - SWA_WIN), BAND)
    qpos = nb * BAND + (lax.broadcasted_iota(I32, (2 * BAND, SWA_WIN), 0) & (BAND - 1))
    kpos = start + lax.broadcasted_iota(I32, (2 * BAND, SWA_WIN), 1)
    valid = jnp.abs(qpos - kpos) <= WINDOW

    def score_fn(hk, par, qp):
        a = hk * 2 * LANES + par * LANES
        kl = kdl_ref[pl.ds(start, SWA_WIN), a:a + LANES]
        sl = jnp.where(valid, _dot_nt(qp, kl) * SWA_SCALE, NEG_INF)
        sc = _dot_nt(qp, kdc_ref[:, a:a + LANES]) * SWA_SCALE
        return [sl, sc]

    def value_fn(hk, probs):
        vl = vdl_ref[pl.ds(start, SWA_WIN), hk * LANES:(hk + 1) * LANES]
        return _dot(probs[0].astype(BF16), vl) + _dot(probs[1].astype(BF16), vdc_ref[:, hk * LANES:(hk + 1) * LANES])

    _swa_heads(q_ref, sink_ref, o_ref, score_fn, value_fn)


def _swa_attn_lat(q, kd, vd, kd_ctx, vd_ctx, sink):
    nbands = DEC_SEQ // BAND
    return pl.pallas_call(
        _swa_attn_lat_kernel,
        grid=(DEC_BATCH, nbands),
        in_specs=[pl.BlockSpec((BAND, SWA_QW), lambda b, i: (b * nbands + i, 0)),
                  pl.BlockSpec((DEC_SEQ, SWA_KW), lambda b, i: (b, 0)),
                  pl.BlockSpec((DEC_SEQ, SWA_VW), lambda b, i: (b, 0)),
                  pl.BlockSpec((PAST_LEN, SWA_KW), lambda b, i: (b, 0)),
                  pl.BlockSpec((PAST_LEN, SWA_VW), lambda b, i: (b, 0)),
                  pl.BlockSpec((1, SWA_HEADS), lambda b, i: (0, 0))],
        out_specs=pl.BlockSpec((BAND, SWA_QW), lambda b, i: (b * nbands + i, 0)),
        out_shape=jax.ShapeDtypeStruct((N_TOK, SWA_QW), BF16),
        compiler_params=_params(("arbitrary", "arbitrary")),
        name="swa_attn_lat",
    )(q, kd, vd, kd_ctx, vd_ctx, sink)


def _post_kernel(a_ref, w_ref, x_ref, g1_ref, sc2_ref, sh2_ref, lng_ref, lnb_ref, rt_ref, x1_ref, h2_ref, lg_ref):
    y = _dot(a_ref[...], w_ref[...])
    x1 = _layer_norm(ALPHA * x_ref[...] + g1_ref[0] * y, lng_ref[...], lnb_ref[...])
    x1_ref[...] = x1
    h2 = x1 * (1.0 + sc2_ref[0]) + sh2_ref[0]
    for j in range(ROW_TILES):
        h2_ref[pl.ds(j, TM, stride=ROW_TILES), :] = h2[:, j * LANES:(j + 1) * LANES]
    rt = rt_ref[...]
    h_hi = h2.astype(BF16)
    h_lo = (h2 - h_hi.astype(F32)).astype(BF16)
    r_hi = rt.astype(BF16)
    r_lo = (rt - r_hi.astype(F32)).astype(BF16)
    both = _dot(h_hi, jnp.concatenate([r_hi, r_lo], axis=1))
    lg_ref[...] = (both[:, :LANES] + both[:, LANES:]) + _dot(h_lo, r_hi)


def _post(a, w_out, x, g1, sc2, sh2, lng, lnb, router_pad, seg_len):
    k = a.shape[1]
    return pl.pallas_call(
        _post_kernel,
        grid=(N_TOK // TM,),
        in_specs=[_row_spec(k), _full_spec((k, D_MODEL)), _row_spec(D_MODEL),
                  _seg_spec(seg_len), _seg_spec(seg_len), _seg_spec(seg_len),
                  _full_spec((1, D_MODEL)), _full_spec((1, D_MODEL)), _full_spec((D_MODEL, LANES))],
        out_specs=[_row_spec(D_MODEL), pl.BlockSpec((TM * ROW_TILES, LANES), lambda r: (r, 0)), _row_spec(LANES)],
        out_shape=[jax.ShapeDtypeStruct((N_TOK, D_MODEL), F32), jax.ShapeDtypeStruct((N_TOK * ROW_TILES, LANES), F32),
                   jax.ShapeDtypeStruct((N_TOK, LANES), F32)],
        compiler_params=_params(("arbitrary",)),
        name="mixer_out",
    )(a, w_out, x, g1, sc2, sh2, lng, lnb, router_pad)


CUM_BLK = 256
REFINE_STEPS = 16
F32_MIN_NORMAL = 1.1754943508222875e-38


def _excl_cumsum_lanes(a, upper):
    outs = []
    carry = jnp.zeros((a.shape[0], 1), F32)
    for b in range(N_TOK // CUM_BLK):
        blk = a[:, b * CUM_BLK:(b + 1) * CUM_BLK]
        outs.append(_dot(blk.astype(BF16), upper) + carry)
        carry = carry + jnp.sum(blk, axis=1, keepdims=True)
    return jnp.concatenate(outs, axis=1)


def _route_kernel(lg_ref, r_ref, tok_ref, roff_ref, idx_ref, rank_scr, w_scr):
    e_rows = N_EXPERTS
    lt = lg_ref[...].T[:e_rows]
    ex = jnp.exp(lt - jnp.max(lt, axis=0, keepdims=True))
    aff = ex / jnp.sum(ex, axis=0, keepdims=True)
    def count_ge(thr):
        return jnp.sum(jnp.where(aff >= thr, 1.0, 0.0), axis=1, keepdims=True)

    cur = jnp.zeros((e_rows, 1), I32)
    for b in range(30, -1, -1):
        cand = cur | jnp.int32(1 << b)
        cur = jnp.where(count_ge(lax.bitcast_convert_type(cand, F32)) >= float(CAP), cand, cur)
    lo = lax.bitcast_convert_type(cur, F32)
    hi = jnp.maximum(lax.bitcast_convert_type(cur + 1, F32), F32_MIN_NORMAL)
    for _ in range(REFINE_STEPS):
        w = hi - lo
        t1, t2, t3 = lo + 0.25 * w, lo + 0.5 * w, lo + 0.75 * w
        ok1, ok2, ok3 = (count_ge(t) >= float(CAP) for t in (t1, t2, t3))
        lo, hi = (jnp.where(ok3, t3, jnp.where(ok2, t2, jnp.where(ok1, t1, lo))),
                  jnp.where(ok1, jnp.where(ok2, jnp.where(ok3, hi, t3), t2), t1))
    gt = aff >= hi
    eq = (aff >= lo) & (aff < hi)
    need = float(CAP) - jnp.sum(jnp.where(gt, 1.0, 0.0), axis=1, keepdims=True)

    ri = lax.broadcasted_iota(I32, (CUM_BLK, CUM_BLK), 0)
    ci = lax.broadcasted_iota(I32, (CUM_BLK, CUM_BLK), 1)
    upper = jnp.where(ri < ci, 1.0, 0.0).astype(BF16)

    tie_rank = _excl_cumsum_lanes(jnp.where(eq, 1.0, 0.0), upper)
    sel = gt | (eq & (tie_rank < need))
    msk = jnp.where(sel, 1.0, 0.0)

    rank = _excl_cumsum_lanes(msk, upper)
    rank_scr[...] = jnp.where(sel, rank, -1.0)

    inv = 1.0 / LANES
    slot_no = jnp.where(sel, rank, -float(LANES))
    slot_hi = jnp.floor(slot_no * inv)
    g0 = aff.astype(BF16).astype(F32)
    r1 = aff - g0
    g1 = r1.astype(BF16).astype(F32)
    g2 = (r1 - g1).astype(BF16).astype(F32)
    tt = lax.broadcasted_iota(I32, (1, N_TOK), 1).astype(F32)
    t_hi = jnp.floor(tt * inv)
    misc = jnp.concatenate([t_hi, tt - t_hi * LANES, jnp.zeros((6, N_TOK), F32)], axis=0)
    feat = jnp.concatenate([slot_hi, slot_no - slot_hi * LANES, g0, g1, g2, misc,
                            jnp.zeros((LANES - 5 * e_rows - 8, N_TOK), F32)], axis=0)
    table = feat.T
    tok_ref[...] = table
    w_scr[...] = table.astype(BF16)

    nblk = N_TOK // TOK_BLK
    bt = lax.broadcasted_iota(I32, (N_TOK, LANES), 0) // TOK_BLK
    bb = lax.broadcasted_iota(I32, (N_TOK, LANES), 1)
    before = jnp.where((bt < bb) & (bb <= nblk), 1.0, 0.0).astype(BF16)
    roff_ref[...] = _dot(msk.astype(BF16), before).astype(I32)

    sub = lax.broadcasted_iota(I32, (8, LANES), 0)
    ln8 = lax.broadcasted_iota(I32, (8, LANES), 1)
    rb_rows = 128

    def per_expert(e, carry):
        rk = rank_scr[pl.ds(e, 1), :]
        want = jnp.where(sub < 5, sub * e_rows + e, sub + (F_T_HI - 5))
        pick = jnp.where(ln8 == want, 1.0, 0.0).astype(BF16)
        for rb in range(CAP // rb_rows):
            slot = (rb * rb_rows + lax.broadcasted_iota(I32, (rb_rows, 1), 0)).astype(F32)
            onehot = jnp.where(rk == slot, 1.0, 0.0).astype(BF16)
            rows = _dot(onehot, w_scr[...])
            r_ref[pl.ds(e, 1), rb * rb_rows:(rb + 1) * rb_rows, :] = rows[None]
            got = _dot_nt(pick, rows.astype(BF16))
            idx_ref[pl.ds(e, 1), :, rb * rb_rows:(rb + 1) * rb_rows] = (got[5:6] * LANES + got[6:7]).astype(I32)[None]
        return carry

    lax.fori_loop(0, e_rows, per_expert, 0)


def _route(logits):
    return pl.pallas_call(
        _route_kernel,
        grid=(1,),
        in_specs=[pl.BlockSpec((N_TOK, LANES), lambda i: (0, 0))],
        out_specs=[pl.BlockSpec((N_EXPERTS, CAP, LANES), lambda i: (0, 0, 0)),
                   pl.BlockSpec((N_TOK, LANES), lambda i: (0, 0)),
                   pl.BlockSpec((N_EXPERTS, LANES), lambda i: (0, 0)),
                   pl.BlockSpec((N_EXPERTS, 1, CAP), lambda i: (0, 0, 0))],
        out_shape=[jax.ShapeDtypeStruct((N_EXPERTS, CAP, LANES), F32),
                   jax.ShapeDtypeStruct((N_TOK, LANES), F32),
                   jax.ShapeDtypeStruct((N_EXPERTS, LANES), I32),
                   jax.ShapeDtypeStruct((N_EXPERTS, 1, CAP), I32)],
        scratch_shapes=[pltpu.VMEM((N_EXPERTS, N_TOK), F32), pltpu.VMEM((N_TOK, LANES), BF16)],
        compiler_params=_params(("arbitrary",)),
        name="moe_route",
    )(logits)


ROWS_E = 2 * CAP
N_FT = EXPERT_FF // FF_TILE
GATHER_UNROLL = 8
PIECE = 256
N_PIECE = FF_TILE // PIECE


def _tile(row):
    return pl.ds(pl.multiple_of(row * ROW_TILES, ROW_TILES), ROW_TILES)


def _ffn_kernel(idx_s, hc_hbm, hl_hbm, rc_ref, rl_ref, wg_ref, wu_ref, wd_ref, y_ref, xe, xb, acc, gsem):
    e = pl.program_id(0)
    f = pl.program_id(1)

    def gather_rows(ee):
        base = ee * ROWS_E

        def body(r, carry):
            tc = idx_s[base + r]
            pltpu.make_async_copy(hc_hbm.at[_tile(tc)], xe.at[_tile(r)], gsem.at[0]).start(priority=0)
            tl = idx_s[base + CAP + r]
            pltpu.make_async_copy(hl_hbm.at[_tile(tl)], xe.at[_tile(CAP + r)], gsem.at[0]).start(priority=1)
            return carry

        lax.fori_loop(0, CAP, body, 0, unroll=GATHER_UNROLL)

    @pl.when(f == 0)
    def _():
        @pl.when(e == 0)
        def _():
            gather_rows(0)

        pltpu.make_async_copy(xe, xe, gsem.at[0]).wait()
        for j in range(ROW_TILES):
            xb[:, j * LANES:(j + 1) * LANES] = xe[pl.ds(j, ROWS_E, stride=ROW_TILES), :].astype(BF16)

        @pl.when(e + 1 < N_EXPERTS)
        def _():
            gather_rows(e + 1)

        acc[...] = jnp.zeros_like(acc)

    x = xb[...]
    for p in range(N_PIECE):
        cols = slice(p * PIECE, (p + 1) * PIECE)
        g = _dot(x, wg_ref[0, :, cols].astype(BF16))
        u = _dot(x, wu_ref[0, :, cols].astype(BF16))
        hid = (g * _sigmoid(g) * u).astype(BF16)
        acc[...] += _dot(hid, wd_ref[0, cols, :].astype(BF16))

    @pl.when(f == N_FT - 1)
    def _():
        lane = lax.broadcasted_iota(I32, (CAP, LANES), 1)
        mine = (lane == F_G0 + e) | (lane == F_G1 + e) | (lane == F_G2 + e)
        gate = jnp.concatenate(
            [jnp.sum(jnp.where(mine, rc_ref[0], 0.0), axis=1, keepdims=True),
             jnp.sum(jnp.where(mine, rl_ref[0], 0.0), axis=1, keepdims=True)], axis=0)
        y_ref[0] = (acc[...] * gate).astype(BF16)


def _moe_ffn(layer, idx_all, h_ctx, h_lat, r_ctx, r_lat, w_gate, w_up, w_down):
    any_spec = pl.BlockSpec(memory_space=pl.ANY)
    e0 = layer * N_EXPERTS
    grid_spec = pltpu.PrefetchScalarGridSpec(
        num_scalar_prefetch=1,
        grid=(N_EXPERTS, N_FT),
        in_specs=[any_spec, any_spec,
                  pl.BlockSpec((1, CAP, LANES), lambda e, f, *_: (e, 0, 0)),
                  pl.BlockSpec((1, CAP, LANES), lambda e, f, *_: (e, 0, 0)),
                  pl.BlockSpec((1, D_MODEL, FF_TILE), lambda e, f, *_: (e0 + e, 0, f)),
                  pl.BlockSpec((1, D_MODEL, FF_TILE), lambda e, f, *_: (e0 + e, 0, f)),
                  pl.BlockSpec((1, FF_TILE, D_MODEL), lambda e, f, *_: (e0 + e, f, 0))],
        out_specs=pl.BlockSpec((1, ROWS_E, D_MODEL), lambda e, f, *_: (e, 0, 0)),
        scratch_shapes=[pltpu.VMEM((ROWS_E * ROW_TILES, LANES), F32), pltpu.VMEM((ROWS_E, D_MODEL), BF16),
                        pltpu.VMEM((ROWS_E, D_MODEL), F32), pltpu.SemaphoreType.DMA((1,))],
    )
    return pl.pallas_call(
        _ffn_kernel,
        grid_spec=grid_spec,
        out_shape=jax.ShapeDtypeStruct((N_EXPERTS, ROWS_E, D_MODEL), BF16),
        compiler_params=_params(("arbitrary", "arbitrary")),
        name="moe_ffn",
    )(idx_all, h_ctx, h_lat, r_ctx, r_lat, w_gate, w_up, w_down)


WIN_E = 64
EXPERTS_PER_DOT = 4
N_TBLK = N_TOK // TOK_BLK
ROFF_STRIDE = LANES


def _combine_kernel(group, roff_s, y_hbm, tok_ref, x_ref, g2_ref, lng_ref, lnb_ref, o_ref, ywin, yext, acc, wsem, esem):
    b = pl.program_id(0)

    def win_start(e, bb):
        first = roff_s[e * ROFF_STRIDE + bb]
        return jnp.minimum((first // BF16_ROWS) * BF16_ROWS, CAP - WIN_E)

    def rows_at(start):
        return pl.ds(pl.multiple_of(group * CAP + start, BF16_ROWS), WIN_E)

    def start_windows(bb, slot):
        for e in range(N_EXPERTS):
            pltpu.make_async_copy(y_hbm.at[e, rows_at(win_start(e, bb))], ywin.at[slot, e], wsem.at[slot]).start()

    @pl.when(b == 0)
    def _():
        start_windows(0, 0)

    @pl.when(b + 1 < N_TBLK)
    def _():
        start_windows(b + 1, (b + 1) % 2)

    slot = b % 2
    tok = tok_ref[...]
    lane = lax.broadcasted_iota(I32, (TOK_BLK, WIN_E), 1)
    owner = [tok[:, F_SLOT_HI + e:F_SLOT_HI + e + 1] * LANES + tok[:, F_SLOT_LO + e:F_SLOT_LO + e + 1]
             for e in range(N_EXPERTS)]
    starts = [win_start(e, b) for e in range(N_EXPERTS)]

    pltpu.make_async_copy(ywin.at[slot], ywin.at[slot], wsem.at[slot]).wait()
    wide = lax.broadcasted_iota(I32, (TOK_BLK, EXPERTS_PER_DOT * WIN_E), 1)
    total = None
    for e0 in range(0, N_EXPERTS, EXPERTS_PER_DOT):
        own, slots = owner[e0], (starts[e0] + wide).astype(F32)
        for k in range(1, EXPERTS_PER_DOT):
            later = wide >= k * WIN_E
            own = jnp.where(later, owner[e0 + k], own)
            slots = jnp.where(later, (starts[e0 + k] - k * WIN_E + wide).astype(F32), slots)
        seg = jnp.where(own == slots, 1.0, 0.0).astype(BF16)
        rows = ywin[slot, e0:e0 + EXPERTS_PER_DOT].reshape(EXPERTS_PER_DOT * WIN_E, D_MODEL)
        part = _dot(seg, rows)
        total = part if total is None else total + part
    acc[...] = total

    for e in range(N_EXPERTS):
        past = jnp.maximum(roff_s[e * ROFF_STRIDE + b + 1] - (starts[e] + WIN_E), 0)

        def over(k, carry, e=e):
            nominal = starts[e] + (k + 1) * WIN_E
            first = jnp.minimum(nominal, CAP - WIN_E)
            cp = pltpu.make_async_copy(y_hbm.at[e, rows_at(first)], yext, esem.at[0])
            cp.start()
            cp.wait()
            slots = first + lane
            seg = jnp.where((owner[e] == slots.astype(F32)) & (slots >= nominal), 1.0, 0.0).astype(BF16)
            acc[...] += _dot(seg, yext[...])
            return carry

        lax.fori_loop(0, (past + WIN_E - 1) // WIN_E, over, 0)

    o_ref[...] = _layer_norm(ALPHA * x_ref[...] + g2_ref[0] * acc[...], lng_ref[...], lnb_ref[...])


def _combine(group, roff, y, tok, x1, g2, lng, lnb, seg_len):
    grid_spec = pltpu.PrefetchScalarGridSpec(
        num_scalar_prefetch=1,
        grid=(N_TBLK,),
        in_specs=[pl.BlockSpec(memory_space=pl.ANY),
                  pl.BlockSpec((TOK_BLK, LANES), lambda b, *_: (b, 0)),
                  pl.BlockSpec((TOK_BLK, D_MODEL), lambda b, *_: (b, 0)),
                  pl.BlockSpec((1, 1, D_MODEL), lambda b, *_: (b * TOK_BLK // seg_len, 0, 0)),
                  pl.BlockSpec((1, D_MODEL), lambda b, *_: (0, 0)),
                  pl.BlockSpec((1, D_MODEL), lambda b, *_: (0, 0))],
        out_specs=pl.BlockSpec((TOK_BLK, D_MODEL), lambda b, *_: (b, 0)),
        scratch_shapes=[pltpu.VMEM((2, N_EXPERTS, WIN_E, D_MODEL), BF16), pltpu.VMEM((WIN_E, D_MODEL), BF16),
                        pltpu.VMEM((TOK_BLK, D_MODEL), F32),
                        pltpu.SemaphoreType.DMA((2,)), pltpu.SemaphoreType.DMA((1,))],
    )
    return pl.pallas_call(
        functools.partial(_combine_kernel, group),
        grid_spec=grid_spec,
        out_shape=jax.ShapeDtypeStruct((N_TOK, D_MODEL), F32),
        compiler_params=_params(("arbitrary",)),
        name="moe_combine",
    )(roff, y, tok, x1, g2, lng, lnb)


def _rope_tables():
    t = jnp.arange(DEC_SEQ)
    n_freq = MLA_ROPE // 4
    inv_freq = ROPE_BASE ** (-jnp.arange(n_freq, dtype=F32) / n_freq)
    ang_r = (t // GRID_W).astype(F32)[:, None] * inv_freq[None, :]
    ang_c = (t % GRID_W).astype(F32)[:, None] * inv_freq[None, :]
    cos64 = jnp.concatenate([jnp.cos(ang_r)] * 2 + [jnp.cos(ang_c)] * 2, axis=1)
    sin64 = jnp.concatenate([jnp.sin(ang_r)] * 2 + [jnp.sin(ang_c)] * 2, axis=1)
    one, zero = jnp.ones_like(cos64), jnp.zeros_like(sin64)
    mla = (jnp.concatenate([cos64, one], axis=1), jnp.concatenate([sin64, zero], axis=1))
    swa = (jnp.concatenate([cos64, cos64], axis=1), jnp.concatenate([sin64, sin64], axis=1))
    return mla, swa


def _swa_key_layout(k):
    z = jnp.zeros_like(k)
    return jnp.stack([k, z, z, k], axis=2).reshape(k.shape[0], SWA_KW)


def _swa_value_layout(v):
    return jnp.stack([v, v], axis=2).reshape(v.shape[0], SWA_VW)


def kernel(x_prompt, x_sample, cache_mla_ckv, cache_mla_kpe, cache_swa_k, cache_swa_v, c, c_ctx, mod_w, mod_b, ln_gain, ln_bias, mla_w_in, mla_q_gain, mla_kv_gain, mla_w_q_up, mla_w_kv_up, mla_w_out, gm_w_in, gm_v_gain, gm_w_s, gm_b_s, gm_w_out, swa_w_qkv, swa_sink, swa_w_out, moe_router, moe_w_gate, moe_w_up, moe_w_down):
    d = D_MODEL
    xs = [x_prompt.reshape(N_TOK, d), x_sample.reshape(N_TOK, d)]
    seg_lens = [N_TOK, DEC_SEQ]
    cond8 = jnp.concatenate([c_ctx[None, :], c, jnp.zeros((3, d), F32)], axis=0)
    mod = _modulation_all(cond8, mod_w, mod_b)
    rope_mla, rope_swa = _rope_tables()
    w_gate_all = moe_w_gate.reshape(DEPTH * N_EXPERTS, d, EXPERT_FF)
    w_up_all = moe_w_up.reshape(DEPTH * N_EXPERTS, d, EXPERT_FF)
    w_down_all = moe_w_down.reshape(DEPTH * N_EXPERTS, EXPERT_FF, d)

    def mod_rows(i, q):
        m = mod[i, :, q * d:(q + 1) * d]
        return [m[0:1].reshape(1, 1, d), m[1:1 + DEC_BATCH].reshape(DEC_BATCH, 1, d)]

    ckv_out, kpe_out, k_out, v_out = [], [], [], []
    for i in range(DEPTH):
        kind, j = i % 3, i // 3
        sh1, sc1, g1, sh2, sc2, g2 = (mod_rows(i, q) for q in range(6))
        router_pad = jnp.pad(moe_router[i], ((0, 0), (0, LANES - N_EXPERTS)))
        lng = ln_gain[i].reshape(2, 1, d)
        lnb = ln_bias[i].reshape(2, 1, d)

        mixed = []
        if kind == 0:
            w = {
                "w_in": jnp.pad(mla_w_in[j], ((0, 0), (0, MLA_ROPE))).astype(BF16),
                "q_gain": mla_q_gain[j][None, :], "kv_gain": mla_kv_gain[j][None, :],
                "w_q": jnp.pad(mla_w_q_up[j], ((0, 0), (0, 0), (0, MLA_QK_PAD - MLA_NOPE - MLA_ROPE)))
                .reshape(MLA_Q_LORA, MLA_HEADS * MLA_QK_PAD).astype(BF16),
                "w_kv": mla_w_kv_up[j].reshape(MLA_KV_LORA, MLA_HEADS * (MLA_NOPE + MLA_V)).astype(BF16),
            }
            w_out = mla_w_out[j].astype(BF16)
            q, kv, kpe, ckv, kraw = _mla_proj(xs[0], sc1[0], sh1[0], w, seg_lens[0], None)
            ckv_out.append(ckv.reshape(CTX_BATCH, CTX_SEQ, MLA_KV_LORA))
            kpe_out.append(kraw.reshape(CTX_BATCH, CTX_SEQ, MLA_ROPE))
            mixed.append(_mla_attn_ctx(q, kv, kpe))
            q, kv, kpe, _, _ = _mla_proj(xs[1], sc1[1], sh1[1], w, seg_lens[1], rope_mla)
            kv_ctx = _matmul_bf16(cache_mla_ckv[:, j].reshape(DEC_BATCH * PAST_LEN, MLA_KV_LORA).astype(BF16), w["w_kv"])
            kpe_ctx = jnp.pad(cache_mla_kpe[:, j].reshape(DEC_BATCH * PAST_LEN, MLA_ROPE),
                              ((0, 0), (0, LANES - MLA_ROPE))).astype(BF16)
            mixed.append(_mla_attn_lat(q, kv, kpe, kv_ctx, kpe_ctx))
        elif kind == 1:
            w = {"w_in": gm_w_in[j].astype(BF16), "v_gain": gm_v_gain[j][None, :],
                 "w_s": gm_w_s[j].astype(BF16), "b_s": gm_b_s[j][:, :, None]}
            w_out = gm_w_out[j].astype(BF16)
            for g in range(2):
                mixed.append(_gmlp_in(xs[g], sc1[g], sh1[g], w, seg_lens[g]))
        else:
            wq = swa_w_qkv[j]
            nq, nkv = SWA_HEADS * SWA_HEAD_DIM, SWA_KV_HEADS * SWA_HEAD_DIM
            w_ext = jnp.concatenate(
                [wq[:, :nq],
                 _swa_key_layout(wq[:, nq:nq + nkv].reshape(d, SWA_KV_HEADS, SWA_HEAD_DIM)),
                 _swa_value_layout(wq[:, nq + nkv:].reshape(d, SWA_KV_HEADS, SWA_HEAD_DIM))], axis=1).astype(BF16)
            w_out = swa_w_out[j].astype(BF16)
            sink = swa_sink[j][None, :]
            q, kd, vd, kraw, vraw = _swa_proj(xs[0], sc1[0], sh1[0], w_ext, seg_lens[0], None)
            k_out.append(kraw.reshape(CTX_BATCH, CTX_SEQ, SWA_KV_HEADS, SWA_HEAD_DIM))
            v_out.append(vraw.reshape(CTX_BATCH, CTX_SEQ, SWA_KV_HEADS, SWA_HEAD_DIM))
            mixed.append(_swa_attn_ctx(q, kd, vd, sink))
            q, kd, vd, _, _ = _swa_proj(xs[1], sc1[1], sh1[1], w_ext, seg_lens[1], rope_swa)
            rows = DEC_BATCH * PAST_LEN
            kd_ctx = _swa_key_layout(cache_swa_k[:, j].reshape(rows, SWA_KV_HEADS, SWA_HEAD_DIM)).astype(BF16)
            vd_ctx = _swa_value_layout(cache_swa_v[:, j].reshape(rows, SWA_KV_HEADS, SWA_HEAD_DIM)).astype(BF16)
            mixed.append(_swa_attn_lat(q, kd, vd, kd_ctx, vd_ctx, sink))

        x1, h2, routed = [], [], []
        for g in range(2):
            x1_g, h2_g, lg_g = _post(mixed[g], w_out, xs[g], g1[g], sc2[g], sh2[g], lng[0], lnb[0], router_pad, seg_lens[g])
            x1.append(x1_g)
            h2.append(h2_g)
            routed.append(_route(lg_g))
        idx_all = jnp.concatenate([routed[0][3], routed[1][3]], axis=2).reshape(-1)
        y = _moe_ffn(i, idx_all, h2[0], h2[1], routed[0][0], routed[1][0], w_gate_all, w_up_all, w_down_all)
        for g in range(2):
            roff = routed[g][2].reshape(-1)
            xs[g] = _combine(g, roff, y, routed[g][1], x1[g], g2[g], lng[1], lnb[1], seg_lens[g])

    return (xs[0].reshape(CTX_BATCH, CTX_SEQ, d), xs[1].reshape(DEC_BATCH, DEC_SEQ, d),
            jnp.stack(ckv_out, axis=1), jnp.stack(kpe_out, axis=1),
            jnp.stack(k_out, axis=1), jnp.stack(v_out, axis=1))
```

```python
import functools

import jax
import jax.numpy as jnp
from jax import lax
from jax.experimental import pallas as pl
from jax.experimental.pallas import tpu as pltpu

D_MODEL = 1024
N_TOK = 4096
DEPTH = 4
DEC_BATCH = 4
DEC_SEQ = 1024
PAST_LEN = 256
CTX_BATCH = 16
CTX_SEQ = 256
GRID_W = 64
ALPHA = (2.0 * DEPTH) ** 0.25
MLA_HEADS = 8
MLA_Q_LORA = 384
MLA_KV_LORA = 256
MLA_NOPE = 128
MLA_ROPE = 64
MLA_V = 128
MLA_QK_PAD = 256
GM_FF = 6 * D_MODEL
GM_HALF = GM_FF // 2
GM_GROUPS = 8
GM_GW = GM_HALF // GM_GROUPS
CHUNK = 128
SWA_HEADS = 16
SWA_KV_HEADS = 4
SWA_HEAD_DIM = 64
WINDOW = 128
BAND = 128
N_EXPERTS = 16
EXPERT_FF = 2 * D_MODEL
CAP = 2 * N_TOK // N_EXPERTS
ROPE_BASE = 10000.0
NEG_INF = -1e30
EPS = 1e-6
F32 = jnp.float32
BF16 = jnp.bfloat16
I32 = jnp.int32
U32 = jnp.uint32
HIGHEST = lax.Precision.HIGHEST

LANES = 128
SUBLANES = 8
BF16_ROWS = 16
ROW_TILES = D_MODEL // LANES
VMEM_LIMIT = 56 * 1024 * 1024
TM = 256
TQ_LAT = 512
TM_PROJ = 512
FF_TILE = 1024
TOK_BLK = 256
MOD_TN = 1536

F_SLOT_HI, F_SLOT_LO, F_G0, F_G1, F_G2 = 0, 16, 32, 48, 64
F_T_HI, F_T_LO = 80, 81


def _params(sem, vmem=VMEM_LIMIT):
    return pltpu.CompilerParams(dimension_semantics=sem, vmem_limit_bytes=vmem)


def _dot(a, b):
    return jnp.dot(a, b, preferred_element_type=F32)


def _dot_nt(a, b):
    return lax.dot_general(a, b, (((1,), (1,)), ((), ())), preferred_element_type=F32)


def _sigmoid(x):
    return 1.0 / (1.0 + jnp.exp(-x))


def _gelu_tanh(x):
    return 0.5 * x * (1.0 + jnp.tanh(0.7978845608028654 * (x + 0.044715 * (x * x * x))))


def _layer_norm(t, gain, bias):
    mu = jnp.mean(t, axis=-1, keepdims=True)
    d = t - mu
    var = jnp.mean(d * d, axis=-1, keepdims=True)
    return d * lax.rsqrt(var + EPS) * gain + bias


def _rms_norm(t, gain):
    return t * lax.rsqrt(jnp.mean(t * t, axis=-1, keepdims=True) + EPS) * gain


def _rope128(x, cos, sin):
    lane = lax.broadcasted_iota(I32, x.shape, 1)
    first = (lane % 32) < 16
    swapped = jnp.where(first, -pltpu.roll(x, 112, 1), pltpu.roll(x, 16, 1))
    return x * cos + swapped * sin


def _mod_kernel(cond_ref, w_ref, b_ref, o_ref):
    c = cond_ref[...]
    s = c * _sigmoid(c)
    o_ref[0] = jnp.dot(s, w_ref[0], precision=HIGHEST, preferred_element_type=F32) + b_ref[0]


def _modulation_all(cond8, mod_w, mod_b):
    return pl.pallas_call(
        _mod_kernel,
        grid=(DEPTH, 6 * D_MODEL // MOD_TN),
        in_specs=[pl.BlockSpec((8, D_MODEL), lambda i, n: (0, 0)),
                  pl.BlockSpec((1, D_MODEL, MOD_TN), lambda i, n: (i, 0, n)),
                  pl.BlockSpec((1, 1, MOD_TN), lambda i, n: (i, 0, n))],
        out_specs=pl.BlockSpec((1, 8, MOD_TN), lambda i, n: (i, 0, n)),
        out_shape=jax.ShapeDtypeStruct((DEPTH, 8, 6 * D_MODEL), F32),
        compiler_params=_params(("arbitrary", "arbitrary")),
        name="modulation",
    )(cond8, mod_w, mod_b.reshape(DEPTH, 1, 6 * D_MODEL))


def _seg_spec(seg_len, tm=TM):
    return pl.BlockSpec((1, 1, D_MODEL), lambda r: (r * tm // seg_len, 0, 0))


def _row_spec(width, tm=TM):
    return pl.BlockSpec((tm, width), lambda r: (r, 0))


def _full_spec(shape):
    nd = len(shape)
    return pl.BlockSpec(shape, lambda r: (0,) * nd)


def _mla_proj_kernel(rope, x_ref, sc_ref, sh_ref, win_ref, qg_ref, kvg_ref, wq_ref, wkv_ref, *rest):
    if rope:
        cos_ref, sin_ref, q_ref, kv_ref, kpe_ref, ckv_ref, kraw_ref = rest
    else:
        q_ref, kv_ref, kpe_ref, ckv_ref, kraw_ref = rest
    h = (x_ref[...] * (1.0 + sc_ref[0]) + sh_ref[0]).astype(BF16)
    z = _dot(h, win_ref[...])
    cq = _rms_norm(z[:, :MLA_Q_LORA], qg_ref[...])
    ckv = _rms_norm(z[:, MLA_Q_LORA:MLA_Q_LORA + MLA_KV_LORA], kvg_ref[...])
    kp = z[:, MLA_Q_LORA + MLA_KV_LORA:]
    ckv_ref[...] = ckv
    kraw_ref[...] = kp[:, :MLA_ROPE]
    q = _dot(cq.astype(BF16), wq_ref[...])
    kv_ref[...] = _dot(ckv.astype(BF16), wkv_ref[...]).astype(BF16)
    if rope:
        cos, sin = cos_ref[...], sin_ref[...]
        for hd in range(MLA_HEADS):
            a = hd * MLA_QK_PAD
            q_ref[:, a:a + LANES] = q[:, a:a + LANES].astype(BF16)
            q_ref[:, a + LANES:a + 2 * LANES] = _rope128(q[:, a + LANES:a + 2 * LANES], cos, sin).astype(BF16)
        kp = _rope128(kp, cos, sin)
    else:
        q_ref[...] = q.astype(BF16)
    kpe_ref[...] = kp.astype(BF16)


def _mla_proj(x, sc, sh, w, seg_len, rope_tabs):
    rope = rope_tabs is not None
    seg, row = _seg_spec(seg_len, TM_PROJ), functools.partial(_row_spec, tm=TM_PROJ)
    in_specs = [row(D_MODEL), seg, seg,
                _full_spec(w["w_in"].shape), _full_spec((1, MLA_Q_LORA)), _full_spec((1, MLA_KV_LORA)),
                _full_spec(w["w_q"].shape), _full_spec(w["w_kv"].shape)]
    args = [x, sc, sh, w["w_in"], w["q_gain"], w["kv_gain"], w["w_q"], w["w_kv"]]
    if rope:
        nblk = DEC_SEQ // TM_PROJ
        tab_spec = pl.BlockSpec((TM_PROJ, LANES), lambda r: (r % nblk, 0))
        in_specs += [tab_spec, tab_spec]
        args += list(rope_tabs)
    wq = MLA_HEADS * MLA_QK_PAD
    return pl.pallas_call(
        functools.partial(_mla_proj_kernel, rope),
        grid=(N_TOK // TM_PROJ,),
        in_specs=in_specs,
        out_specs=[row(wq), row(wq), row(LANES), row(MLA_KV_LORA), row(MLA_ROPE)],
        out_shape=[jax.ShapeDtypeStruct((N_TOK, wq), BF16), jax.ShapeDtypeStruct((N_TOK, wq), BF16),
                   jax.ShapeDtypeStruct((N_TOK, LANES), BF16), jax.ShapeDtypeStruct((N_TOK, MLA_KV_LORA), F32),
                   jax.ShapeDtypeStruct((N_TOK, MLA_ROPE), F32)],
        compiler_params=_params(("arbitrary",)),
        name="mla_proj",
    )(*args)


def _mm_kernel(a_ref, w_ref, o_ref):
    o_ref[...] = _dot(a_ref[...], w_ref[...]).astype(o_ref.dtype)


def _matmul_bf16(a, w):
    m, k = a.shape
    n = w.shape[1]
    return pl.pallas_call(
        _mm_kernel,
        grid=(m // TM,),
        in_specs=[pl.BlockSpec((TM, k), lambda r: (r, 0)), _full_spec(w.shape)],
        out_specs=pl.BlockSpec((TM, n), lambda r: (r, 0)),
        out_shape=jax.ShapeDtypeStruct((m, n), BF16),
        compiler_params=_params(("arbitrary",)),
        name="matmul",
    )(a, w)


MLA_SCALE = (MLA_NOPE + MLA_ROPE) ** -0.5


def _mla_attn_ctx_kernel(q_ref, kv_ref, kpe_ref, o_ref):
    kpe = kpe_ref[...]
    for hd in range(MLA_HEADS):
        a = hd * MLA_QK_PAD
        qh = q_ref[:, a:a + MLA_QK_PAD]
        kh = jnp.concatenate([kv_ref[:, a:a + LANES], kpe], axis=1)
        vh = kv_ref[:, a + LANES:a + 2 * LANES]
        s = _dot_nt(qh, kh) * MLA_SCALE
        p = jnp.exp(s - jnp.max(s, axis=-1, keepdims=True))
        l = jnp.sum(p, axis=-1, keepdims=True)
        o_ref[:, hd * MLA_V:(hd + 1) * MLA_V] = (_dot(p.astype(BF16), vh) / l).astype(BF16)


def _mla_attn_ctx(q, kv, kpe):
    wq = MLA_HEADS * MLA_QK_PAD
    blk = lambda w: pl.BlockSpec((CTX_SEQ, w), lambda b: (b, 0))
    return pl.pallas_call(
        _mla_attn_ctx_kernel,
        grid=(CTX_BATCH,),
        in_specs=[blk(wq), blk(wq), blk(LANES)],
        out_specs=blk(MLA_HEADS * MLA_V),
        out_shape=jax.ShapeDtypeStruct((N_TOK, MLA_HEADS * MLA_V), BF16),
        compiler_params=_params(("arbitrary",)),
        name="mla_attn_ctx",
    )(q, kv, kpe)


def _mla_attn_lat_kernel(q_ref, kvl_ref, kpel_ref, kvc_ref, kpec_ref, o_ref):
    kpel = kpel_ref[...]
    kpec = kpec_ref[...]
    for hd in range(MLA_HEADS):
        a = hd * MLA_QK_PAD
        qh = q_ref[:, a:a + MLA_QK_PAD]
        kc = jnp.concatenate([kvc_ref[:, a:a + LANES], kpec], axis=1)
        kl = jnp.concatenate([kvl_ref[:, a:a + LANES], kpel], axis=1)
        sc = _dot_nt(qh, kc) * MLA_SCALE
        sl = _dot_nt(qh, kl) * MLA_SCALE
        m = jnp.maximum(jnp.max(sc, axis=-1, keepdims=True), jnp.max(sl, axis=-1, keepdims=True))
        pc = jnp.exp(sc - m)
        plat = jnp.exp(sl - m)
        l = jnp.sum(pc, axis=-1, keepdims=True) + jnp.sum(plat, axis=-1, keepdims=True)
        o = _dot(pc.astype(BF16), kvc_ref[:, a + LANES:a + 2 * LANES]) \
            + _dot(plat.astype(BF16), kvl_ref[:, a + LANES:a + 2 * LANES])
        o_ref[:, hd * MLA_V:(hd + 1) * MLA_V] = (o / l).astype(BF16)


def _mla_attn_lat(q, kv, kpe, kv_ctx, kpe_ctx):
    wq = MLA_HEADS * MLA_QK_PAD
    nq = DEC_SEQ // TQ_LAT
    return pl.pallas_call(
        _mla_attn_lat_kernel,
        grid=(DEC_BATCH, nq),
        in_specs=[pl.BlockSpec((TQ_LAT, wq), lambda b, i: (b * nq + i, 0)),
                  pl.BlockSpec((DEC_SEQ, wq), lambda b, i: (b, 0)),
                  pl.BlockSpec((DEC_SEQ, LANES), lambda b, i: (b, 0)),
                  pl.BlockSpec((PAST_LEN, wq), lambda b, i: (b, 0)),
                  pl.BlockSpec((PAST_LEN, LANES), lambda b, i: (b, 0))],
        out_specs=pl.BlockSpec((TQ_LAT, MLA_HEADS * MLA_V), lambda b, i: (b * nq + i, 0)),
        out_shape=jax.ShapeDtypeStruct((N_TOK, MLA_HEADS * MLA_V), BF16),
        compiler_params=_params(("arbitrary", "arbitrary")),
        name="mla_attn_lat",
    )(q, kv, kpe, kv_ctx, kpe_ctx)


def _gmlp_kernel(x_ref, sc_ref, sh_ref, win_ref, vg_ref, ws_ref, bs_ref, p_ref):
    h = (x_ref[...] * (1.0 + sc_ref[0]) + sh_ref[0]).astype(BF16)
    zv = _gelu_tanh(_dot(h, win_ref[:, GM_HALF:]))
    mu = jnp.mean(zv, axis=-1, keepdims=True)
    d = zv - mu
    var = jnp.mean(d * d, axis=-1, keepdims=True)
    vn = (d * lax.rsqrt(var + EPS) * vg_ref[...]).astype(BF16)
    for g in range(GM_GROUPS):
        c0 = g * GM_GW
        if g % 2 == 0:
            u2 = _gelu_tanh(_dot(h, win_ref[:, c0:c0 + 2 * GM_GW]))
        u = u2[:, (g % 2) * GM_GW:(g % 2 + 1) * GM_GW]
        mixed = jnp.concatenate(
            [_dot(ws_ref[g], vn[k * CHUNK:(k + 1) * CHUNK, c0:c0 + GM_GW]) for k in range(TM // CHUNK)], axis=0)
        bias = jnp.concatenate([bs_ref[g]] * (TM // CHUNK), axis=0)
        p_ref[:, c0:c0 + GM_GW] = (u * (mixed + bias)).astype(BF16)


def _gmlp_in(x, sc, sh, w, seg_len):
    return pl.pallas_call(
        _gmlp_kernel,
        grid=(N_TOK // TM,),
        in_specs=[_row_spec(D_MODEL), _seg_spec(seg_len), _seg_spec(seg_len),
                  _full_spec((D_MODEL, GM_FF)), _full_spec((1, GM_HALF)),
                  _full_spec((GM_GROUPS, CHUNK, CHUNK)), _full_spec((GM_GROUPS, CHUNK, 1))],
        out_specs=_row_spec(GM_HALF),
        out_shape=jax.ShapeDtypeStruct((N_TOK, GM_HALF), BF16),
        compiler_params=_params(("arbitrary",)),
        name="gmlp_in",
    )(x, sc, sh, w["w_in"], w["v_gain"], w["w_s"], w["b_s"])


SWA_SCALE = SWA_HEAD_DIM ** -0.5
SWA_QW = SWA_HEADS * SWA_HEAD_DIM
SWA_KW = SWA_KV_HEADS * 2 * LANES
SWA_VW = SWA_KV_HEADS * LANES
SWA_GROUP = SWA_KV_HEADS


def _swa_proj_kernel(rope, x_ref, sc_ref, sh_ref, w_ref, *rest):
    if rope:
        cos_ref, sin_ref, q_ref, kd_ref, vd_ref, kraw_ref, vraw_ref = rest
    else:
        q_ref, kd_ref, vd_ref, kraw_ref, vraw_ref = rest
    h = (x_ref[...] * (1.0 + sc_ref[0]) + sh_ref[0]).astype(BF16)
    z = _dot(h, w_ref[...])
    lane = lax.broadcasted_iota(I32, (x_ref.shape[0], LANES), 1)
    low = lane < SWA_HEAD_DIM
    for m in range(SWA_KV_HEADS // 2):
        k0 = z[:, SWA_QW + (2 * m) * 2 * LANES:SWA_QW + (2 * m) * 2 * LANES + LANES]
        k1 = z[:, SWA_QW + (2 * m + 1) * 2 * LANES + LANES:SWA_QW + (2 * m + 2) * 2 * LANES]
        kraw_ref[:, m * LANES:(m + 1) * LANES] = jnp.where(low, k0, k1)
        v0 = z[:, SWA_QW + SWA_KW + (2 * m) * LANES:SWA_QW + SWA_KW + (2 * m + 1) * LANES]
        v1 = z[:, SWA_QW + SWA_KW + (2 * m + 1) * LANES:SWA_QW + SWA_KW + (2 * m + 2) * LANES]
        vraw_ref[:, m * LANES:(m + 1) * LANES] = jnp.where(low, v0, v1)
    vd_ref[...] = z[:, SWA_QW + SWA_KW:].astype(BF16)
    if rope:
        cos, sin = cos_ref[...], sin_ref[...]
        for j in range(SWA_QW // LANES):
            q_ref[:, j * LANES:(j + 1) * LANES] = _rope128(z[:, j * LANES:(j + 1) * LANES], cos, sin).astype(BF16)
        for j in range(SWA_KW // LANES):
            a = SWA_QW + j * LANES
            kd_ref[:, j * LANES:(j + 1) * LANES] = _rope128(z[:, a:a + LANES], cos, sin).astype(BF16)
    else:
        q_ref[...] = z[:, :SWA_QW].astype(BF16)
        kd_ref[...] = z[:, SWA_QW:SWA_QW + SWA_KW].astype(BF16)


def _swa_proj(x, sc, sh, w_ext, seg_len, rope_tabs):
    rope = rope_tabs is not None
    seg, row = _seg_spec(seg_len, TM_PROJ), functools.partial(_row_spec, tm=TM_PROJ)
    in_specs = [row(D_MODEL), seg, seg, _full_spec(w_ext.shape)]
    args = [x, sc, sh, w_ext]
    if rope:
        nblk = DEC_SEQ // TM_PROJ
        tab_spec = pl.BlockSpec((TM_PROJ, LANES), lambda r: (r % nblk, 0))
        in_specs += [tab_spec, tab_spec]
        args += list(rope_tabs)
    kvw = SWA_KV_HEADS * SWA_HEAD_DIM
    return pl.pallas_call(
        functools.partial(_swa_proj_kernel, rope),
        grid=(N_TOK // TM_PROJ,),
        in_specs=in_specs,
        out_specs=[row(SWA_QW), row(SWA_KW), row(SWA_VW), row(kvw), row(kvw)],
        out_shape=[jax.ShapeDtypeStruct((N_TOK, SWA_QW), BF16), jax.ShapeDtypeStruct((N_TOK, SWA_KW), BF16),
                   jax.ShapeDtypeStruct((N_TOK, SWA_VW), BF16), jax.ShapeDtypeStruct((N_TOK, kvw), F32),
                   jax.ShapeDtypeStruct((N_TOK, kvw), F32)],
        compiler_params=_params(("arbitrary",)),
        name="swa_proj",
    )(*args)


def _swa_heads(q_ref, sink_ref, o_ref, score_fn, value_fn):
    rows = q_ref.shape[0]
    lane = lax.broadcasted_iota(I32, (rows, LANES), 1)
    low = lane < SWA_HEAD_DIM
    for hk0 in range(0, SWA_KV_HEADS, SWA_GROUP):
        stacked, sinks = None, []
        for hk in range(hk0, hk0 + SWA_GROUP):
            c0 = hk * 2 * LANES
            q2 = jnp.concatenate([q_ref[:, c0:c0 + LANES], q_ref[:, c0 + LANES:c0 + 2 * LANES]], axis=0)
            for par in range(2):
                for pr in range(2):
                    hd = hk * 4 + pr * 2 + par
                    sinks.append(jnp.broadcast_to(sink_ref[:, hd:hd + 1], (rows, 1)))
                blocks = score_fn(hk, par, q2)
                if stacked is None:
                    stacked = [[] for _ in blocks]
                for kind, s in zip(stacked, blocks):
                    kind.append(s)
        scores = [jnp.concatenate(kind, axis=0) for kind in stacked]
        sk = jnp.concatenate(sinks, axis=0)
        m = sk
        for s in scores:
            m = jnp.maximum(m, jnp.max(s, axis=-1, keepdims=True))
        probs = [jnp.exp(s - m) for s in scores]
        l = jnp.exp(sk - m)
        for p in probs:
            l = l + jnp.sum(p, axis=-1, keepdims=True)
        inv = 1.0 / l
        n = 0
        for hk in range(hk0, hk0 + SWA_GROUP):
            res = []
            for par in range(2):
                blk = slice(n * 2 * rows, (n + 1) * 2 * rows)
                res.append(value_fn(hk, [p[blk] for p in probs]) * inv[blk])
                n += 1
            for pr in range(2):
                c0 = hk * 2 * LANES + pr * LANES
                part = slice(pr * rows, (pr + 1) * rows)
                o_ref[:, c0:c0 + LANES] = jnp.where(low, res[0][part], res[1][part]).astype(BF16)


def _swa_attn_ctx_kernel(q_ref, kd_ref, vd_ref, sink_ref, o_ref):
    def score_fn(hk, par, qp):
        a = hk * 2 * LANES + par * LANES
        return [_dot_nt(qp, kd_ref[:, a:a + LANES]) * SWA_SCALE]

    def value_fn(hk, probs):
        return _dot(probs[0].astype(BF16), vd_ref[:, hk * LANES:(hk + 1) * LANES])

    _swa_heads(q_ref, sink_ref, o_ref, score_fn, value_fn)


def _swa_attn_ctx(q, kd, vd, sink):
    blk = lambda w: pl.BlockSpec((CTX_SEQ, w), lambda b: (b, 0))
    return pl.pallas_call(
        _swa_attn_ctx_kernel,
        grid=(CTX_BATCH,),
        in_specs=[blk(SWA_QW), blk(SWA_KW), blk(SWA_VW), pl.BlockSpec((1, SWA_HEADS), lambda b: (0, 0))],
        out_specs=blk(SWA_QW),
        out_shape=jax.ShapeDtypeStruct((N_TOK, SWA_QW), BF16),
        compiler_params=_params(("arbitrary",)),
        name="swa_attn_ctx",
    )(q, kd, vd, sink)


SWA_WIN = 3 * BAND


def _swa_attn_lat_kernel(q_ref, kdl_ref, vdl_ref, kdc_ref, vdc_ref, sink_ref, o_ref):
    nb = pl.program_id(1)
    start = pl.multiple_of(jnp.clip((nb - 1) * BAND, 0, DEC_SEQ - SWA_WIN), BAND)
    qpos = nb * BAND + (lax.broadcasted_iota(I32, (2 * BAND, SWA_WIN), 0) & (BAND - 1))
    kpos = start + lax.broadcasted_iota(I32, (2 * BAND, SWA_WIN), 1)
    valid = jnp.abs(qpos - kpos) <= WINDOW

    def score_fn(hk, par, qp):
        a = hk * 2 * LANES + par * LANES
        kl = kdl_ref[pl.ds(start, SWA_WIN), a:a + LANES]
        sl = jnp.where(valid, _dot_nt(qp, kl) * SWA_SCALE, NEG_INF)
        sc = _dot_nt(qp, kdc_ref[:, a:a + LANES]) * SWA_SCALE
        return [sl, sc]

    def value_fn(hk, probs):
        vl = vdl_ref[pl.ds(start, SWA_WIN), hk * LANES:(hk + 1) * LANES]
        return _dot(probs[0].astype(BF16), vl) + _dot(probs[1].astype(BF16), vdc_ref[:, hk * LANES:(hk + 1) * LANES])

    _swa_heads(q_ref, sink_ref, o_ref, score_fn, value_fn)


def _swa_attn_lat(q, kd, vd, kd_ctx, vd_ctx, sink):
    nbands = DEC_SEQ // BAND
    return pl.pallas_call(
        _swa_attn_lat_kernel,
        grid=(DEC_BATCH, nbands),
        in_specs=[pl.BlockSpec((BAND, SWA_QW), lambda b, i: (b * nbands + i, 0)),
                  pl.BlockSpec((DEC_SEQ, SWA_KW), lambda b, i: (b, 0)),
                  pl.BlockSpec((DEC_SEQ, SWA_VW), lambda b, i: (b, 0)),
                  pl.BlockSpec((PAST_LEN, SWA_KW), lambda b, i: (b, 0)),
                  pl.BlockSpec((PAST_LEN, SWA_VW), lambda b, i: (b, 0)),
                  pl.BlockSpec((1, SWA_HEADS), lambda b, i: (0, 0))],
        out_specs=pl.BlockSpec((BAND, SWA_QW), lambda b, i: (b * nbands + i, 0)),
        out_shape=jax.ShapeDtypeStruct((N_TOK, SWA_QW), BF16),
        compiler_params=_params(("arbitrary", "arbitrary")),
        name="swa_attn_lat",
    )(q, kd, vd, kd_ctx, vd_ctx, sink)


def _post_kernel(a_ref, w_ref, x_ref, g1_ref, sc2_ref, sh2_ref, lng_ref, lnb_ref, rt_ref, x1_ref, h2_ref, lg_ref):
    y = _dot(a_ref[...], w_ref[...])
    x1 = _layer_norm(ALPHA * x_ref[...] + g1_ref[0] * y, lng_ref[...], lnb_ref[...])
    x1_ref[...] = x1
    h2 = x1 * (1.0 + sc2_ref[0]) + sh2_ref[0]
    for j in range(ROW_TILES):
        h2_ref[pl.ds(j, TM, stride=ROW_TILES), :] = h2[:, j * LANES:(j + 1) * LANES]
    rt = rt_ref[...]
    h_hi = h2.astype(BF16)
    h_lo = (h2 - h_hi.astype(F32)).astype(BF16)
    r_hi = rt.astype(BF16)
    r_lo = (rt - r_hi.astype(F32)).astype(BF16)
    both = _dot(h_hi, jnp.concatenate([r_hi, r_lo], axis=1))
    lg_ref[...] = (both[:, :LANES] + both[:, LANES:]) + _dot(h_lo, r_hi)


def _post(a, w_out, x, g1, sc2, sh2, lng, lnb, router_pad, seg_len):
    k = a.shape[1]
    return pl.pallas_call(
        _post_kernel,
        grid=(N_TOK // TM,),
        in_specs=[_row_spec(k), _full_spec((k, D_MODEL)), _row_spec(D_MODEL),
                  _seg_spec(seg_len), _seg_spec(seg_len), _seg_spec(seg_len),
                  _full_spec((1, D_MODEL)), _full_spec((1, D_MODEL)), _full_spec((D_MODEL, LANES))],
        out_specs=[_row_spec(D_MODEL), pl.BlockSpec((TM * ROW_TILES, LANES), lambda r: (r, 0)), _row_spec(LANES)],
        out_shape=[jax.ShapeDtypeStruct((N_TOK, D_MODEL), F32), jax.ShapeDtypeStruct((N_TOK * ROW_TILES, LANES), F32),
                   jax.ShapeDtypeStruct((N_TOK, LANES), F32)],
        compiler_params=_params(("arbitrary",)),
        name="mixer_out",
    )(a, w_out, x, g1, sc2, sh2, lng, lnb, router_pad)


CUM_BLK = 256
REFINE_STEPS = 16
F32_MIN_NORMAL = 1.1754943508222875e-38


def _excl_cumsum_lanes(a, upper):
    outs = []
    carry = jnp.zeros((a.shape[0], 1), F32)
    for b in range(N_TOK // CUM_BLK):
        blk = a[:, b * CUM_BLK:(b + 1) * CUM_BLK]
        outs.append(_dot(blk.astype(BF16), upper) + carry)
        carry = carry + jnp.sum(blk, axis=1, keepdims=True)
    return jnp.concatenate(outs, axis=1)


def _route_kernel(lg_ref, r_ref, tok_ref, roff_ref, idx_ref, rank_scr, w_scr):
    e_rows = N_EXPERTS
    lt = lg_ref[...].T[:e_rows]
    ex = jnp.exp(lt - jnp.max(lt, axis=0, keepdims=True))
    aff = ex / jnp.sum(ex, axis=0, keepdims=True)
    def count_ge(thr):
        return jnp.sum(jnp.where(aff >= thr, 1.0, 0.0), axis=1, keepdims=True)

    cur = jnp.zeros((e_rows, 1), I32)
    for b in range(30, -1, -1):
        cand = cur | jnp.int32(1 << b)
        cur = jnp.where(count_ge(lax.bitcast_convert_type(cand, F32)) >= float(CAP), cand, cur)
    lo = lax.bitcast_convert_type(cur, F32)
    hi = jnp.maximum(lax.bitcast_convert_type(cur + 1, F32), F32_MIN_NORMAL)
    for _ in range(REFINE_STEPS):
        w = hi - lo
        t1, t2, t3 = lo + 0.25 * w, lo + 0.5 * w, lo + 0.75 * w
        ok1, ok2, ok3 = (count_ge(t) >= float(CAP) for t in (t1, t2, t3))
        lo, hi = (jnp.where(ok3, t3, jnp.where(ok2, t2, jnp.where(ok1, t1, lo))),
                  jnp.where(ok1, jnp.where(ok2, jnp.where(ok3, hi, t3), t2), t1))
    gt = aff >= hi
    eq = (aff >= lo) & (aff < hi)
    need = float(CAP) - jnp.sum(jnp.where(gt, 1.0, 0.0), axis=1, keepdims=True)

    ri = lax.broadcasted_iota(I32, (CUM_BLK, CUM_BLK), 0)
    ci = lax.broadcasted_iota(I32, (CUM_BLK, CUM_BLK), 1)
    upper = jnp.where(ri < ci, 1.0, 0.0).astype(BF16)

    tie_rank = _excl_cumsum_lanes(jnp.where(eq, 1.0, 0.0), upper)
    sel = gt | (eq & (tie_rank < need))
    msk = jnp.where(sel, 1.0, 0.0)

    rank = _excl_cumsum_lanes(msk, upper)
    rank_scr[...] = jnp.where(sel, rank, -1.0)

    inv = 1.0 / LANES
    slot_no = jnp.where(sel, rank, -float(LANES))
    slot_hi = jnp.floor(slot_no * inv)
    g0 = aff.astype(BF16).astype(F32)
    r1 = aff - g0
    g1 = r1.astype(BF16).astype(F32)
    g2 = (r1 - g1).astype(BF16).astype(F32)
    tt = lax.broadcasted_iota(I32, (1, N_TOK), 1).astype(F32)
    t_hi = jnp.floor(tt * inv)
    misc = jnp.concatenate([t_hi, tt - t_hi * LANES, jnp.zeros((6, N_TOK), F32)], axis=0)
    feat = jnp.concatenate([slot_hi, slot_no - slot_hi * LANES, g0, g1, g2, misc,
                            jnp.zeros((LANES - 5 * e_rows - 8, N_TOK), F32)], axis=0)
    table = feat.T
    tok_ref[...] = table
    w_scr[...] = table.astype(BF16)

    nblk = N_TOK // TOK_BLK
    bt = lax.broadcasted_iota(I32, (N_TOK, LANES), 0) // TOK_BLK
    bb = lax.broadcasted_iota(I32, (N_TOK, LANES), 1)
    before = jnp.where((bt < bb) & (bb <= nblk), 1.0, 0.0).astype(BF16)
    roff_ref[...] = _dot(msk.astype(BF16), before).astype(I32)

    sub = lax.broadcasted_iota(I32, (8, LANES), 0)
    ln8 = lax.broadcasted_iota(I32, (8, LANES), 1)
    rb_rows = 128

    def per_expert(e, carry):
        rk = rank_scr[pl.ds(e, 1), :]
        want = jnp.where(sub < 5, sub * e_rows + e, sub + (F_T_HI - 5))
        pick = jnp.where(ln8 == want, 1.0, 0.0).astype(BF16)
        for rb in range(CAP // rb_rows):
            slot = (rb * rb_rows + lax.broadcasted_iota(I32, (rb_rows, 1), 0)).astype(F32)
            onehot = jnp.where(rk == slot, 1.0, 0.0).astype(BF16)
            rows = _dot(onehot, w_scr[...])
            r_ref[pl.ds(e, 1), rb * rb_rows:(rb + 1) * rb_rows, :] = rows[None]
            got = _dot_nt(pick, rows.astype(BF16))
            idx_ref[pl.ds(e, 1), :, rb * rb_rows:(rb + 1) * rb_rows] = (got[5:6] * LANES + got[6:7]).astype(I32)[None]
        return carry

    lax.fori_loop(0, e_rows, per_expert, 0)


def _route(logits):
    return pl.pallas_call(
        _route_kernel,
        grid=(1,),
        in_specs=[pl.BlockSpec((N_TOK, LANES), lambda i: (0, 0))],
        out_specs=[pl.BlockSpec((N_EXPERTS, CAP, LANES), lambda i: (0, 0, 0)),
                   pl.BlockSpec((N_TOK, LANES), lambda i: (0, 0)),
                   pl.BlockSpec((N_EXPERTS, LANES), lambda i: (0, 0)),
                   pl.BlockSpec((N_EXPERTS, 1, CAP), lambda i: (0, 0, 0))],
        out_shape=[jax.ShapeDtypeStruct((N_EXPERTS, CAP, LANES), F32),
                   jax.ShapeDtypeStruct((N_TOK, LANES), F32),
                   jax.ShapeDtypeStruct((N_EXPERTS, LANES), I32),
                   jax.ShapeDtypeStruct((N_EXPERTS, 1, CAP), I32)],
        scratch_shapes=[pltpu.VMEM((N_EXPERTS, N_TOK), F32), pltpu.VMEM((N_TOK, LANES), BF16)],
        compiler_params=_params(("arbitrary",)),
        name="moe_route",
    )(logits)


ROWS_E = 2 * CAP
N_FT = EXPERT_FF // FF_TILE
GATHER_UNROLL = 8
PIECE = 256
N_PIECE = FF_TILE // PIECE


def _tile(row):
    return pl.ds(pl.multiple_of(row * ROW_TILES, ROW_TILES), ROW_TILES)


def _ffn_kernel(idx_s, hc_hbm, hl_hbm, rc_ref, rl_ref, wg_ref, wu_ref, wd_ref, y_ref, xe, xb, acc, gsem):
    e = pl.program_id(0)
    f = pl.program_id(1)

    def gather_rows(ee):
        base = ee * ROWS_E

        def body(r, carry):
            tc = idx_s[base + r]
            pltpu.make_async_copy(hc_hbm.at[_tile(tc)], xe.at[_tile(r)], gsem.at[0]).start(priority=0)
            tl = idx_s[base + CAP + r]
            pltpu.make_async_copy(hl_hbm.at[_tile(tl)], xe.at[_tile(CAP + r)], gsem.at[0]).start(priority=1)
            return carry

        lax.fori_loop(0, CAP, body, 0, unroll=GATHER_UNROLL)

    @pl.when(f == 0)
    def _():
        @pl.when(e == 0)
        def _():
            gather_rows(0)

        pltpu.make_async_copy(xe, xe, gsem.at[0]).wait()
        for j in range(ROW_TILES):
            xb[:, j * LANES:(j + 1) * LANES] = xe[pl.ds(j, ROWS_E, stride=ROW_TILES), :].astype(BF16)

        @pl.when(e + 1 < N_EXPERTS)
        def _():
            gather_rows(e + 1)

        acc[...] = jnp.zeros_like(acc)

    x = xb[...]
    for p in range(N_PIECE):
        cols = slice(p * PIECE, (p + 1) * PIECE)
        g = _dot(x, wg_ref[0, :, cols].astype(BF16))
        u = _dot(x, wu_ref[0, :, cols].astype(BF16))
        hid = (g * _sigmoid(g) * u).astype(BF16)
        acc[...] += _dot(hid, wd_ref[0, cols, :].astype(BF16))

    @pl.when(f == N_FT - 1)
    def _():
        lane = lax.broadcasted_iota(I32, (CAP, LANES), 1)
        mine = (lane == F_G0 + e) | (lane == F_G1 + e) | (lane == F_G2 + e)
        gate = jnp.concatenate(
            [jnp.sum(jnp.where(mine, rc_ref[0], 0.0), axis=1, keepdims=True),
             jnp.sum(jnp.where(mine, rl_ref[0], 0.0), axis=1, keepdims=True)], axis=0)
        y_ref[0] = (acc[...] * gate).astype(BF16)


def _moe_ffn(layer, idx_all, h_ctx, h_lat, r_ctx, r_lat, w_gate, w_up, w_down):
    any_spec = pl.BlockSpec(memory_space=pl.ANY)
    e0 = layer * N_EXPERTS
    grid_spec = pltpu.PrefetchScalarGridSpec(
        num_scalar_prefetch=1,
        grid=(N_EXPERTS, N_FT),
        in_specs=[any_spec, any_spec,
                  pl.BlockSpec((1, CAP, LANES), lambda e, f, *_: (e, 0, 0)),
                  pl.BlockSpec((1, CAP, LANES), lambda e, f, *_: (e, 0, 0)),
                  pl.BlockSpec((1, D_MODEL, FF_TILE), lambda e, f, *_: (e0 + e, 0, f)),
                  pl.BlockSpec((1, D_MODEL, FF_TILE), lambda e, f, *_: (e0 + e, 0, f)),
                  pl.BlockSpec((1, FF_TILE, D_MODEL), lambda e, f, *_: (e0 + e, f, 0))],
        out_specs=pl.BlockSpec((1, ROWS_E, D_MODEL), lambda e, f, *_: (e, 0, 0)),
        scratch_shapes=[pltpu.VMEM((ROWS_E * ROW_TILES, LANES), F32), pltpu.VMEM((ROWS_E, D_MODEL), BF16),
                        pltpu.VMEM((ROWS_E, D_MODEL), F32), pltpu.SemaphoreType.DMA((1,))],
    )
    return pl.pallas_call(
        _ffn_kernel,
        grid_spec=grid_spec,
        out_shape=jax.ShapeDtypeStruct((N_EXPERTS, ROWS_E, D_MODEL), BF16),
        compiler_params=_params(("arbitrary", "arbitrary")),
        name="moe_ffn",
    )(idx_all, h_ctx, h_lat, r_ctx, r_lat, w_gate, w_up, w_down)


WIN_E = 64
EXPERTS_PER_DOT = 4
N_TBLK = N_TOK // TOK_BLK
ROFF_STRIDE = LANES


def _combine_kernel(group, roff_s, y_hbm, tok_ref, x_ref, g2_ref, lng_ref, lnb_ref, o_ref, ywin, yext, acc, wsem, esem):
    b = pl.program_id(0)

    def win_start(e, bb):
        first = roff_s[e * ROFF_STRIDE + bb]
        return jnp.minimum((first // BF16_ROWS) * BF16_ROWS, CAP - WIN_E)

    def rows_at(start):
        return pl.ds(pl.multiple_of(group * CAP + start, BF16_ROWS), WIN_E)

    def start_windows(bb, slot):
        for e in range(N_EXPERTS):
            pltpu.make_async_copy(y_hbm.at[e, rows_at(win_start(e, bb))], ywin.at[slot, e], wsem.at[slot]).start()

    @pl.when(b == 0)
    def _():
        start_windows(0, 0)

    @pl.when(b + 1 < N_TBLK)
    def _():
        start_windows(b + 1, (b + 1) % 2)

    slot = b % 2
    tok = tok_ref[...]
    lane = lax.broadcasted_iota(I32, (TOK_BLK, WIN_E), 1)
    owner = [tok[:, F_SLOT_HI + e:F_SLOT_HI + e + 1] * LANES + tok[:, F_SLOT_LO + e:F_SLOT_LO + e + 1]
             for e in range(N_EXPERTS)]
    starts = [win_start(e, b) for e in range(N_EXPERTS)]

    pltpu.make_async_copy(ywin.at[slot], ywin.at[slot], wsem.at[slot]).wait()
    wide = lax.broadcasted_iota(I32, (TOK_BLK, EXPERTS_PER_DOT * WIN_E), 1)
    total = None
    for e0 in range(0, N_EXPERTS, EXPERTS_PER_DOT):
        own, slots = owner[e0], (starts[e0] + wide).astype(F32)
        for k in range(1, EXPERTS_PER_DOT):
            later = wide >= k * WIN_E
            own = jnp.where(later, owner[e0 + k], own)
            slots = jnp.where(later, (starts[e0 + k] - k * WIN_E + wide).astype(F32), slots)
        seg = jnp.where(own == slots, 1.0, 0.0).astype(BF16)
        rows = ywin[slot, e0:e0 + EXPERTS_PER_DOT].reshape(EXPERTS_PER_DOT * WIN_E, D_MODEL)
        part = _dot(seg, rows)
        total = part if total is None else total + part
    acc[...] = total

    for e in range(N_EXPERTS):
        past = jnp.maximum(roff_s[e * ROFF_STRIDE + b + 1] - (starts[e] + WIN_E), 0)

        def over(k, carry, e=e):
            nominal = starts[e] + (k + 1) * WIN_E
            first = jnp.minimum(nominal, CAP - WIN_E)
            cp = pltpu.make_async_copy(y_hbm.at[e, rows_at(first)], yext, esem.at[0])
            cp.start()
            cp.wait()
            slots = first + lane
            seg = jnp.where((owner[e] == slots.astype(F32)) & (slots >= nominal), 1.0, 0.0).astype(BF16)
            acc[...] += _dot(seg, yext[...])
            return carry

        lax.fori_loop(0, (past + WIN_E - 1) // WIN_E, over, 0)

    o_ref[...] = _layer_norm(ALPHA * x_ref[...] + g2_ref[0] * acc[...], lng_ref[...], lnb_ref[...])


def _combine(group, roff, y, tok, x1, g2, lng, lnb, seg_len):
    grid_spec = pltpu.PrefetchScalarGridSpec(
        num_scalar_prefetch=1,
        grid=(N_TBLK,),
        in_specs=[pl.BlockSpec(memory_space=pl.ANY),
                  pl.BlockSpec((TOK_BLK, LANES), lambda b, *_: (b, 0)),
                  pl.BlockSpec((TOK_BLK, D_MODEL), lambda b, *_: (b, 0)),
                  pl.BlockSpec((1, 1, D_MODEL), lambda b, *_: (b * TOK_BLK // seg_len, 0, 0)),
                  pl.BlockSpec((1, D_MODEL), lambda b, *_: (0, 0)),
                  pl.BlockSpec((1, D_MODEL), lambda b, *_: (0, 0))],
        out_specs=pl.BlockSpec((TOK_BLK, D_MODEL), lambda b, *_: (b, 0)),
        scratch_shapes=[pltpu.VMEM((2, N_EXPERTS, WIN_E, D_MODEL), BF16), pltpu.VMEM((WIN_E, D_MODEL), BF16),
                        pltpu.VMEM((TOK_BLK, D_MODEL), F32),
                        pltpu.SemaphoreType.DMA((2,)), pltpu.SemaphoreType.DMA((1,))],
    )
    return pl.pallas_call(
        functools.partial(_combine_kernel, group),
        grid_spec=grid_spec,
        out_shape=jax.ShapeDtypeStruct((N_TOK, D_MODEL), F32),
        compiler_params=_params(("arbitrary",)),
        name="moe_combine",
    )(roff, y, tok, x1, g2, lng, lnb)


def _rope_tables():
    t = jnp.arange(DEC_SEQ)
    n_freq = MLA_ROPE // 4
    inv_freq = ROPE_BASE ** (-jnp.arange(n_freq, dtype=F32) / n_freq)
    ang_r = (t // GRID_W).astype(F32)[:, None] * inv_freq[None, :]
    ang_c = (t % GRID_W).astype(F32)[:, None] * inv_freq[None, :]
    cos64 = jnp.concatenate([jnp.cos(ang_r)] * 2 + [jnp.cos(ang_c)] * 2, axis=1)
    sin64 = jnp.concatenate([jnp.sin(ang_r)] * 2 + [jnp.sin(ang_c)] * 2, axis=1)
    one, zero = jnp.ones_like(cos64), jnp.zeros_like(sin64)
    mla = (jnp.concatenate([cos64, one], axis=1), jnp.concatenate([sin64, zero], axis=1))
    swa = (jnp.concatenate([cos64, cos64], axis=1), jnp.concatenate([sin64, sin64], axis=1))
    return mla, swa


def _swa_key_layout(k):
    z = jnp.zeros_like(k)
    return jnp.stack([k, z, z, k], axis=2).reshape(k.shape[0], SWA_KW)


def _swa_value_layout(v):
    return jnp.stack([v, v], axis=2).reshape(v.shape[0], SWA_VW)


def kernel(x_prompt, x_sample, cache_mla_ckv, cache_mla_kpe, cache_swa_k, cache_swa_v, c, c_ctx, mod_w, mod_b, ln_gain, ln_bias, mla_w_in, mla_q_gain, mla_kv_gain, mla_w_q_up, mla_w_kv_up, mla_w_out, gm_w_in, gm_v_gain, gm_w_s, gm_b_s, gm_w_out, swa_w_qkv, swa_sink, swa_w_out, moe_router, moe_w_gate, moe_w_up, moe_w_down):
    d = D_MODEL
    xs = [x_prompt.reshape(N_TOK, d), x_sample.reshape(N_TOK, d)]
    seg_lens = [N_TOK, DEC_SEQ]
    cond8 = jnp.concatenate([c_ctx[None, :], c, jnp.zeros((3, d), F32)], axis=0)
    mod = _modulation_all(cond8, mod_w, mod_b)
    rope_mla, rope_swa = _rope_tables()
    w_gate_all = moe_w_gate.reshape(DEPTH * N_EXPERTS, d, EXPERT_FF)
    w_up_all = moe_w_up.reshape(DEPTH * N_EXPERTS, d, EXPERT_FF)
    w_down_all = moe_w_down.reshape(DEPTH * N_EXPERTS, EXPERT_FF, d)

    def mod_rows(i, q):
        m = mod[i, :, q * d:(q + 1) * d]
        return [m[0:1].reshape(1, 1, d), m[1:1 + DEC_BATCH].reshape(DEC_BATCH, 1, d)]

    ckv_out, kpe_out, k_out, v_out = [], [], [], []
    for i in range(DEPTH):
        kind, j = i % 3, i // 3
        sh1, sc1, g1, sh2, sc2, g2 = (mod_rows(i, q) for q in range(6))
        router_pad = jnp.pad(moe_router[i], ((0, 0), (0, LANES - N_EXPERTS)))
        lng = ln_gain[i].reshape(2, 1, d)
        lnb = ln_bias[i].reshape(2, 1, d)

        mixed = []
        if kind == 0:
            w = {
                "w_in": jnp.pad(mla_w_in[j], ((0, 0), (0, MLA_ROPE))).astype(BF16),
                "q_gain": mla_q_gain[j][None, :], "kv_gain": mla_kv_gain[j][None, :],
                "w_q": jnp.pad(mla_w_q_up[j], ((0, 0), (0, 0), (0, MLA_QK_PAD - MLA_NOPE - MLA_ROPE)))
                .reshape(MLA_Q_LORA, MLA_HEADS * MLA_QK_PAD).astype(BF16),
                "w_kv": mla_w_kv_up[j].reshape(MLA_KV_LORA, MLA_HEADS * (MLA_NOPE + MLA_V)).astype(BF16),
            }
            w_out = mla_w_out[j].astype(BF16)
            q, kv, kpe, ckv, kraw = _mla_proj(xs[0], sc1[0], sh1[0], w, seg_lens[0], None)
            ckv_out.append(ckv.reshape(CTX_BATCH, CTX_SEQ, MLA_KV_LORA))
            kpe_out.append(kraw.reshape(CTX_BATCH, CTX_SEQ, MLA_ROPE))
            mixed.append(_mla_attn_ctx(q, kv, kpe))
            q, kv, kpe, _, _ = _mla_proj(xs[1], sc1[1], sh1[1], w, seg_lens[1], rope_mla)
            kv_ctx = _matmul_bf16(cache_mla_ckv[:, j].reshape(DEC_BATCH * PAST_LEN, MLA_KV_LORA).astype(BF16), w["w_kv"])
            kpe_ctx = jnp.pad(cache_mla_kpe[:, j].reshape(DEC_BATCH * PAST_LEN, MLA_ROPE),
                              ((0, 0), (0, LANES - MLA_ROPE))).astype(BF16)
            mixed.append(_mla_attn_lat(q, kv, kpe, kv_ctx, kpe_ctx))
        elif kind == 1:
            w = {"w_in": gm_w_in[j].astype(BF16), "v_gain": gm_v_gain[j][None, :],
                 "w_s": gm_w_s[j].astype(BF16), "b_s": gm_b_s[j][:, :, None]}
            w_out = gm_w_out[j].astype(BF16)
            for g in range(2):
                mixed.append(_gmlp_in(xs[g], sc1[g], sh1[g], w, seg_lens[g]))
        else:
            wq = swa_w_qkv[j]
            nq, nkv = SWA_HEADS * SWA_HEAD_DIM, SWA_KV_HEADS * SWA_HEAD_DIM
            w_ext = jnp.concatenate(
                [wq[:, :nq],
                 _swa_key_layout(wq[:, nq:nq + nkv].reshape(d, SWA_KV_HEADS, SWA_HEAD_DIM)),
                 _swa_value_layout(wq[:, nq + nkv:].reshape(d, SWA_KV_HEADS, SWA_HEAD_DIM))], axis=1).astype(BF16)
            w_out = swa_w_out[j].astype(BF16)
            sink = swa_sink[j][None, :]
            q, kd, vd, kraw, vraw = _swa_proj(xs[0], sc1[0], sh1[0], w_ext, seg_lens[0], None)
            k_out.append(kraw.reshape(CTX_BATCH, CTX_SEQ, SWA_KV_HEADS, SWA_HEAD_DIM))
            v_out.append(vraw.reshape(CTX_BATCH, CTX_SEQ, SWA_KV_HEADS, SWA_HEAD_DIM))
            mixed.append(_swa_attn_ctx(q, kd, vd, sink))
            q, kd, vd, _, _ = _swa_proj(xs[1], sc1[1], sh1[1], w_ext, seg_lens[1], rope_swa)
            rows = DEC_BATCH * PAST_LEN
            kd_ctx = _swa_key_layout(cache_swa_k[:, j].reshape(rows, SWA_KV_HEADS, SWA_HEAD_DIM)).astype(BF16)
            vd_ctx = _swa_value_layout(cache_swa_v[:, j].reshape(rows, SWA_KV_HEADS, SWA_HEAD_DIM)).astype(BF16)
            mixed.append(_swa_attn_lat(q, kd, vd, kd_ctx, vd_ctx, sink))

        x1, h2, routed = [], [], []
        for g in range(2):
            x1_g, h2_g, lg_g = _post(mixed[g], w_out, xs[g], g1[g], sc2[g], sh2[g], lng[0], lnb[0], router_pad, seg_lens[g])
            x1.append(x1_g)
            h2.append(h2_g)
            routed.append(_route(lg_g))
        idx_all = jnp.concatenate([routed[0][3], routed[1][3]], axis=2).reshape(-1)
        y = _moe_ffn(i, idx_all, h2[0], h2[1], routed[0][0], routed[1][0], w_gate_all, w_up_all, w_down_all)
        for g in range(2):
            roff = routed[g][2].reshape(-1)
            xs[g] = _combine(g, roff, y, routed[g][1], x1[g], g2[g], lng[1], lnb[1], seg_lens[g])

    return (xs[0].reshape(CTX_BATCH, CTX_SEQ, d), xs[1].reshape(DEC_BATCH, DEC_SEQ, d),
            jnp.stack(ckv_out, axis=1), jnp.stack(kpe_out, axis=1),
            jnp.stack(k_out, axis=1), jnp.stack(v_out, axis=1))
```

```python
import functools

import jax
import jax.numpy as jnp
from jax import lax
from jax.experimental import pallas as pl
from jax.experimental.pallas import tpu as pltpu

D_MODEL = 1024
N_TOK = 4096
DEPTH = 4
DEC_BATCH = 4
DEC_SEQ = 1024
PAST_LEN = 256
CTX_BATCH = 16
CTX_SEQ = 256
GRID_W = 64
ALPHA = (2.0 * DEPTH) ** 0.25
MLA_HEADS = 8
MLA_Q_LORA = 384
MLA_KV_LORA = 256
MLA_NOPE = 128
MLA_ROPE = 64
MLA_V = 128
MLA_QK_PAD = 256
GM_FF = 6 * D_MODEL
GM_HALF = GM_FF // 2
GM_GROUPS = 8
GM_GW = GM_HALF // GM_GROUPS
CHUNK = 128
SWA_HEADS = 16
SWA_KV_HEADS = 4
SWA_HEAD_DIM = 64
WINDOW = 128
BAND = 128
N_EXPERTS = 16
EXPERT_FF = 2 * D_MODEL
CAP = 2 * N_TOK // N_EXPERTS
ROPE_BASE = 10000.0
NEG_INF = -1e30
EPS = 1e-6
F32 = jnp.float32
BF16 = jnp.bfloat16
I32 = jnp.int32
U32 = jnp.uint32
HIGHEST = lax.Precision.HIGHEST

LANES = 128
SUBLANES = 8
BF16_ROWS = 16
ROW_TILES = D_MODEL // LANES
VMEM_LIMIT = 56 * 1024 * 1024
TM = 256
TQ_LAT = 512
TM_PROJ = 512
FF_TILE = 1024
TOK_BLK = 256
MOD_TN = 1536

F_SLOT_HI, F_SLOT_LO, F_G0, F_G1, F_G2 = 0, 16, 32, 48, 64
F_T_HI, F_T_LO = 80, 81


def _params(sem, vmem=VMEM_LIMIT):
    return pltpu.CompilerParams(dimension_semantics=sem, vmem_limit_bytes=vmem)


def _dot(a, b):
    return jnp.dot(a, b, preferred_element_type=F32)


def _dot_nt(a, b):
    return lax.dot_general(a, b, (((1,), (1,)), ((), ())), preferred_element_type=F32)


def _sigmoid(x):
    return 1.0 / (1.0 + jnp.exp(-x))


def _gelu_tanh(x):
    return 0.5 * x * (1.0 + jnp.tanh(0.7978845608028654 * (x + 0.044715 * (x * x * x))))


def _layer_norm(t, gain, bias):
    mu = jnp.mean(t, axis=-1, keepdims=True)
    d = t - mu
    var = jnp.mean(d * d, axis=-1, keepdims=True)
    return d * lax.rsqrt(var + EPS) * gain + bias


def _rms_norm(t, gain):
    return t * lax.rsqrt(jnp.mean(t * t, axis=-1, keepdims=True) + EPS) * gain


def _rope128(x, cos, sin):
    lane = lax.broadcasted_iota(I32, x.shape, 1)
    first = (lane % 32) < 16
    swapped = jnp.where(first, -pltpu.roll(x, 112, 1), pltpu.roll(x, 16, 1))
    return x * cos + swapped * sin


def _mod_kernel(cond_ref, w_ref, b_ref, o_ref):
    c = cond_ref[...]
    s = c * _sigmoid(c)
    o_ref[0] = jnp.dot(s, w_ref[0], precision=HIGHEST, preferred_element_type=F32) + b_ref[0]


def _modulation_all(cond8, mod_w, mod_b):
    return pl.pallas_call(
        _mod_kernel,
        grid=(DEPTH, 6 * D_MODEL // MOD_TN),
        in_specs=[pl.BlockSpec((8, D_MODEL), lambda i, n: (0, 0)),
                  pl.BlockSpec((1, D_MODEL, MOD_TN), lambda i, n: (i, 0, n)),
                  pl.BlockSpec((1, 1, MOD_TN), lambda i, n: (i, 0, n))],
        out_specs=pl.BlockSpec((1, 8, MOD_TN), lambda i, n: (i, 0, n)),
        out_shape=jax.ShapeDtypeStruct((DEPTH, 8, 6 * D_MODEL), F32),
        compiler_params=_params(("arbitrary", "arbitrary")),
        name="modulation",
    )(cond8, mod_w, mod_b.reshape(DEPTH, 1, 6 * D_MODEL))


def _seg_spec(seg_len, tm=TM):
    return pl.BlockSpec((1, 1, D_MODEL), lambda r: (r * tm // seg_len, 0, 0))


def _row_spec(width, tm=TM):
    return pl.BlockSpec((tm, width), lambda r: (r, 0))


def _full_spec(shape):
    nd = len(shape)
    return pl.BlockSpec(shape, lambda r: (0,) * nd)


def _mla_proj_kernel(rope, x_ref, sc_ref, sh_ref, win_ref, qg_ref, kvg_ref, wq_ref, wkv_ref, *rest):
    if rope:
        cos_ref, sin_ref, q_ref, kv_ref, kpe_ref, ckv_ref, kraw_ref = rest
    else:
        q_ref, kv_ref, kpe_ref, ckv_ref, kraw_ref = rest
    h = (x_ref[...] * (1.0 + sc_ref[0]) + sh_ref[0]).astype(BF16)
    z = _dot(h, win_ref[...])
    cq = _rms_norm(z[:, :MLA_Q_LORA], qg_ref[...])
    ckv = _rms_norm(z[:, MLA_Q_LORA:MLA_Q_LORA + MLA_KV_LORA], kvg_ref[...])
    kp = z[:, MLA_Q_LORA + MLA_KV_LORA:]
    ckv_ref[...] = ckv
    kraw_ref[...] = kp[:, :MLA_ROPE]
    q = _dot(cq.astype(BF16), wq_ref[...])
    kv_ref[...] = _dot(ckv.astype(BF16), wkv_ref[...]).astype(BF16)
    if rope:
        cos, sin = cos_ref[...], sin_ref[...]
        for hd in range(MLA_HEADS):
            a = hd * MLA_QK_PAD
            q_ref[:, a:a + LANES] = q[:, a:a + LANES].astype(BF16)
            q_ref[:, a + LANES:a + 2 * LANES] = _rope128(q[:, a + LANES:a + 2 * LANES], cos, sin).astype(BF16)
        kp = _rope128(kp, cos, sin)
    else:
        q_ref[...] = q.astype(BF16)
    kpe_ref[...] = kp.astype(BF16)


def _mla_proj(x, sc, sh, w, seg_len, rope_tabs):
    rope = rope_tabs is not None
    seg, row = _seg_spec(seg_len, TM_PROJ), functools.partial(_row_spec, tm=TM_PROJ)
    in_specs = [row(D_MODEL), seg, seg,
                _full_spec(w["w_in"].shape), _full_spec((1, MLA_Q_LORA)), _full_spec((1, MLA_KV_LORA)),
                _full_spec(w["w_q"].shape), _full_spec(w["w_kv"].shape)]
    args = [x, sc, sh, w["w_in"], w["q_gain"], w["kv_gain"], w["w_q"], w["w_kv"]]
    if rope:
        nblk = DEC_SEQ // TM_PROJ
        tab_spec = pl.BlockSpec((TM_PROJ, LANES), lambda r: (r % nblk, 0))
        in_specs += [tab_spec, tab_spec]
        args += list(rope_tabs)
    wq = MLA_HEADS * MLA_QK_PAD
    return pl.pallas_call(
        functools.partial(_mla_proj_kernel, rope),
        grid=(N_TOK // TM_PROJ,),
        in_specs=in_specs,
        out_specs=[row(wq), row(wq), row(LANES), row(MLA_KV_LORA), row(MLA_ROPE)],
        out_shape=[jax.ShapeDtypeStruct((N_TOK, wq), BF16), jax.ShapeDtypeStruct((N_TOK, wq), BF16),
                   jax.ShapeDtypeStruct((N_TOK, LANES), BF16), jax.ShapeDtypeStruct((N_TOK, MLA_KV_LORA), F32),
                   jax.ShapeDtypeStruct((N_TOK, MLA_ROPE), F32)],
        compiler_params=_params(("arbitrary",)),
        name="mla_proj",
    )(*args)


def _mm_kernel(a_ref, w_ref, o_ref):
    o_ref[...] = _dot(a_ref[...], w_ref[...]).astype(o_ref.dtype)


def _matmul_bf16(a, w):
    m, k = a.shape
    n = w.shape[1]
    return pl.pallas_call(
        _mm_kernel,
        grid=(m // TM,),
        in_specs=[pl.BlockSpec((TM, k), lambda r: (r, 0)), _full_spec(w.shape)],
        out_specs=pl.BlockSpec((TM, n), lambda r: (r, 0)),
        out_shape=jax.ShapeDtypeStruct((m, n), BF16),
        compiler_params=_params(("arbitrary",)),
        name="matmul",
    )(a, w)


MLA_SCALE = (MLA_NOPE + MLA_ROPE) ** -0.5


CTX_PER_STEP = 4


def _mla_attn_ctx_kernel(q_ref, kv_ref, kpe_ref, o_ref):
    for hd in range(MLA_HEADS):
        a = hd * MLA_QK_PAD
        for sb in range(CTX_PER_STEP):
            rows = slice(sb * CTX_SEQ, (sb + 1) * CTX_SEQ)
            qh = q_ref[rows, a:a + MLA_QK_PAD]
            kh = jnp.concatenate([kv_ref[rows, a:a + LANES], kpe_ref[rows, :]], axis=1)
            vh = kv_ref[rows, a + LANES:a + 2 * LANES]
            s = _dot_nt(qh, kh) * MLA_SCALE
            p = jnp.exp(s - jnp.max(s, axis=-1, keepdims=True))
            l = jnp.sum(p, axis=-1, keepdims=True)
            o_ref[rows, hd * MLA_V:(hd + 1) * MLA_V] = (_dot(p.astype(BF16), vh) / l).astype(BF16)


def _mla_attn_ctx(q, kv, kpe):
    wq = MLA_HEADS * MLA_QK_PAD
    blk = lambda w: pl.BlockSpec((CTX_PER_STEP * CTX_SEQ, w), lambda b: (b, 0))
    return pl.pallas_call(
        _mla_attn_ctx_kernel,
        grid=(CTX_BATCH // CTX_PER_STEP,),
        in_specs=[blk(wq), blk(wq), blk(LANES)],
        out_specs=blk(MLA_HEADS * MLA_V),
        out_shape=jax.ShapeDtypeStruct((N_TOK, MLA_HEADS * MLA_V), BF16),
        compiler_params=_params(("arbitrary",)),
        name="mla_attn_ctx",
    )(q, kv, kpe)


def _mla_attn_lat_kernel(q_ref, kvl_ref, kpel_ref, kvc_ref, kpec_ref, o_ref):
    kpel = kpel_ref[...]
    kpec = kpec_ref[...]
    for hd in range(MLA_HEADS):
        a = hd * MLA_QK_PAD
        qh = q_ref[:, a:a + MLA_QK_PAD]
        kc = jnp.concatenate([kvc_ref[:, a:a + LANES], kpec], axis=1)
        kl = jnp.concatenate([kvl_ref[:, a:a + LANES], kpel], axis=1)
        sc = _dot_nt(qh, kc) * MLA_SCALE
        sl = _dot_nt(qh, kl) * MLA_SCALE
        m = jnp.maximum(jnp.max(sc, axis=-1, keepdims=True), jnp.max(sl, axis=-1, keepdims=True))
        pc = jnp.exp(sc - m)
        plat = jnp.exp(sl - m)
        l = jnp.sum(pc, axis=-1, keepdims=True) + jnp.sum(plat, axis=-1, keepdims=True)
        o = _dot(pc.astype(BF16), kvc_ref[:, a + LANES:a + 2 * LANES]) \
            + _dot(plat.astype(BF16), kvl_ref[:, a + LANES:a + 2 * LANES])
        o_ref[:, hd * MLA_V:(hd + 1) * MLA_V] = (o / l).astype(BF16)


def _mla_attn_lat(q, kv, kpe, kv_ctx, kpe_ctx):
    wq = MLA_HEADS * MLA_QK_PAD
    nq = DEC_SEQ // TQ_LAT
    return pl.pallas_call(
        _mla_attn_lat_kernel,
        grid=(DEC_BATCH, nq),
        in_specs=[pl.BlockSpec((TQ_LAT, wq), lambda b, i: (b * nq + i, 0)),
                  pl.BlockSpec((DEC_SEQ, wq), lambda b, i: (b, 0)),
                  pl.BlockSpec((DEC_SEQ, LANES), lambda b, i: (b, 0)),
                  pl.BlockSpec((PAST_LEN, wq), lambda b, i: (b, 0)),
                  pl.BlockSpec((PAST_LEN, LANES), lambda b, i: (b, 0))],
        out_specs=pl.BlockSpec((TQ_LAT, MLA_HEADS * MLA_V), lambda b, i: (b * nq + i, 0)),
        out_shape=jax.ShapeDtypeStruct((N_TOK, MLA_HEADS * MLA_V), BF16),
        compiler_params=_params(("arbitrary", "arbitrary")),
        name="mla_attn_lat",
    )(q, kv, kpe, kv_ctx, kpe_ctx)


def _gmlp_kernel(x_ref, sc_ref, sh_ref, win_ref, vg_ref, ws_ref, bs_ref, p_ref):
    h = (x_ref[...] * (1.0 + sc_ref[0]) + sh_ref[0]).astype(BF16)
    zv = _gelu_tanh(_dot(h, win_ref[:, GM_HALF:]))
    mu = jnp.mean(zv, axis=-1, keepdims=True)
    d = zv - mu
    var = jnp.mean(d * d, axis=-1, keepdims=True)
    vn = (d * lax.rsqrt(var + EPS) * vg_ref[...]).astype(BF16)
    for g in range(GM_GROUPS):
        c0 = g * GM_GW
        if g % 2 == 0:
            u2 = _gelu_tanh(_dot(h, win_ref[:, c0:c0 + 2 * GM_GW]))
        u = u2[:, (g % 2) * GM_GW:(g % 2 + 1) * GM_GW]
        mixed = jnp.concatenate(
            [_dot(ws_ref[g], vn[k * CHUNK:(k + 1) * CHUNK, c0:c0 + GM_GW]) for k in range(TM // CHUNK)], axis=0)
        bias = jnp.concatenate([bs_ref[g]] * (TM // CHUNK), axis=0)
        p_ref[:, c0:c0 + GM_GW] = (u * (mixed + bias)).astype(BF16)


def _gmlp_in(x, sc, sh, w, seg_len):
    return pl.pallas_call(
        _gmlp_kernel,
        grid=(N_TOK // TM,),
        in_specs=[_row_spec(D_MODEL), _seg_spec(seg_len), _seg_spec(seg_len),
                  _full_spec((D_MODEL, GM_FF)), _full_spec((1, GM_HALF)),
                  _full_spec((GM_GROUPS, CHUNK, CHUNK)), _full_spec((GM_GROUPS, CHUNK, 1))],
        out_specs=_row_spec(GM_HALF),
        out_shape=jax.ShapeDtypeStruct((N_TOK, GM_HALF), BF16),
        compiler_params=_params(("arbitrary",)),
        name="gmlp_in",
    )(x, sc, sh, w["w_in"], w["v_gain"], w["w_s"], w["b_s"])


SWA_SCALE = SWA_HEAD_DIM ** -0.5
SWA_QW = SWA_HEADS * SWA_HEAD_DIM
SWA_KW = SWA_KV_HEADS * 2 * LANES
SWA_VW = SWA_KV_HEADS * LANES
SWA_GROUP = SWA_KV_HEADS


def _swa_proj_kernel(rope, x_ref, sc_ref, sh_ref, w_ref, *rest):
    if rope:
        cos_ref, sin_ref, q_ref, kd_ref, vd_ref, kraw_ref, vraw_ref = rest
    else:
        q_ref, kd_ref, vd_ref, kraw_ref, vraw_ref = rest
    h = (x_ref[...] * (1.0 + sc_ref[0]) + sh_ref[0]).astype(BF16)
    z = _dot(h, w_ref[...])
    lane = lax.broadcasted_iota(I32, (x_ref.shape[0], LANES), 1)
    low = lane < SWA_HEAD_DIM
    for m in range(SWA_KV_HEADS // 2):
        k0 = z[:, SWA_QW + (2 * m) * 2 * LANES:SWA_QW + (2 * m) * 2 * LANES + LANES]
        k1 = z[:, SWA_QW + (2 * m + 1) * 2 * LANES + LANES:SWA_QW + (2 * m + 2) * 2 * LANES]
        kraw_ref[:, m * LANES:(m + 1) * LANES] = jnp.where(low, k0, k1)
        v0 = z[:, SWA_QW + SWA_KW + (2 * m) * LANES:SWA_QW + SWA_KW + (2 * m + 1) * LANES]
        v1 = z[:, SWA_QW + SWA_KW + (2 * m + 1) * LANES:SWA_QW + SWA_KW + (2 * m + 2) * LANES]
        vraw_ref[:, m * LANES:(m + 1) * LANES] = jnp.where(low, v0, v1)
    vd_ref[...] = z[:, SWA_QW + SWA_KW:].astype(BF16)
    if rope:
        cos, sin = cos_ref[...], sin_ref[...]
        for j in range(SWA_QW // LANES):
            q_ref[:, j * LANES:(j + 1) * LANES] = _rope128(z[:, j * LANES:(j + 1) * LANES], cos, sin).astype(BF16)
        for j in range(SWA_KW // LANES):
            a = SWA_QW + j * LANES
            kd_ref[:, j * LANES:(j + 1) * LANES] = _rope128(z[:, a:a + LANES], cos, sin).astype(BF16)
    else:
        q_ref[...] = z[:, :SWA_QW].astype(BF16)
        kd_ref[...] = z[:, SWA_QW:SWA_QW + SWA_KW].astype(BF16)


def _swa_proj(x, sc, sh, w_ext, seg_len, rope_tabs):
    rope = rope_tabs is not None
    seg, row = _seg_spec(seg_len, TM_PROJ), functools.partial(_row_spec, tm=TM_PROJ)
    in_specs = [row(D_MODEL), seg, seg, _full_spec(w_ext.shape)]
    args = [x, sc, sh, w_ext]
    if rope:
        nblk = DEC_SEQ // TM_PROJ
        tab_spec = pl.BlockSpec((TM_PROJ, LANES), lambda r: (r % nblk, 0))
        in_specs += [tab_spec, tab_spec]
        args += list(rope_tabs)
    kvw = SWA_KV_HEADS * SWA_HEAD_DIM
    return pl.pallas_call(
        functools.partial(_swa_proj_kernel, rope),
        grid=(N_TOK // TM_PROJ,),
        in_specs=in_specs,
        out_specs=[row(SWA_QW), row(SWA_KW), row(SWA_VW), row(kvw), row(kvw)],
        out_shape=[jax.ShapeDtypeStruct((N_TOK, SWA_QW), BF16), jax.ShapeDtypeStruct((N_TOK, SWA_KW), BF16),
                   jax.ShapeDtypeStruct((N_TOK, SWA_VW), BF16), jax.ShapeDtypeStruct((N_TOK, kvw), F32),
                   jax.ShapeDtypeStruct((N_TOK, kvw), F32)],
        compiler_params=_params(("arbitrary",)),
        name="swa_proj",
    )(*args)


def _swa_heads(q_ref, sink_ref, o_ref, score_fn, value_fn):
    rows = q_ref.shape[0]
    lane = lax.broadcasted_iota(I32, (rows, LANES), 1)
    low = lane < SWA_HEAD_DIM
    for hk0 in range(0, SWA_KV_HEADS, SWA_GROUP):
        stacked, sinks = None, []
        for hk in range(hk0, hk0 + SWA_GROUP):
            c0 = hk * 2 * LANES
            q2 = jnp.concatenate([q_ref[:, c0:c0 + LANES], q_ref[:, c0 + LANES:c0 + 2 * LANES]], axis=0)
            for par in range(2):
                for pr in range(2):
                    hd = hk * 4 + pr * 2 + par
                    sinks.append(jnp.broadcast_to(sink_ref[:, hd:hd + 1], (rows, 1)))
                blocks = score_fn(hk, par, q2)
                if stacked is None:
                    stacked = [[] for _ in blocks]
                for kind, s in zip(stacked, blocks):
                    kind.append(s)
        scores = [jnp.concatenate(kind, axis=0) for kind in stacked]
        sk = jnp.concatenate(sinks, axis=0)
        m = sk
        for s in scores:
            m = jnp.maximum(m, jnp.max(s, axis=-1, keepdims=True))
        probs = [jnp.exp(s - m) for s in scores]
        l = jnp.exp(sk - m)
        for p in probs:
            l = l + jnp.sum(p, axis=-1, keepdims=True)
        inv = 1.0 / l
        n = 0
        for hk in range(hk0, hk0 + SWA_GROUP):
            res = []
            for par in range(2):
                blk = slice(n * 2 * rows, (n + 1) * 2 * rows)
                res.append(value_fn(hk, [p[blk] for p in probs]) * inv[blk])
                n += 1
            for pr in range(2):
                c0 = hk * 2 * LANES + pr * LANES
                part = slice(pr * rows, (pr + 1) * rows)
                o_ref[:, c0:c0 + LANES] = jnp.where(low, res[0][part], res[1][part]).astype(BF16)


def _swa_attn_ctx_kernel(q_ref, kd_ref, vd_ref, sink_ref, o_ref):
    def score_fn(hk, par, qp):
        a = hk * 2 * LANES + par * LANES
        return [_dot_nt(qp, kd_ref[:, a:a + LANES]) * SWA_SCALE]

    def value_fn(hk, probs):
        return _dot(probs[0].astype(BF16), vd_ref[:, hk * LANES:(hk + 1) * LANES])

    _swa_heads(q_ref, sink_ref, o_ref, score_fn, value_fn)


def _swa_attn_ctx(q, kd, vd, sink):
    blk = lambda w: pl.BlockSpec((CTX_SEQ, w), lambda b: (b, 0))
    return pl.pallas_call(
        _swa_attn_ctx_kernel,
        grid=(CTX_BATCH,),
        in_specs=[blk(SWA_QW), blk(SWA_KW), blk(SWA_VW), pl.BlockSpec((1, SWA_HEADS), lambda b: (0, 0))],
        out_specs=blk(SWA_QW),
        out_shape=jax.ShapeDtypeStruct((N_TOK, SWA_QW), BF16),
        compiler_params=_params(("arbitrary",)),
        name="swa_attn_ctx",
    )(q, kd, vd, sink)


SWA_WIN = 3 * BAND


def _swa_attn_lat_kernel(q_ref, kdl_ref, vdl_ref, kdc_ref, vdc_ref, sink_ref, o_ref):
    nb = pl.program_id(1)
    start = pl.multiple_of(jnp.clip((nb - 1) * BAND, 0, DEC_SEQ - SWA_WIN), BAND)
    qpos = nb * BAND + (lax.broadcasted_iota(I32, (2 * BAND, SWA_WIN), 0) & (BAND - 1))
    kpos = start + lax.broadcasted_iota(I32, (2 * BAND, SWA_WIN), 1)
    valid = jnp.abs(qpos - kpos) <= WINDOW

    def score_fn(hk, par, qp):
        a = hk * 2 * LANES + par * LANES
        kl = kdl_ref[pl.ds(start, SWA_WIN), a:a + LANES]
        sl = jnp.where(valid, _dot_nt(qp, kl) * SWA_SCALE, NEG_INF)
        sc = _dot_nt(qp, kdc_ref[:, a:a + LANES]) * SWA_SCALE
        return [sl, sc]

    def value_fn(hk, probs):
        vl = vdl_ref[pl.ds(start, SWA_WIN), hk * LANES:(hk + 1) * LANES]
        return _dot(probs[0].astype(BF16), vl) + _dot(probs[1].astype(BF16), vdc_ref[:, hk * LANES:(hk + 1) * LANES])

    _swa_heads(q_ref, sink_ref, o_ref, score_fn, value_fn)


def _swa_attn_lat(q, kd, vd, kd_ctx, vd_ctx, sink):
    nbands = DEC_SEQ // BAND
    return pl.pallas_call(
        _swa_attn_lat_kernel,
        grid=(DEC_BATCH, nbands),
        in_specs=[pl.BlockSpec((BAND, SWA_QW), lambda b, i: (b * nbands + i, 0)),
                  pl.BlockSpec((DEC_SEQ, SWA_KW), lambda b, i: (b, 0)),
                  pl.BlockSpec((DEC_SEQ, SWA_VW), lambda b, i: (b, 0)),
                  pl.BlockSpec((PAST_LEN, SWA_KW), lambda b, i: (b, 0)),
                  pl.BlockSpec((PAST_LEN, SWA_VW), lambda b, i: (b, 0)),
                  pl.BlockSpec((1, SWA_HEADS), lambda b, i: (0, 0))],
        out_specs=pl.BlockSpec((BAND, SWA_QW), lambda b, i: (b * nbands + i, 0)),
        out_shape=jax.ShapeDtypeStruct((N_TOK, SWA_QW), BF16),
        compiler_params=_params(("arbitrary", "arbitrary")),
        name="swa_attn_lat",
    )(q, kd, vd, kd_ctx, vd_ctx, sink)


def _post_kernel(a_ref, w_ref, x_ref, g1_ref, sc2_ref, sh2_ref, lng_ref, lnb_ref, rt_ref, x1_ref, h2_ref, lg_ref):
    y = _dot(a_ref[...], w_ref[...])
    x1 = _layer_norm(ALPHA * x_ref[...] + g1_ref[0] * y, lng_ref[...], lnb_ref[...])
    x1_ref[...] = x1
    h2 = x1 * (1.0 + sc2_ref[0]) + sh2_ref[0]
    for j in range(ROW_TILES):
        h2_ref[pl.ds(j, TM, stride=ROW_TILES), :] = h2[:, j * LANES:(j + 1) * LANES]
    rt = rt_ref[...]
    h_hi = h2.astype(BF16)
    h_lo = (h2 - h_hi.astype(F32)).astype(BF16)
    r_hi = rt.astype(BF16)
    r_lo = (rt - r_hi.astype(F32)).astype(BF16)
    both = _dot(h_hi, jnp.concatenate([r_hi, r_lo], axis=1))
    lg_ref[...] = (both[:, :LANES] + both[:, LANES:]) + _dot(h_lo, r_hi)


def _post(a, w_out, x, g1, sc2, sh2, lng, lnb, router_pad, seg_len):
    k = a.shape[1]
    return pl.pallas_call(
        _post_kernel,
        grid=(N_TOK // TM,),
        in_specs=[_row_spec(k), _full_spec((k, D_MODEL)), _row_spec(D_MODEL),
                  _seg_spec(seg_len), _seg_spec(seg_len), _seg_spec(seg_len),
                  _full_spec((1, D_MODEL)), _full_spec((1, D_MODEL)), _full_spec((D_MODEL, LANES))],
        out_specs=[_row_spec(D_MODEL), pl.BlockSpec((TM * ROW_TILES, LANES), lambda r: (r, 0)), _row_spec(LANES)],
        out_shape=[jax.ShapeDtypeStruct((N_TOK, D_MODEL), F32), jax.ShapeDtypeStruct((N_TOK * ROW_TILES, LANES), F32),
                   jax.ShapeDtypeStruct((N_TOK, LANES), F32)],
        compiler_params=_params(("arbitrary",)),
        name="mixer_out",
    )(a, w_out, x, g1, sc2, sh2, lng, lnb, router_pad)


CUM_BLK = 256
REFINE_STEPS = 16
F32_MIN_NORMAL = 1.1754943508222875e-38


def _excl_cumsum_lanes(a, upper):
    outs = []
    carry = jnp.zeros((a.shape[0], 1), F32)
    for b in range(N_TOK // CUM_BLK):
        blk = a[:, b * CUM_BLK:(b + 1) * CUM_BLK]
        outs.append(_dot(blk.astype(BF16), upper) + carry)
        carry = carry + jnp.sum(blk, axis=1, keepdims=True)
    return jnp.concatenate(outs, axis=1)


def _route_kernel(lg_ref, r_ref, tok_ref, roff_ref, idx_ref, rank_scr, w_scr):
    e_rows = N_EXPERTS
    lt = lg_ref[...].T[:e_rows]
    ex = jnp.exp(lt - jnp.max(lt, axis=0, keepdims=True))
    aff = ex / jnp.sum(ex, axis=0, keepdims=True)
    def count_ge(thr):
        return jnp.sum(jnp.where(aff >= thr, 1.0, 0.0), axis=1, keepdims=True)

    cur = jnp.zeros((e_rows, 1), I32)
    for b in range(30, -1, -1):
        cand = cur | jnp.int32(1 << b)
        cur = jnp.where(count_ge(lax.bitcast_convert_type(cand, F32)) >= float(CAP), cand, cur)
    lo = lax.bitcast_convert_type(cur, F32)
    hi = jnp.maximum(lax.bitcast_convert_type(cur + 1, F32), F32_MIN_NORMAL)
    for _ in range(REFINE_STEPS):
        w = hi - lo
        t1, t2, t3 = lo + 0.25 * w, lo + 0.5 * w, lo + 0.75 * w
        ok1, ok2, ok3 = (count_ge(t) >= float(CAP) for t in (t1, t2, t3))
        lo, hi = (jnp.where(ok3, t3, jnp.where(ok2, t2, jnp.where(ok1, t1, lo))),
                  jnp.where(ok1, jnp.where(ok2, jnp.where(ok3, hi, t3), t2), t1))
    gt = aff >= hi
    eq = (aff >= lo) & (aff < hi)
    need = float(CAP) - jnp.sum(jnp.where(gt, 1.0, 0.0), axis=1, keepdims=True)

    ri = lax.broadcasted_iota(I32, (CUM_BLK, CUM_BLK), 0)
    ci = lax.broadcasted_iota(I32, (CUM_BLK, CUM_BLK), 1)
    upper = jnp.where(ri < ci, 1.0, 0.0).astype(BF16)

    tie_rank = _excl_cumsum_lanes(jnp.where(eq, 1.0, 0.0), upper)
    sel = gt | (eq & (tie_rank < need))
    msk = jnp.where(sel, 1.0, 0.0)

    rank = _excl_cumsum_lanes(msk, upper)
    rank_scr[...] = jnp.where(sel, rank, -1.0)

    inv = 1.0 / LANES
    slot_no = jnp.where(sel, rank, -float(LANES))
    slot_hi = jnp.floor(slot_no * inv)
    g0 = aff.astype(BF16).astype(F32)
    r1 = aff - g0
    g1 = r1.astype(BF16).astype(F32)
    g2 = (r1 - g1).astype(BF16).astype(F32)
    tt = lax.broadcasted_iota(I32, (1, N_TOK), 1).astype(F32)
    t_hi = jnp.floor(tt * inv)
    misc = jnp.concatenate([t_hi, tt - t_hi * LANES, jnp.zeros((6, N_TOK), F32)], axis=0)
    feat = jnp.concatenate([slot_hi, slot_no - slot_hi * LANES, g0, g1, g2, misc,
                            jnp.zeros((LANES - 5 * e_rows - 8, N_TOK), F32)], axis=0)
    table = feat.T
    tok_ref[...] = table
    w_scr[...] = table.astype(BF16)

    nblk = N_TOK // TOK_BLK
    bt = lax.broadcasted_iota(I32, (N_TOK, LANES), 0) // TOK_BLK
    bb = lax.broadcasted_iota(I32, (N_TOK, LANES), 1)
    before = jnp.where((bt < bb) & (bb <= nblk), 1.0, 0.0).astype(BF16)
    roff_ref[...] = _dot(msk.astype(BF16), before).astype(I32)

    sub = lax.broadcasted_iota(I32, (8, LANES), 0)
    ln8 = lax.broadcasted_iota(I32, (8, LANES), 1)
    rb_rows = 128

    def per_expert(e, carry):
        rk = rank_scr[pl.ds(e, 1), :]
        want = jnp.where(sub < 5, sub * e_rows + e, sub + (F_T_HI - 5))
        pick = jnp.where(ln8 == want, 1.0, 0.0).astype(BF16)
        for rb in range(CAP // rb_rows):
            slot = (rb * rb_rows + lax.broadcasted_iota(I32, (rb_rows, 1), 0)).astype(F32)
            onehot = jnp.where(rk == slot, 1.0, 0.0).astype(BF16)
            rows = _dot(onehot, w_scr[...])
            r_ref[pl.ds(e, 1), rb * rb_rows:(rb + 1) * rb_rows, :] = rows[None]
            got = _dot_nt(pick, rows.astype(BF16))
            idx_ref[pl.ds(e, 1), :, rb * rb_rows:(rb + 1) * rb_rows] = (got[5:6] * LANES + got[6:7]).astype(I32)[None]
        return carry

    lax.fori_loop(0, e_rows, per_expert, 0)


def _route(logits):
    return pl.pallas_call(
        _route_kernel,
        grid=(1,),
        in_specs=[pl.BlockSpec((N_TOK, LANES), lambda i: (0, 0))],
        out_specs=[pl.BlockSpec((N_EXPERTS, CAP, LANES), lambda i: (0, 0, 0)),
                   pl.BlockSpec((N_TOK, LANES), lambda i: (0, 0)),
                   pl.BlockSpec((N_EXPERTS, LANES), lambda i: (0, 0)),
                   pl.BlockSpec((N_EXPERTS, 1, CAP), lambda i: (0, 0, 0))],
        out_shape=[jax.ShapeDtypeStruct((N_EXPERTS, CAP, LANES), F32),
                   jax.ShapeDtypeStruct((N_TOK, LANES), F32),
                   jax.ShapeDtypeStruct((N_EXPERTS, LANES), I32),
                   jax.ShapeDtypeStruct((N_EXPERTS, 1, CAP), I32)],
        scratch_shapes=[pltpu.VMEM((N_EXPERTS, N_TOK), F32), pltpu.VMEM((N_TOK, LANES), BF16)],
        compiler_params=_params(("arbitrary",)),
        name="moe_route",
    )(logits)


ROWS_E = 2 * CAP
N_FT = EXPERT_FF // FF_TILE
GATHER_UNROLL = 8
PIECE = 256
N_PIECE = FF_TILE // PIECE


def _tile(row):
    return pl.ds(pl.multiple_of(row * ROW_TILES, ROW_TILES), ROW_TILES)


def _ffn_kernel(idx_s, hc_hbm, hl_hbm, rc_ref, rl_ref, wg_ref, wu_ref, wd_ref, y_ref, xe, xb, acc, gsem):
    e = pl.program_id(0)
    f = pl.program_id(1)

    def gather_rows(ee):
        base = ee * ROWS_E

        def body(r, carry):
            tc = idx_s[base + r]
            pltpu.make_async_copy(hc_hbm.at[_tile(tc)], xe.at[_tile(r)], gsem.at[0]).start(priority=0)
            tl = idx_s[base + CAP + r]
            pltpu.make_async_copy(hl_hbm.at[_tile(tl)], xe.at[_tile(CAP + r)], gsem.at[0]).start(priority=1)
            return carry

        lax.fori_loop(0, CAP, body, 0, unroll=GATHER_UNROLL)

    @pl.when(f == 0)
    def _():
        @pl.when(e == 0)
        def _():
            gather_rows(0)

        pltpu.make_async_copy(xe, xe, gsem.at[0]).wait()
        for j in range(ROW_TILES):
            xb[:, j * LANES:(j + 1) * LANES] = xe[pl.ds(j, ROWS_E, stride=ROW_TILES), :].astype(BF16)

        @pl.when(e + 1 < N_EXPERTS)
        def _():
            gather_rows(e + 1)

        acc[...] = jnp.zeros_like(acc)

    x = xb[...]
    for p in range(N_PIECE):
        cols = slice(p * PIECE, (p + 1) * PIECE)
        g = _dot(x, wg_ref[0, :, cols].astype(BF16))
        u = _dot(x, wu_ref[0, :, cols].astype(BF16))
        hid = (g * _sigmoid(g) * u).astype(BF16)
        acc[...] += _dot(hid, wd_ref[0, cols, :].astype(BF16))

    @pl.when(f == N_FT - 1)
    def _():
        lane = lax.broadcasted_iota(I32, (CAP, LANES), 1)
        mine = (lane == F_G0 + e) | (lane == F_G1 + e) | (lane == F_G2 + e)
        gate = jnp.concatenate(
            [jnp.sum(jnp.where(mine, rc_ref[0], 0.0), axis=1, keepdims=True),
             jnp.sum(jnp.where(mine, rl_ref[0], 0.0), axis=1, keepdims=True)], axis=0)
        y_ref[0] = (acc[...] * gate).astype(BF16)


def _moe_ffn(layer, idx_all, h_ctx, h_lat, r_ctx, r_lat, w_gate, w_up, w_down):
    any_spec = pl.BlockSpec(memory_space=pl.ANY)
    e0 = layer * N_EXPERTS
    grid_spec = pltpu.PrefetchScalarGridSpec(
        num_scalar_prefetch=1,
        grid=(N_EXPERTS, N_FT),
        in_specs=[any_spec, any_spec,
                  pl.BlockSpec((1, CAP, LANES), lambda e, f, *_: (e, 0, 0)),
                  pl.BlockSpec((1, CAP, LANES), lambda e, f, *_: (e, 0, 0)),
                  pl.BlockSpec((1, D_MODEL, FF_TILE), lambda e, f, *_: (e0 + e, 0, f)),
                  pl.BlockSpec((1, D_MODEL, FF_TILE), lambda e, f, *_: (e0 + e, 0, f)),
                  pl.BlockSpec((1, FF_TILE, D_MODEL), lambda e, f, *_: (e0 + e, f, 0))],
        out_specs=pl.BlockSpec((1, ROWS_E, D_MODEL), lambda e, f, *_: (e, 0, 0)),
        scratch_shapes=[pltpu.VMEM((ROWS_E * ROW_TILES, LANES), F32), pltpu.VMEM((ROWS_E, D_MODEL), BF16),
                        pltpu.VMEM((ROWS_E, D_MODEL), F32), pltpu.SemaphoreType.DMA((1,))],
    )
    return pl.pallas_call(
        _ffn_kernel,
        grid_spec=grid_spec,
        out_shape=jax.ShapeDtypeStruct((N_EXPERTS, ROWS_E, D_MODEL), BF16),
        compiler_params=_params(("arbitrary", "arbitrary")),
        name="moe_ffn",
    )(idx_all, h_ctx, h_lat, r_ctx, r_lat, w_gate, w_up, w_down)


WIN_E = 64
EXPERTS_PER_DOT = 4
N_TBLK = N_TOK // TOK_BLK
ROFF_STRIDE = LANES


def _combine_kernel(group, roff_s, y_hbm, tok_ref, x_ref, g2_ref, lng_ref, lnb_ref, o_ref, ywin, yext, acc, wsem, esem):
    b = pl.program_id(0)

    def win_start(e, bb):
        first = roff_s[e * ROFF_STRIDE + bb]
        return jnp.minimum((first // BF16_ROWS) * BF16_ROWS, CAP - WIN_E)

    def rows_at(start):
        return pl.ds(pl.multiple_of(group * CAP + start, BF16_ROWS), WIN_E)

    def start_windows(bb, slot):
        for e in range(N_EXPERTS):
            pltpu.make_async_copy(y_hbm.at[e, rows_at(win_start(e, bb))], ywin.at[slot, e], wsem.at[slot]).start()

    @pl.when(b == 0)
    def _():
        start_windows(0, 0)

    @pl.when(b + 1 < N_TBLK)
    def _():
        start_windows(b + 1, (b + 1) % 2)

    slot = b % 2
    tok = tok_ref[...]
    lane = lax.broadcasted_iota(I32, (TOK_BLK, WIN_E), 1)
    owner = [tok[:, F_SLOT_HI + e:F_SLOT_HI + e + 1] * LANES + tok[:, F_SLOT_LO + e:F_SLOT_LO + e + 1]
             for e in range(N_EXPERTS)]
    starts = [win_start(e, b) for e in range(N_EXPERTS)]

    pltpu.make_async_copy(ywin.at[slot], ywin.at[slot], wsem.at[slot]).wait()
    wide = lax.broadcasted_iota(I32, (TOK_BLK, EXPERTS_PER_DOT * WIN_E), 1)
    total = None
    for e0 in range(0, N_EXPERTS, EXPERTS_PER_DOT):
        own, slots = owner[e0], (starts[e0] + wide).astype(F32)
        for k in range(1, EXPERTS_PER_DOT):
            later = wide >= k * WIN_E
            own = jnp.where(later, owner[e0 + k], own)
            slots = jnp.where(later, (starts[e0 + k] - k * WIN_E + wide).astype(F32), slots)
        seg = jnp.where(own == slots, 1.0, 0.0).astype(BF16)
        rows = ywin[slot, e0:e0 + EXPERTS_PER_DOT].reshape(EXPERTS_PER_DOT * WIN_E, D_MODEL)
        part = _dot(seg, rows)
        total = part if total is None else total + part
    acc[...] = total

    for e in range(N_EXPERTS):
        past = jnp.maximum(roff_s[e * ROFF_STRIDE + b + 1] - (starts[e] + WIN_E), 0)

        def over(k, carry, e=e):
            nominal = starts[e] + (k + 1) * WIN_E
            first = jnp.minimum(nominal, CAP - WIN_E)
            cp = pltpu.make_async_copy(y_hbm.at[e, rows_at(first)], yext, esem.at[0])
            cp.start()
            cp.wait()
            slots = first + lane
            seg = jnp.where((owner[e] == slots.astype(F32)) & (slots >= nominal), 1.0, 0.0).astype(BF16)
            acc[...] += _dot(seg, yext[...])
            return carry

        lax.fori_loop(0, (past + WIN_E - 1) // WIN_E, over, 0)

    o_ref[...] = _layer_norm(ALPHA * x_ref[...] + g2_ref[0] * acc[...], lng_ref[...], lnb_ref[...])


def _combine(group, roff, y, tok, x1, g2, lng, lnb, seg_len):
    grid_spec = pltpu.PrefetchScalarGridSpec(
        num_scalar_prefetch=1,
        grid=(N_TBLK,),
        in_specs=[pl.BlockSpec(memory_space=pl.ANY),
                  pl.BlockSpec((TOK_BLK, LANES), lambda b, *_: (b, 0)),
                  pl.BlockSpec((TOK_BLK, D_MODEL), lambda b, *_: (b, 0)),
                  pl.BlockSpec((1, 1, D_MODEL), lambda b, *_: (b * TOK_BLK // seg_len, 0, 0)),
                  pl.BlockSpec((1, D_MODEL), lambda b, *_: (0, 0)),
                  pl.BlockSpec((1, D_MODEL), lambda b, *_: (0, 0))],
        out_specs=pl.BlockSpec((TOK_BLK, D_MODEL), lambda b, *_: (b, 0)),
        scratch_shapes=[pltpu.VMEM((2, N_EXPERTS, WIN_E, D_MODEL), BF16), pltpu.VMEM((WIN_E, D_MODEL), BF16),
                        pltpu.VMEM((TOK_BLK, D_MODEL), F32),
                        pltpu.SemaphoreType.DMA((2,)), pltpu.SemaphoreType.DMA((1,))],
    )
    return pl.pallas_call(
        functools.partial(_combine_kernel, group),
        grid_spec=grid_spec,
        out_shape=jax.ShapeDtypeStruct((N_TOK, D_MODEL), F32),
        compiler_params=_params(("arbitrary",)),
        name="moe_combine",
    )(roff, y, tok, x1, g2, lng, lnb)


def _rope_tables():
    t = jnp.arange(DEC_SEQ)
    n_freq = MLA_ROPE // 4
    inv_freq = ROPE_BASE ** (-jnp.arange(n_freq, dtype=F32) / n_freq)
    ang_r = (t // GRID_W).astype(F32)[:, None] * inv_freq[None, :]
    ang_c = (t % GRID_W).astype(F32)[:, None] * inv_freq[None, :]
    cos64 = jnp.concatenate([jnp.cos(ang_r)] * 2 + [jnp.cos(ang_c)] * 2, axis=1)
    sin64 = jnp.concatenate([jnp.sin(ang_r)] * 2 + [jnp.sin(ang_c)] * 2, axis=1)
    one, zero = jnp.ones_like(cos64), jnp.zeros_like(sin64)
    mla = (jnp.concatenate([cos64, one], axis=1), jnp.concatenate([sin64, zero], axis=1))
    swa = (jnp.concatenate([cos64, cos64], axis=1), jnp.concatenate([sin64, sin64], axis=1))
    return mla, swa


def _swa_key_layout(k):
    z = jnp.zeros_like(k)
    return jnp.stack([k, z, z, k], axis=2).reshape(k.shape[0], SWA_KW)


def _swa_value_layout(v):
    return jnp.stack([v, v], axis=2).reshape(v.shape[0], SWA_VW)


def kernel(x_prompt, x_sample, cache_mla_ckv, cache_mla_kpe, cache_swa_k, cache_swa_v, c, c_ctx, mod_w, mod_b, ln_gain, ln_bias, mla_w_in, mla_q_gain, mla_kv_gain, mla_w_q_up, mla_w_kv_up, mla_w_out, gm_w_in, gm_v_gain, gm_w_s, gm_b_s, gm_w_out, swa_w_qkv, swa_sink, swa_w_out, moe_router, moe_w_gate, moe_w_up, moe_w_down):
    d = D_MODEL
    xs = [x_prompt.reshape(N_TOK, d), x_sample.reshape(N_TOK, d)]
    seg_lens = [N_TOK, DEC_SEQ]
    cond8 = jnp.concatenate([c_ctx[None, :], c, jnp.zeros((3, d), F32)], axis=0)
    mod = _modulation_all(cond8, mod_w, mod_b)
    rope_mla, rope_swa = _rope_tables()
    w_gate_all = moe_w_gate.reshape(DEPTH * N_EXPERTS, d, EXPERT_FF)
    w_up_all = moe_w_up.reshape(DEPTH * N_EXPERTS, d, EXPERT_FF)
    w_down_all = moe_w_down.reshape(DEPTH * N_EXPERTS, EXPERT_FF, d)

    def mod_rows(i, q):
        m = mod[i, :, q * d:(q + 1) * d]
        return [m[0:1].reshape(1, 1, d), m[1:1 + DEC_BATCH].reshape(DEC_BATCH, 1, d)]

    ckv_out, kpe_out, k_out, v_out = [], [], [], []
    for i in range(DEPTH):
        kind, j = i % 3, i // 3
        sh1, sc1, g1, sh2, sc2, g2 = (mod_rows(i, q) for q in range(6))
        router_pad = jnp.pad(moe_router[i], ((0, 0), (0, LANES - N_EXPERTS)))
        lng = ln_gain[i].reshape(2, 1, d)
        lnb = ln_bias[i].reshape(2, 1, d)

        mixed = []
        if kind == 0:
            w = {
                "w_in": jnp.pad(mla_w_in[j], ((0, 0), (0, MLA_ROPE))).astype(BF16),
                "q_gain": mla_q_gain[j][None, :], "kv_gain": mla_kv_gain[j][None, :],
                "w_q": jnp.pad(mla_w_q_up[j], ((0, 0), (0, 0), (0, MLA_QK_PAD - MLA_NOPE - MLA_ROPE)))
                .reshape(MLA_Q_LORA, MLA_HEADS * MLA_QK_PAD).astype(BF16),
                "w_kv": mla_w_kv_up[j].reshape(MLA_KV_LORA, MLA_HEADS * (MLA_NOPE + MLA_V)).astype(BF16),
            }
            w_out = mla_w_out[j].astype(BF16)
            q, kv, kpe, ckv, kraw = _mla_proj(xs[0], sc1[0], sh1[0], w, seg_lens[0], None)
            ckv_out.append(ckv.reshape(CTX_BATCH, CTX_SEQ, MLA_KV_LORA))
            kpe_out.append(kraw.reshape(CTX_BATCH, CTX_SEQ, MLA_ROPE))
            mixed.append(_mla_attn_ctx(q, kv, kpe))
            q, kv, kpe, _, _ = _mla_proj(xs[1], sc1[1], sh1[1], w, seg_lens[1], rope_mla)
            kv_ctx = _matmul_bf16(cache_mla_ckv[:, j].reshape(DEC_BATCH * PAST_LEN, MLA_KV_LORA).astype(BF16), w["w_kv"])
            kpe_ctx = jnp.pad(cache_mla_kpe[:, j].reshape(DEC_BATCH * PAST_LEN, MLA_ROPE),
                              ((0, 0), (0, LANES - MLA_ROPE))).astype(BF16)
            mixed.append(_mla_attn_lat(q, kv, kpe, kv_ctx, kpe_ctx))
        elif kind == 1:
            w = {"w_in": gm_w_in[j].astype(BF16), "v_gain": gm_v_gain[j][None, :],
                 "w_s": gm_w_s[j].astype(BF16), "b_s": gm_b_s[j][:, :, None]}
            w_out = gm_w_out[j].astype(BF16)
            for g in range(2):
                mixed.append(_gmlp_in(xs[g], sc1[g], sh1[g], w, seg_lens[g]))
        else:
            wq = swa_w_qkv[j]
            nq, nkv = SWA_HEADS * SWA_HEAD_DIM, SWA_KV_HEADS * SWA_HEAD_DIM
            w_ext = jnp.concatenate(
                [wq[:, :nq],
                 _swa_key_layout(wq[:, nq:nq + nkv].reshape(d, SWA_KV_HEADS, SWA_HEAD_DIM)),
                 _swa_value_layout(wq[:, nq + nkv:].reshape(d, SWA_KV_HEADS, SWA_HEAD_DIM))], axis=1).astype(BF16)
            w_out = swa_w_out[j].astype(BF16)
            sink = swa_sink[j][None, :]
            q, kd, vd, kraw, vraw = _swa_proj(xs[0], sc1[0], sh1[0], w_ext, seg_lens[0], None)
            k_out.append(kraw.reshape(CTX_BATCH, CTX_SEQ, SWA_KV_HEADS, SWA_HEAD_DIM))
            v_out.append(vraw.reshape(CTX_BATCH, CTX_SEQ, SWA_KV_HEADS, SWA_HEAD_DIM))
            mixed.append(_swa_attn_ctx(q, kd, vd, sink))
            q, kd, vd, _, _ = _swa_proj(xs[1], sc1[1], sh1[1], w_ext, seg_lens[1], rope_swa)
            rows = DEC_BATCH * PAST_LEN
            kd_ctx = _swa_key_layout(cache_swa_k[:, j].reshape(rows, SWA_KV_HEADS, SWA_HEAD_DIM)).astype(BF16)
            vd_ctx = _swa_value_layout(cache_swa_v[:, j].reshape(rows, SWA_KV_HEADS, SWA_HEAD_DIM)).astype(BF16)
            mixed.append(_swa_attn_lat(q, kd, vd, kd_ctx, vd_ctx, sink))

        x1, h2, routed = [], [], []
        for g in range(2):
            x1_g, h2_g, lg_g = _post(mixed[g], w_out, xs[g], g1[g], sc2[g], sh2[g], lng[0], lnb[0], router_pad, seg_lens[g])
            x1.append(x1_g)
            h2.append(h2_g)
            routed.append(_route(lg_g))
        idx_all = jnp.concatenate([routed[0][3], routed[1][3]], axis=2).reshape(-1)
        y = _moe_ffn(i, idx_all, h2[0], h2[1], routed[0][0], routed[1][0], w_gate_all, w_up_all, w_down_all)
        for g in range(2):
            roff = routed[g][2].reshape(-1)
            xs[g] = _combine(g, roff, y, routed[g][1], x1[g], g2[g], lng[1], lnb[1], seg_lens[g])

    return (xs[0].reshape(CTX_BATCH, CTX_SEQ, d), xs[1].reshape(DEC_BATCH, DEC_SEQ, d),
            jnp.stack(ckv_out, axis=1), jnp.stack(kpe_out, axis=1),
            jnp.stack(k_out, axis=1), jnp.stack(v_out, axis=1))
```
